```python
import jax, jax.numpy as jnp
from jax import lax
import numpy as np

D_MODEL = 1024
BATCH = 2
SEQ = 8192
DEPTH = 1

HEAD_DIM = 128
ATT_HEADS = 4
DILATED_PAIRS = ((128, 1), (512, 4), (2048, 16))
N_DIL_GROUPS = len(DILATED_PAIRS)
Q_W = N_DIL_GROUPS * ATT_HEADS * HEAD_DIM
KV_W = ATT_HEADS * HEAD_DIM
POOL_WINDOWS = (2, 4, 8, 16)
N_POOL_GROUPS = len(POOL_WINDOWS)
POOL_GROUP_DIM = 128
POOL_W = N_POOL_GROUPS * POOL_GROUP_DIM
MIX_W = KV_W + POOL_W
IN_W = Q_W + 2 * KV_W + POOL_W
BLOCK = 128
ROT_DIM = HEAD_DIM // 4
ROT_HALF = ROT_DIM // 2
ROPE_THETA = 500000.0
N_MEM = 256
X_HEADS = 4
X_W = X_HEADS * HEAD_DIM
D_FF = ((8 * D_MODEL // 3 + 255) // 256) * 256
EPS = 1e-6
NEG_INF = -1e30

kernel_name = "hybrid_pool_dilated_attn_block"


def rms_norm(x, g):
    xf = x.astype(jnp.float32)
    y = xf * lax.rsqrt(jnp.mean(xf * xf, axis=-1, keepdims=True) + EPS)
    return (y * g.astype(jnp.float32)).astype(x.dtype)


def rope_partial(t, cos, sin):
    ex = tuple(range(2, t.ndim - 1))
    c = jnp.expand_dims(cos, ex)
    s = jnp.expand_dims(sin, ex)
    tr = t[..., :ROT_DIM].astype(jnp.float32)
    x1, x2 = tr[..., :ROT_HALF], tr[..., ROT_HALF:]
    rot = jnp.concatenate([x1 * c - x2 * s, x2 * c + x1 * s], axis=-1).astype(t.dtype)
    return jnp.concatenate([rot, t[..., ROT_DIM:]], axis=-1)


def dilated_branch(q, k, v, window, dilation):
    B, S, H, D = q.shape
    n_back = window // dilation
    L = S // dilation
    nb = -(-L // BLOCK)
    Lp = nb * BLOCK

    def to_sub(t):
        t = t.astype(jnp.float32).reshape(B, L, dilation, H, D).transpose(0, 2, 3, 1, 4)
        return jnp.pad(t, ((0, 0), (0, 0), (0, 0), (0, Lp - L), (0, 0)))

    def windows(t):
        tp = jnp.pad(t, ((0, 0), (0, 0), (0, 0), (BLOCK, 0), (0, 0)))
        tp = tp.reshape(B, dilation, H, nb + 1, BLOCK, D)
        return jnp.concatenate([tp[:, :, :, :-1], tp[:, :, :, 1:]], axis=4)

    qb = to_sub(q).reshape(B, dilation, H, nb, BLOCK, D)
    kw = windows(to_sub(k))
    vw = windows(to_sub(v))
    s = jnp.einsum('bdhnqc,bdhnkc->bdhnqk', qb, kw)
    qi = jnp.arange(BLOCK)[:, None]
    kj = jnp.arange(2 * BLOCK)[None, :]
    delta = qi + BLOCK - kj
    key_idx = jnp.arange(nb)[:, None, None] * BLOCK - BLOCK + kj[None]
    valid = (delta >= 0) & (delta <= n_back) & (key_idx >= 0)
    s = jnp.where(valid, s, NEG_INF)
    m = jnp.max(s, axis=-1)
    p = jnp.exp(s - m[..., None])
    l = jnp.sum(p, axis=-1)
    acc = jnp.einsum('bdhnqk,bdhnkc->bdhnqc', p, vw)
    acc = acc.reshape(B, dilation, H, Lp, D)[:, :, :, :L].transpose(0, 3, 1, 2, 4).reshape(B, S, H, D)
    m = m.reshape(B, dilation, H, Lp)[..., :L].transpose(0, 3, 1, 2).reshape(B, S, H)
    l = l.reshape(B, dilation, H, Lp)[..., :L].transpose(0, 3, 1, 2).reshape(B, S, H)
    return acc, m, l


def pool_mixer(u, pool_w, pool_scale):
    B, S, _ = u.shape
    uf = u.astype(jnp.float32).reshape(B, S, N_POOL_GROUPS, POOL_GROUP_DIM)
    c = jnp.cumsum(uf, axis=1)
    t = jnp.arange(S)
    outs = []
    for g, w in enumerate(POOL_WINDOWS):
        cg = c[:, :, g]
        shifted = jnp.pad(cg, ((0, 0), (w, 0), (0, 0)))[:, :S]
        count = jnp.minimum(t + 1, w).astype(jnp.float32)[None, :, None]
        outs.append((cg - shifted) / count - uf[:, :, g])
    d = jnp.stack(outs, axis=2).astype(u.dtype)
    y = jnp.einsum('bsgc,gce->bsge', d, pool_w).reshape(B, S, POOL_W)
    return y * pool_scale


def parallel_mixer(xn, cos, sin, w_in, q_norm_g, k_norm_g, pool_w, pool_scale, w_out):
    B, S, _ = xn.shape
    proj = xn @ w_in
    q, k, v, u = jnp.split(proj, [Q_W, Q_W + KV_W, Q_W + 2 * KV_W], axis=-1)
    q = q.reshape(B, S, N_DIL_GROUPS, ATT_HEADS, HEAD_DIM)
    k = k.reshape(B, S, ATT_HEADS, HEAD_DIM)
    v = v.reshape(B, S, ATT_HEADS, HEAD_DIM)
    q = rope_partial(rms_norm(q, q_norm_g), cos, sin) * (HEAD_DIM ** -0.5)
    k = rope_partial(rms_norm(k, k_norm_g), cos, sin)
    accs, ms, ls = [], [], []
    for g, (window, dilation) in enumerate(DILATED_PAIRS):
        a, m, l = dilated_branch(q[:, :, g], k, v, window, dilation)
        accs.append(a); ms.append(m); ls.append(l)
    ms = jnp.stack(ms)
    wts = jnp.exp(ms - jnp.max(ms, axis=0, keepdims=True))
    num = jnp.sum(wts[..., None] * jnp.stack(accs), axis=0)
    den = jnp.sum(wts * jnp.stack(ls), axis=0)
    attn = (num / den[..., None]).astype(xn.dtype).reshape(B, S, KV_W)
    pooled = pool_mixer(u, pool_w, pool_scale)
    return jnp.concatenate([attn, pooled], axis=-1) @ w_out


def memory_cross_attention(hn, mem_n, w_cq, w_ckv, cq_norm_g, ck_norm_g, w_co):
    B, S, _ = hn.shape
    M = mem_n.shape[1]
    q = (hn @ w_cq).reshape(B, S, X_HEADS, HEAD_DIM)
    k, v = jnp.split(mem_n @ w_ckv, 2, axis=-1)
    k = k.reshape(B, M, X_HEADS, HEAD_DIM)
    v = v.reshape(B, M, X_HEADS, HEAD_DIM)
    q = rms_norm(q, cq_norm_g).astype(jnp.float32) * (HEAD_DIM ** -0.5)
    k = rms_norm(k, ck_norm_g).astype(jnp.float32)
    p = jax.nn.softmax(jnp.einsum('bshd,bmhd->bhsm', q, k), axis=-1)
    o = jnp.einsum('bhsm,bmhd->bshd', p, v.astype(jnp.float32)).astype(hn.dtype)
    return o.reshape(B, S, X_W) @ w_co


def swiglu_ffn(hn, w_gate_up, w_down):
    g, u = jnp.split(hn @ w_gate_up, 2, axis=-1)
    return (jax.nn.silu(g) * u) @ w_down


def setup_inputs(seed: int = 0) -> dict:
    key = jax.random.key(seed)
    ks = jax.random.split(key, 24)
    f32 = jnp.float32

    def w(k, shape, fan_in):
        return jax.random.normal(k, shape, f32) * (fan_in ** -0.5)

    def gain(k, shape):
        return 1.0 + 0.02 * jax.random.normal(k, shape, f32)

    x = jax.random.normal(ks[0], (BATCH, SEQ, D_MODEL), f32)
    mem = jax.random.normal(ks[1], (BATCH, N_MEM, D_MODEL), f32)
    offset = jax.random.randint(ks[2], (BATCH, 1), 0, 4096, dtype=jnp.int32)
    positions = jnp.arange(SEQ, dtype=jnp.int32)[None, :] + offset
    return {
        "x": x,
        "mem": mem,
        "positions": positions,
        "mix_norm_g": gain(ks[3], (DEPTH, D_MODEL)),
        "w_in": w(ks[4], (DEPTH, D_MODEL, IN_W), D_MODEL),
        "q_norm_g": gain(ks[5], (DEPTH, HEAD_DIM)),
        "k_norm_g": gain(ks[6], (DEPTH, HEAD_DIM)),
        "pool_w": w(ks[7], (DEPTH, N_POOL_GROUPS, POOL_GROUP_DIM, POOL_GROUP_DIM), POOL_GROUP_DIM),
        "pool_scale": gain(ks[8], (DEPTH, POOL_W)),
        "w_out": w(ks[9], (DEPTH, MIX_W, D_MODEL), MIX_W),
        "cross_norm_g": gain(ks[10], (DEPTH, D_MODEL)),
        "mem_norm_g": gain(ks[11], (DEPTH, D_MODEL)),
        "w_cq": w(ks[12], (DEPTH, D_MODEL, X_W), D_MODEL),
        "w_ckv": w(ks[13], (DEPTH, D_MODEL, 2 * X_W), D_MODEL),
        "cq_norm_g": gain(ks[14], (DEPTH, HEAD_DIM)),
        "ck_norm_g": gain(ks[15], (DEPTH, HEAD_DIM)),
        "w_co": w(ks[16], (DEPTH, X_W, D_MODEL), X_W),
        "ffn_norm_g": gain(ks[17], (DEPTH, D_MODEL)),
        "w_gate_up": w(ks[18], (DEPTH, D_MODEL, 2 * D_FF), D_MODEL),
        "w_down": w(ks[19], (DEPTH, D_FF, D_MODEL), D_FF),
    }


def reference(x, mem, positions, mix_norm_g, w_in, q_norm_g, k_norm_g, pool_w, pool_scale, w_out,
              cross_norm_g, mem_norm_g, w_cq, w_ckv, cq_norm_g, ck_norm_g, w_co,
              ffn_norm_g, w_gate_up, w_down):
    inv_freq = ROPE_THETA ** (-jnp.arange(0, ROT_DIM, 2, dtype=jnp.float32) / ROT_DIM)
    ang = positions.astype(jnp.float32)[..., None] * inv_freq
    cos, sin = jnp.cos(ang), jnp.sin(ang)
    h = x
    for layer in range(DEPTH):
        h = h + parallel_mixer(rms_norm(h, mix_norm_g[layer]), cos, sin, w_in[layer],
                               q_norm_g[layer], k_norm_g[layer], pool_w[layer],
                               pool_scale[layer], w_out[layer])
        h = h + memory_cross_attention(rms_norm(h, cross_norm_g[layer]),
                                       rms_norm(mem, mem_norm_g[layer]), w_cq[layer],
                                       w_ckv[layer], cq_norm_g[layer], ck_norm_g[layer],
                                       w_co[layer])
        h = h + swiglu_ffn(rms_norm(h, ffn_norm_g[layer]), w_gate_up[layer], w_down[layer])
    return h
```

```python
import functools

import jax
import jax.numpy as jnp
from jax import lax
from jax.experimental import pallas as pl
from jax.experimental.pallas import tpu as pltpu

D_MODEL = 1024
HEAD_DIM = 128
ATT_HEADS = 4
DILATED_PAIRS = ((128, 1), (512, 4), (2048, 16))
KV_W = ATT_HEADS * HEAD_DIM
POOL_WINDOWS = (2, 4, 8, 16)
POOL_W = len(POOL_WINDOWS) * HEAD_DIM
ROT_DIM = HEAD_DIM // 4
ROT_HALF = ROT_DIM // 2
ROPE_THETA = 500000.0
X_HEADS = 4
X_W = X_HEADS * HEAD_DIM
D_FF = 2816
EPS = 1e-6
NEG_INF = -1e30
ATT_BLOCK = 128
POOL_HALO = 16
LANES = 128

F32 = jnp.float32
BF16 = jnp.bfloat16

VMEM_LIMIT = 56 * 1024 * 1024


def _rms(x, g):
    ms = jnp.mean(x * x, axis=-1, keepdims=True)
    return x * lax.rsqrt(ms + EPS) * g


def _resident(shape):
    nd = len(shape)
    return pl.BlockSpec(shape, lambda *_: (0,) * nd, pipeline_mode=pl.Buffered(1))


def _rope_table_kernel(pos_ref, invf_ref, cos_ref, sin_ref):
    ang = pos_ref[...] * invf_ref[...]
    cos_ref[...] = jnp.cos(ang)
    sin_ref[...] = jnp.sin(ang)


def _rope_tables(positions):
    n_tok = positions.size
    per_row = LANES // ROT_HALF
    rows = n_tok // per_row
    pos = jnp.repeat(positions.reshape(-1).astype(F32), ROT_HALF).reshape(rows, LANES)
    inv_freq = ROPE_THETA ** (-jnp.arange(0, ROT_DIM, 2, dtype=F32) / ROT_DIM)
    invf = jnp.tile(inv_freq, per_row).reshape(1, LANES)
    blk = 256
    cos, sin = pl.pallas_call(
        _rope_table_kernel,
        grid=(rows // blk,),
        in_specs=[pl.BlockSpec((blk, LANES), lambda i: (i, 0)),
                  pl.BlockSpec((1, LANES), lambda i: (0, 0))],
        out_specs=[pl.BlockSpec((blk, LANES), lambda i: (i, 0))] * 2,
        out_shape=[jax.ShapeDtypeStruct((rows, LANES), F32)] * 2,
        name="rope_tables",
    )(pos, invf)
    return cos.reshape(n_tok, ROT_HALF), sin.reshape(n_tok, ROT_HALF)


def _in_proj_kernel(x_ref, g_ref, w_ref, qg_ref, kg_ref, cos_ref, sin_ref, pw_ref, ps_ref,
                    q1_ref, q2_ref, q3_ref, k_ref, v_ref, y_ref, halo_ref, *, tm, tiles_per_seq):
    xn = _rms(x_ref[...], g_ref[...]).astype(BF16)

    cs = cos_ref[...]
    sn = sin_ref[...]
    rest = LANES - ROT_DIM
    c_tab = jnp.concatenate([cs, cs, jnp.ones((tm, rest), F32)], axis=-1)
    sa_tab = jnp.concatenate([-sn, jnp.zeros((tm, LANES - ROT_HALF), F32)], axis=-1)
    sb_tab = jnp.concatenate([jnp.zeros((tm, ROT_HALF), F32), sn, jnp.zeros((tm, rest), F32)], axis=-1)

    def norm_rope(t, g):
        y = _rms(t, g)
        return (y * c_tab + pltpu.roll(y, LANES - ROT_HALF, 1) * sa_tab
                + pltpu.roll(y, ROT_HALF, 1) * sb_tab)

    def proj(c):
        return jnp.dot(xn, w_ref[:, c * KV_W:(c + 1) * KV_W], preferred_element_type=F32)

    for c, (out_ref, gain_ref) in enumerate(((q1_ref, qg_ref), (q2_ref, qg_ref), (q3_ref, qg_ref),
                                             (k_ref, kg_ref))):
        p = proj(c)
        for h in range(ATT_HEADS):
            sl = slice(h * HEAD_DIM, (h + 1) * HEAD_DIM)
            out_ref[:, sl] = norm_rope(p[:, sl], gain_ref[...]).astype(BF16)

    v_ref[...] = proj(4).astype(BF16)

    u = proj(5)
    seq_tile = pl.program_id(0) % tiles_per_seq

    @pl.when(seq_tile == 0)
    def _():
        halo_ref[0:POOL_HALO, :] = jnp.zeros((POOL_HALO, POOL_W), F32)

    halo_ref[POOL_HALO:POOL_HALO + tm, :] = u
    tpos = seq_tile * tm + lax.broadcasted_iota(jnp.int32, (tm, 1), 0)
    for g, w in enumerate(POOL_WINDOWS):
        sl = slice(g * HEAD_DIM, (g + 1) * HEAD_DIM)
        ug = u[:, sl]
        acc = ug
        for j in range(1, w):
            acc = acc + halo_ref[POOL_HALO - j:POOL_HALO - j + tm, sl]
        count = jnp.minimum(tpos + 1, w).astype(F32)
        d = acc / count - ug
        yg = jnp.dot(d.astype(BF16), pw_ref[g], preferred_element_type=F32) * ps_ref[:, sl]
        y_ref[:, sl] = yg.astype(BF16)
    halo_ref[0:POOL_HALO, :] = halo_ref[tm:tm + POOL_HALO, :]


def _in_proj(x2, g_mix, w_in, qg, kg, cos, sin, pool_w, pool_scale, seq):
    n_tok = x2.shape[0]
    tm = 512
    tiles_per_seq = seq // tm
    in_w = w_in.shape[1]
    tok = lambda w: pl.BlockSpec((tm, w), lambda i: (i, 0))
    out_sds = jax.ShapeDtypeStruct((n_tok, KV_W), BF16)
    return pl.pallas_call(
        functools.partial(_in_proj_kernel, tm=tm, tiles_per_seq=tiles_per_seq),
        grid=(n_tok // tm,),
        in_specs=[tok(D_MODEL), _resident((1, D_MODEL)), _resident((D_MODEL, in_w)),
                  _resident((1, HEAD_DIM)), _resident((1, HEAD_DIM)),
                  tok(ROT_HALF), tok(ROT_HALF),
                  _resident((len(POOL_WINDOWS), HEAD_DIM, HEAD_DIM)), _resident((1, POOL_W))],
        out_specs=[tok(KV_W)] * 6,
        out_shape=[out_sds] * 6,
        scratch_shapes=[pltpu.VMEM((POOL_HALO + tm, POOL_W), F32)],
        compiler_params=pltpu.CompilerParams(dimension_semantics=("arbitrary",),
                                             vmem_limit_bytes=VMEM_LIMIT),
        name="in_proj",
    )(x2, g_mix, w_in, qg, kg, cos, sin, pool_w, pool_scale)


def _dilated_attn_kernel(q_ref, k_ref, kh_ref, v_ref, vh_ref, o_ref, st_ref, *, tl):
    n_sub = tl // ATT_BLOCK
    qi = lax.broadcasted_iota(jnp.int32, (ATT_BLOCK, 2 * ATT_BLOCK), 0)
    kj = lax.broadcasted_iota(jnp.int32, (ATT_BLOCK, 2 * ATT_BLOCK), 1)
    delta = qi + ATT_BLOCK - kj
    in_band = jnp.logical_and(delta >= 0, delta <= ATT_BLOCK)
    bias = jnp.where(in_band, 0.0, NEG_INF).astype(F32)
    n_missing = jnp.where(pl.program_id(2) == 0, ATT_BLOCK, 0)
    bias_first = jnp.where(kj < n_missing, NEG_INF, bias)
    lane = lax.broadcasted_iota(jnp.int32, (ATT_BLOCK, LANES), 1)

    def block(row0, k_blk, v_blk, b):
        rows = pl.ds(row0, ATT_BLOCK)
        st = jnp.zeros((ATT_BLOCK, LANES), F32)
        for h in range(ATT_HEADS):
            sl = slice(h * HEAD_DIM, (h + 1) * HEAD_DIM)
            s = lax.dot_general(q_ref[0, rows, sl], k_blk[:, sl], (((1,), (1,)), ((), ())),
                                preferred_element_type=F32) + b
            m = jnp.max(s, axis=-1, keepdims=True)
            p = jnp.exp(s - m)
            l = jnp.sum(p, axis=-1, keepdims=True)
            acc = jnp.dot(p.astype(BF16), v_blk[:, sl], preferred_element_type=F32)
            o_ref[0, rows, sl] = acc.astype(BF16)
            st = jnp.where(lane == h, m, st)
            st = jnp.where(lane == ATT_HEADS + h, l, st)
        st_ref[0, rows, :] = st

    block(0, jnp.concatenate([kh_ref[0], k_ref[0, 0:ATT_BLOCK, :]], axis=0),
          jnp.concatenate([vh_ref[0], v_ref[0, 0:ATT_BLOCK, :]], axis=0), bias_first)

    def body(i, carry):
        prev0 = pl.multiple_of((i - 1) * ATT_BLOCK, ATT_BLOCK)
        rows2 = pl.ds(prev0, 2 * ATT_BLOCK)
        block(pl.multiple_of(i * ATT_BLOCK, ATT_BLOCK), k_ref[0, rows2, :], v_ref[0, rows2, :], bias)
        return carry

    lax.fori_loop(1, n_sub, body, 0)


def _dilated_attn(q, k, v, batch, seq, dilation):
    n_tok = q.shape[0]
    sub_len = seq // dilation
    tl = min(sub_len, 1024)
    n_sub = tl // ATT_BLOCK
    view = lambda a, w: a.reshape(batch, sub_len, dilation * w)
    cur = lambda w: pl.BlockSpec((1, tl, w), lambda b, r, i: (b, i, r))
    halo = pl.BlockSpec((1, ATT_BLOCK, KV_W), lambda b, r, i: (b, jnp.maximum(i * n_sub - 1, 0), r))
    acc, st = pl.pallas_call(
        functools.partial(_dilated_attn_kernel, tl=tl),
        grid=(batch, dilation, sub_len // tl),
        in_specs=[cur(KV_W), cur(KV_W), halo, cur(KV_W), halo],
        out_specs=[cur(KV_W), cur(LANES)],
        out_shape=[jax.ShapeDtypeStruct((batch, sub_len, dilation * KV_W), BF16),
                   jax.ShapeDtypeStruct((batch, sub_len, dilation * LANES), F32)],
        compiler_params=pltpu.CompilerParams(
            dimension_semantics=("parallel", "parallel", "parallel"), vmem_limit_bytes=VMEM_LIMIT),
        name=f"dilated_attn_d{dilation}",
    )(view(q, KV_W), view(k, KV_W), view(k, KV_W), view(v, KV_W), view(v, KV_W))
    return acc.reshape(n_tok, KV_W), st.reshape(n_tok, LANES)


def _mem_kv_kernel(mem_ref, g_ref, w_ref, kg_ref, k_ref, v_ref):
    mn = _rms(mem_ref[0], g_ref[...]).astype(BF16)
    kv = jnp.dot(mn, w_ref[...], preferred_element_type=F32)
    for h in range(X_HEADS):
        sl = slice(h * HEAD_DIM, (h + 1) * HEAD_DIM)
        k_ref[0, :, sl] = _rms(kv[:, sl], kg_ref[...]).astype(BF16)
    v_ref[0] = kv[:, X_W:].astype(BF16)


def _mem_kv(mem, g_mem, w_ckv, ckg):
    batch, n_mem, _ = mem.shape
    out = jax.ShapeDtypeStruct((batch, n_mem, X_W), BF16)
    blk = pl.BlockSpec((1, n_mem, X_W), lambda b: (b, 0, 0))
    return pl.pallas_call(
        _mem_kv_kernel,
        grid=(batch,),
        in_specs=[pl.BlockSpec((1, n_mem, D_MODEL), lambda b: (b, 0, 0)), _resident((1, D_MODEL)),
                  _resident((D_MODEL, 2 * X_W)), _resident((1, HEAD_DIM))],
        out_specs=[blk, blk],
        out_shape=[out, out],
        compiler_params=pltpu.CompilerParams(dimension_semantics=("parallel",),
                                             vmem_limit_bytes=VMEM_LIMIT),
        name="mem_kv",
    )(mem, g_mem, w_ckv, ckg)


def _mix_cross_kernel(x_ref, a1_ref, a2_ref, a3_ref, s1_ref, s2_ref, s3_ref, y_ref, wo_ref,
                      gc_ref, wq_ref, qg_ref, km_ref, vm_ref, wc_ref, h_ref):
    acc_refs = (a1_ref, a2_ref, a3_ref)
    stats = [r[...] for r in (s1_ref, s2_ref, s3_ref)]
    m_all = jnp.maximum(jnp.maximum(stats[0], stats[1]), stats[2])
    wts = [jnp.exp(s - m_all) for s in stats]
    mix = jnp.dot(y_ref[...], wo_ref[KV_W:, :], preferred_element_type=F32)
    for h in range(ATT_HEADS):
        sl = slice(h * HEAD_DIM, (h + 1) * HEAD_DIM)
        num = jnp.zeros((x_ref.shape[0], HEAD_DIM), F32)
        den = jnp.zeros((x_ref.shape[0], 1), F32)
        for g in range(len(DILATED_PAIRS)):
            w = wts[g][:, h:h + 1]
            num = num + w * acc_refs[g][:, sl].astype(F32)
            den = den + w * stats[g][:, ATT_HEADS + h:ATT_HEADS + h + 1]
        attn = (num / den).astype(BF16)
        mix = mix + jnp.dot(attn, wo_ref[sl, :], preferred_element_type=F32)
    h1 = x_ref[...] + mix

    hn = _rms(h1, gc_ref[...]).astype(BF16)
    qc = jnp.dot(hn, wq_ref[...], preferred_element_type=F32)
    out = h1
    for h in range(X_HEADS):
        sl = slice(h * HEAD_DIM, (h + 1) * HEAD_DIM)
        qh = _rms(qc[:, sl], qg_ref[...]).astype(BF16)
        s = lax.dot_general(qh, km_ref[0, :, sl], (((1,), (1,)), ((), ())),
                            preferred_element_type=F32)
        p = jnp.exp(s - jnp.max(s, axis=-1, keepdims=True))
        l = jnp.sum(p, axis=-1, keepdims=True)
        o = jnp.dot(p.astype(BF16), vm_ref[0, :, sl], preferred_element_type=F32) / l
        out = out + jnp.dot(o.astype(BF16), wc_ref[sl, :], preferred_element_type=F32)
    h_ref[...] = out


def _mix_cross(x2, accs, stats, y, w_out, g_cross, w_cq, cqg, k_mem, v_mem, w_co, seq):
    n_tok = x2.shape[0]
    tm = 512
    tiles_per_seq = seq // tm
    n_mem = k_mem.shape[1]
    tok = lambda w: pl.BlockSpec((tm, w), lambda i: (i, 0))
    mem_blk = pl.BlockSpec((1, n_mem, X_W), lambda i: (i // tiles_per_seq, 0, 0))
    return pl.pallas_call(
        _mix_cross_kernel,
        grid=(n_tok // tm,),
        in_specs=[tok(D_MODEL), tok(KV_W), tok(KV_W), tok(KV_W), tok(LANES), tok(LANES), tok(LANES),
                  tok(POOL_W), _resident((KV_W + POOL_W, D_MODEL)), _resident((1, D_MODEL)),
                  _resident((D_MODEL, X_W)), _resident((1, HEAD_DIM)), mem_blk, mem_blk,
                  _resident((X_W, D_MODEL))],
        out_specs=tok(D_MODEL),
        out_shape=jax.ShapeDtypeStruct((n_tok, D_MODEL), F32),
        compiler_params=pltpu.CompilerParams(dimension_semantics=("parallel",),
                                             vmem_limit_bytes=VMEM_LIMIT),
        name="mix_cross",
    )(x2, *accs, *stats, y, w_out, g_cross, w_cq, cqg, k_mem, v_mem, w_co)


def _ffn_kernel(h_ref, g_ref, wgu_ref, wd_ref, o_ref):
    h = h_ref[...]
    hn = _rms(h, g_ref[...]).astype(BF16)
    gate = jnp.dot(hn, wgu_ref[:, :D_FF], preferred_element_type=F32)
    up = jnp.dot(hn, wgu_ref[:, D_FF:], preferred_element_type=F32)
    act = (gate * jax.nn.sigmoid(gate) * up).astype(BF16)
    o_ref[...] = h + jnp.dot(act, wd_ref[...], preferred_element_type=F32)


def _ffn(h2, g_ffn, w_gate_up, w_down):
    n_tok = h2.shape[0]
    tm = 256
    tok = pl.BlockSpec((tm, D_MODEL), lambda i: (i, 0))
    return pl.pallas_call(
        _ffn_kernel,
        grid=(n_tok // tm,),
        in_specs=[tok, _resident((1, D_MODEL)), _resident((D_MODEL, 2 * D_FF)),
                  _resident((D_FF, D_MODEL))],
        out_specs=tok,
        out_shape=jax.ShapeDtypeStruct((n_tok, D_MODEL), F32),
        compiler_params=pltpu.CompilerParams(dimension_semantics=("parallel",),
                                             vmem_limit_bytes=VMEM_LIMIT),
        name="ffn",
    )(h2, g_ffn, w_gate_up, w_down)


def kernel(x, mem, positions, mix_norm_g, w_in, q_norm_g, k_norm_g, pool_w, pool_scale, w_out,
           cross_norm_g, mem_norm_g, w_cq, w_ckv, cq_norm_g, ck_norm_g, w_co,
           ffn_norm_g, w_gate_up, w_down):
    batch, seq, _ = x.shape
    depth = w_in.shape[0]
    scale = HEAD_DIM ** -0.5
    cos, sin = _rope_tables(positions)
    row = lambda a: a.reshape(1, -1)
    h = x.reshape(batch * seq, D_MODEL)
    for layer in range(depth):
        q1, q2, q3, k, v, y = _in_proj(
            h, row(mix_norm_g[layer]), w_in[layer].astype(BF16), row(q_norm_g[layer] * scale),
            row(k_norm_g[layer]), cos, sin, pool_w[layer].astype(BF16), row(pool_scale[layer]), seq)
        accs, stats = [], []
        for qg, (_, dilation) in zip((q1, q2, q3), DILATED_PAIRS):
            a, s = _dilated_attn(qg, k, v, batch, seq, dilation)
            accs.append(a)
            stats.append(s)
        k_mem, v_mem = _mem_kv(mem, row(mem_norm_g[layer]), w_ckv[layer].astype(BF16),
                               row(ck_norm_g[layer]))
        h = _mix_cross(h, accs, stats, y, w_out[layer].astype(BF16), row(cross_norm_g[layer]),
                       w_cq[layer].astype(BF16), row(cq_norm_g[layer] * scale), k_mem, v_mem,
                       w_co[layer].astype(BF16), seq)
        h = _ffn(h, row(ffn_norm_g[layer]), w_gate_up[layer].astype(BF16),
                 w_down[layer].astype(BF16))
    return h.reshape(batch, seq, D_MODEL)
```

```python
import functools

import jax
import jax.numpy as jnp
import numpy as np
from jax import lax
from jax.experimental import pallas as pl
from jax.experimental.pallas import tpu as pltpu

D_MODEL = 1024
HEAD_DIM = 128
ATT_HEADS = 4
DILATED_PAIRS = ((128, 1), (512, 4), (2048, 16))
DILATIONS = tuple(d for _, d in DILATED_PAIRS)
KV_W = ATT_HEADS * HEAD_DIM
POOL_WINDOWS = (2, 4, 8, 16)
POOL_W = len(POOL_WINDOWS) * HEAD_DIM
ROT_DIM = HEAD_DIM // 4
ROT_HALF = ROT_DIM // 2
ROPE_THETA = 500000.0
X_HEADS = 4
X_W = X_HEADS * HEAD_DIM
D_FF = 2816
EPS = 1e-6
NEG_INF = -1e30
ATT_BLOCK = 128
LANES = 128
SUBLANES = 8
ROT_PAIR_LANE = LANES // 2
POOL_HALO = 32

F32 = jnp.float32
BF16 = jnp.bfloat16

VMEM_LIMIT = 56 * 1024 * 1024


def _rms(x, g):
    ms = jnp.mean(x * x, axis=-1, keepdims=True)
    return x * lax.rsqrt(ms + EPS) * g


def _resident(shape):
    nd = len(shape)
    return pl.BlockSpec(shape, lambda *_: (0,) * nd, pipeline_mode=pl.Buffered(1))


def _rotary_lane_order():
    rest = np.arange(ROT_DIM, HEAD_DIM)
    n_mid = ROT_PAIR_LANE - ROT_HALF
    return np.concatenate([np.arange(ROT_HALF), rest[:n_mid],
                           np.arange(ROT_HALF, ROT_DIM), rest[n_mid:]])


def _rope_table_kernel(pos_ref, invf_ref, cos_ref, sin_ref):
    ang = pos_ref[...] * invf_ref[...]
    cos_ref[...] = jnp.cos(ang)
    sin_ref[...] = jnp.sin(ang)


def _rope_tables(positions):
    n_tok = positions.size
    per_row = LANES // ROT_HALF
    rows = n_tok // per_row
    pos = jnp.repeat(positions.reshape(-1).astype(F32), ROT_HALF).reshape(rows, LANES)
    inv_freq = ROPE_THETA ** (-jnp.arange(0, ROT_DIM, 2, dtype=F32) / ROT_DIM)
    invf = jnp.tile(inv_freq, per_row).reshape(1, LANES)
    blk = 256
    cos, sin = pl.pallas_call(
        _rope_table_kernel,
        grid=(rows // blk,),
        in_specs=[pl.BlockSpec((blk, LANES), lambda i: (i, 0)),
                  pl.BlockSpec((1, LANES), lambda i: (0, 0))],
        out_specs=[pl.BlockSpec((blk, LANES), lambda i: (i, 0))] * 2,
        out_shape=[jax.ShapeDtypeStruct((rows, LANES), F32)] * 2,
        name="rope_tables",
    )(pos, invf)
    return cos.reshape(n_tok, ROT_HALF), sin.reshape(n_tok, ROT_HALF)


def _in_proj_kernel(x_ref, g_ref, w_ref, qg_ref, kg_ref, cos_ref, sin_ref, pw_ref, ps_ref,
                    q1_ref, k1_ref, v1_ref, q4_ref, k4_ref, v4_ref, q16_ref, k16_ref, v16_ref, y_ref,
                    hist_ref, lvl_ref, stage_ref, *, tm, tiles_per_seq):
    xn = _rms(x_ref[...], g_ref[...]).astype(BF16)

    def proj(c):
        return jnp.dot(xn, w_ref[:, c * KV_W:(c + 1) * KV_W], preferred_element_type=F32)

    u = proj(5)
    seq_tile = pl.program_id(0) % tiles_per_seq

    @pl.when(seq_tile == 0)
    def _():
        hist_ref[0:POOL_HALO, :] = jnp.zeros((POOL_HALO, POOL_W), F32)

    hist_ref[POOL_HALO:POOL_HALO + tm, :] = u
    end = POOL_HALO + tm
    row16 = lax.broadcasted_iota(jnp.int32, (2 * SUBLANES, HEAD_DIM), 0)
    for g, w in enumerate(POOL_WINDOWS):
        sl = slice(g * HEAD_DIM, (g + 1) * HEAD_DIM)
        src, start, shift = hist_ref, SUBLANES, 1
        while True:
            cols = sl if src is hist_ref else slice(None)
            tot = src[start:end, cols] + src[start - shift:end - shift, cols]
            shift *= 2
            if shift == w:
                break
            lvl_ref[g, start:end, :] = tot
            src, start = lvl_ref.at[g], start + SUBLANES
        tot = tot[POOL_HALO - start:, :]
        n_first = jnp.maximum(jnp.minimum(row16 + 1, w), jnp.where(seq_tile == 0, 0, w))
        inv_first = 1.0 / n_first.astype(F32)
        ug = u[:, sl]
        d = jnp.concatenate([tot[:2 * SUBLANES] * inv_first, tot[2 * SUBLANES:] * (1.0 / w)], axis=0) - ug
        yg = jnp.dot(d.astype(BF16), pw_ref[g], preferred_element_type=F32) * ps_ref[:, sl]
        y_ref[:, sl] = yg.astype(BF16)
    hist_ref[0:POOL_HALO, :] = hist_ref[tm:tm + POOL_HALO, :]

    lane = lax.broadcasted_iota(jnp.int32, (tm, LANES), 1)
    cs = jnp.tile(cos_ref[...], (1, LANES // ROT_HALF))
    sn = jnp.tile(sin_ref[...], (1, LANES // ROT_HALF))
    first = lane < ROT_HALF
    second = jnp.logical_and(lane >= ROT_PAIR_LANE, lane < ROT_PAIR_LANE + ROT_HALF)
    c_tab = jnp.where(jnp.logical_or(first, second), cs, 1.0)
    s_tab = jnp.where(first, -sn, jnp.where(second, sn, 0.0))

    def norm_rope(t, g):
        y = _rms(t, g)
        return y * c_tab + pltpu.roll(y, ROT_PAIR_LANE, 1) * s_tab

    def emit(a, slabs, tok_ref, views):
        for h, t in enumerate(slabs):
            if tok_ref is not None:
                tok_ref[:, h * HEAD_DIM:(h + 1) * HEAD_DIM] = t.astype(BF16)
            if views:
                stage_ref[a, h] = t
        for view_ref, d in views:
            n = tm // d
            for r in range(d):
                for h in range(ATT_HEADS):
                    c0 = r * KV_W + h * HEAD_DIM
                    view_ref[:, c0:c0 + HEAD_DIM] = stage_ref[a, h, pl.ds(r, n, stride=d), :].astype(BF16)

    def heads(p, gain_ref):
        return [norm_rope(p[:, h * HEAD_DIM:(h + 1) * HEAD_DIM], gain_ref[...])
                for h in range(ATT_HEADS)]

    emit(0, heads(proj(0), qg_ref), q1_ref, ())
    emit(0, heads(proj(1), qg_ref), None, ((q4_ref, 4),))
    emit(1, heads(proj(2), qg_ref), None, ((q16_ref, 16),))
    emit(2, heads(proj(3), kg_ref), k1_ref, ((k4_ref, 4), (k16_ref, 16)))
    pv = proj(4)
    emit(3, [pv[:, h * HEAD_DIM:(h + 1) * HEAD_DIM] for h in range(ATT_HEADS)], v1_ref,
         ((v4_ref, 4), (v16_ref, 16)))


def _in_proj(x2, g_mix, w_in, qg, kg, cos, sin, pool_w, pool_scale, seq):
    n_tok = x2.shape[0]
    tm = 512
    tiles_per_seq = seq // tm
    in_w = w_in.shape[1]
    tok = lambda w: pl.BlockSpec((tm, w), lambda i: (i, 0))
    view = lambda d: pl.BlockSpec((tm // d, d * KV_W), lambda i: (i, 0))
    view_sds = lambda d: jax.ShapeDtypeStruct((n_tok // d, d * KV_W), BF16)
    out_d = (1, 1, 1, 4, 4, 4, 16, 16, 16, 1)
    return pl.pallas_call(
        functools.partial(_in_proj_kernel, tm=tm, tiles_per_seq=tiles_per_seq),
        grid=(n_tok // tm,),
        in_specs=[tok(D_MODEL), _resident((1, D_MODEL)), _resident((D_MODEL, in_w)),
                  _resident((1, HEAD_DIM)), _resident((1, HEAD_DIM)),
                  tok(ROT_HALF), tok(ROT_HALF),
                  _resident((len(POOL_WINDOWS), HEAD_DIM, HEAD_DIM)), _resident((1, POOL_W))],
        out_specs=[view(d) for d in out_d],
        out_shape=[view_sds(d) for d in out_d],
        scratch_shapes=[pltpu.VMEM((POOL_HALO + tm, POOL_W), F32),
                        pltpu.VMEM((len(POOL_WINDOWS), POOL_HALO + tm, HEAD_DIM), F32),
                        pltpu.VMEM((4, ATT_HEADS, tm, HEAD_DIM), F32)],
        compiler_params=pltpu.CompilerParams(dimension_semantics=("arbitrary",),
                                             vmem_limit_bytes=VMEM_LIMIT),
        name="in_proj",
    )(x2, g_mix, w_in, qg, kg, cos, sin, pool_w, pool_scale)


def _dilated_attn_kernel(q_ref, k_ref, kh_ref, v_ref, vh_ref, o_ref, st_ref, *, tl, rb):
    n_sub = tl // ATT_BLOCK
    qi = lax.broadcasted_iota(jnp.int32, (ATT_BLOCK, 2 * ATT_BLOCK), 0)
    kj = lax.broadcasted_iota(jnp.int32, (ATT_BLOCK, 2 * ATT_BLOCK), 1)
    delta = qi + ATT_BLOCK - kj
    in_band = jnp.logical_and(delta >= 0, delta <= ATT_BLOCK)
    bias = jnp.where(in_band, 0.0, NEG_INF).astype(F32)
    n_missing = jnp.where(pl.program_id(2) == 0, ATT_BLOCK, 0)
    bias_first = jnp.where(kj < n_missing, NEG_INF, bias)
    lane = lax.broadcasted_iota(jnp.int32, (ATT_BLOCK, LANES), 1)

    for r in range(rb):
        for i in range(n_sub):
            rows = slice(i * ATT_BLOCK, (i + 1) * ATT_BLOCK)
            rows2 = slice((i - 1) * ATT_BLOCK, (i + 1) * ATT_BLOCK)
            st = jnp.zeros((ATT_BLOCK, LANES), F32)
            for h in range(ATT_HEADS):
                c0 = r * KV_W + h * HEAD_DIM
                sl = slice(c0, c0 + HEAD_DIM)
                if i == 0:
                    k_blk = jnp.concatenate([kh_ref[0, :, sl], k_ref[0, rows, sl]], axis=0)
                    v_blk = jnp.concatenate([vh_ref[0, :, sl], v_ref[0, rows, sl]], axis=0)
                    b = bias_first
                else:
                    k_blk, v_blk, b = k_ref[0, rows2, sl], v_ref[0, rows2, sl], bias
                s = lax.dot_general(q_ref[0, rows, sl], k_blk, (((1,), (1,)), ((), ())),
                                    preferred_element_type=F32) + b
                m = jnp.max(s, axis=-1, keepdims=True)
                p = jnp.exp(s - m)
                l = jnp.sum(p, axis=-1, keepdims=True)
                acc = jnp.dot(p.astype(BF16), v_blk, preferred_element_type=F32)
                o_ref[0, rows, sl] = acc.astype(BF16)
                st = jnp.where(lane == h, m, st)
                st = jnp.where(lane == ATT_HEADS + h, l, st)
            st_ref[0, rows, r * LANES:(r + 1) * LANES] = st


def _dilated_attn(q, k, v, batch, seq, dilation):
    sub_len = seq // dilation
    tl = min(sub_len, 512)
    rb = min(dilation, 1024 // tl)
    n_sub = tl // ATT_BLOCK
    view = lambda a: a.reshape(batch, sub_len, a.shape[1])
    cur = lambda w: pl.BlockSpec((1, tl, rb * w), lambda b, r, i: (b, i, r))
    halo = pl.BlockSpec((1, ATT_BLOCK, rb * KV_W),
                        lambda b, r, i: (b, jnp.maximum(i * n_sub - 1, 0), r))
    acc, st = pl.pallas_call(
        functools.partial(_dilated_attn_kernel, tl=tl, rb=rb),
        grid=(batch, dilation // rb, sub_len // tl),
        in_specs=[cur(KV_W), cur(KV_W), halo, cur(KV_W), halo],
        out_specs=[cur(KV_W), cur(LANES)],
        out_shape=[jax.ShapeDtypeStruct((batch, sub_len, dilation * KV_W), BF16),
                   jax.ShapeDtypeStruct((batch, sub_len, dilation * LANES), F32)],
        compiler_params=pltpu.CompilerParams(
            dimension_semantics=("parallel", "parallel", "parallel"), vmem_limit_bytes=VMEM_LIMIT),
        name=f"dilated_attn_d{dilation}",
    )(view(q), view(k), view(k), view(v), view(v))
    return (acc.reshape(batch * sub_len, dilation * KV_W), st.reshape(batch * sub_len, dilation * LANES))


def _mem_kv_kernel(mem_ref, g_ref, w_ref, kg_ref, k_ref, v_ref):
    mn = _rms(mem_ref[0], g_ref[...]).astype(BF16)
    kv = jnp.dot(mn, w_ref[...], preferred_element_type=F32)
    for h in range(X_HEADS):
        sl = slice(h * HEAD_DIM, (h + 1) * HEAD_DIM)
        k_ref[0, :, sl] = _rms(kv[:, sl], kg_ref[...]).astype(BF16)
    v_ref[0] = kv[:, X_W:].astype(BF16)


def _mem_kv(mem, g_mem, w_ckv, ckg):
    batch, n_mem, _ = mem.shape
    out = jax.ShapeDtypeStruct((batch, n_mem, X_W), BF16)
    blk = pl.BlockSpec((1, n_mem, X_W), lambda b: (b, 0, 0))
    return pl.pallas_call(
        _mem_kv_kernel,
        grid=(batch,),
        in_specs=[pl.BlockSpec((1, n_mem, D_MODEL), lambda b: (b, 0, 0)), _resident((1, D_MODEL)),
                  _resident((D_MODEL, 2 * X_W)), _resident((1, HEAD_DIM))],
        out_specs=[blk, blk],
        out_shape=[out, out],
        compiler_params=pltpu.CompilerParams(dimension_semantics=("parallel",),
                                             vmem_limit_bytes=VMEM_LIMIT),
        name="mem_kv",
    )(mem, g_mem, w_ckv, ckg)


def _mix_cross_kernel(x_ref, a1_ref, a4_ref, a16_ref, s1_ref, s4_ref, s16_ref, y_ref, wo_ref,
                      gc_ref, wq_ref, qg_ref, km_ref, vm_ref, wc_ref, h_ref, acc_tok, st_tok, *, tm):
    for gi, (a_ref, s_ref, d) in enumerate(((a4_ref, s4_ref, 4), (a16_ref, s16_ref, 16))):
        n = tm // d
        for r in range(d):
            dst = pl.ds(r, n, stride=d)
            st_tok[gi, dst, :] = s_ref[:, r * LANES:(r + 1) * LANES]
            for h in range(ATT_HEADS):
                c0 = r * KV_W + h * HEAD_DIM
                acc_tok[gi, h, dst, :] = a_ref[:, c0:c0 + HEAD_DIM].astype(F32)

    stats = [s1_ref[...], st_tok[0], st_tok[1]]
    m_all = jnp.maximum(jnp.maximum(stats[0], stats[1]), stats[2])
    wts = [jnp.exp(s - m_all) for s in stats]
    mix = jnp.dot(y_ref[...], wo_ref[KV_W:, :], preferred_element_type=F32)
    for h in range(ATT_HEADS):
        sl = slice(h * HEAD_DIM, (h + 1) * HEAD_DIM)
        accs = (a1_ref[:, sl].astype(F32), acc_tok[0, h], acc_tok[1, h])
        num = jnp.zeros((tm, HEAD_DIM), F32)
        den = jnp.zeros((tm, 1), F32)
        for g in range(len(DILATED_PAIRS)):
            w = wts[g][:, h:h + 1]
            num = num + w * accs[g]
            den = den + w * stats[g][:, ATT_HEADS + h:ATT_HEADS + h + 1]
        attn = (num / den).astype(BF16)
        mix = mix + jnp.dot(attn, wo_ref[sl, :], preferred_element_type=F32)
    h1 = x_ref[...] + mix

    hn = _rms(h1, gc_ref[...]).astype(BF16)
    qc = jnp.dot(hn, wq_ref[...], preferred_element_type=F32)
    out = h1
    for h in range(X_HEADS):
        sl = slice(h * HEAD_DIM, (h + 1) * HEAD_DIM)
        qh = _rms(qc[:, sl], qg_ref[...]).astype(BF16)
        s = lax.dot_general(qh, km_ref[0, :, sl], (((1,), (1,)), ((), ())),
                            preferred_element_type=F32)
        p = jnp.exp(s - jnp.max(s, axis=-1, keepdims=True))
        l = jnp.sum(p, axis=-1, keepdims=True)
        o = jnp.dot(p.astype(BF16), vm_ref[0, :, sl], preferred_element_type=F32) / l
        out = out + jnp.dot(o.astype(BF16), wc_ref[sl, :], preferred_element_type=F32)
    h_ref[...] = out


def _mix_cross(x2, accs, stats, y, w_out, g_cross, w_cq, cqg, k_mem, v_mem, w_co, seq):
    n_tok = x2.shape[0]
    tm = 512
    tiles_per_seq = seq // tm
    n_mem = k_mem.shape[1]
    tok = lambda w: pl.BlockSpec((tm, w), lambda i: (i, 0))
    view = lambda d, w: pl.BlockSpec((tm // d, d * w), lambda i: (i, 0))
    mem_blk = pl.BlockSpec((1, n_mem, X_W), lambda i: (i // tiles_per_seq, 0, 0))
    return pl.pallas_call(
        functools.partial(_mix_cross_kernel, tm=tm),
        grid=(n_tok // tm,),
        in_specs=[tok(D_MODEL)] + [view(d, KV_W) for d in DILATIONS]
                 + [view(d, LANES) for d in DILATIONS]
                 + [tok(POOL_W), _resident((KV_W + POOL_W, D_MODEL)), _resident((1, D_MODEL)),
                    _resident((D_MODEL, X_W)), _resident((1, HEAD_DIM)), mem_blk, mem_blk,
                    _resident((X_W, D_MODEL))],
        out_specs=tok(D_MODEL),
        out_shape=jax.ShapeDtypeStruct((n_tok, D_MODEL), F32),
        scratch_shapes=[pltpu.VMEM((2, ATT_HEADS, tm, HEAD_DIM), F32),
                        pltpu.VMEM((2, tm, LANES), F32)],
        compiler_params=pltpu.CompilerParams(dimension_semantics=("parallel",),
                                             vmem_limit_bytes=VMEM_LIMIT),
        name="mix_cross",
    )(x2, *accs, *stats, y, w_out, g_cross, w_cq, cqg, k_mem, v_mem, w_co)


def _ffn_kernel(h_ref, g_ref, wgu_ref, wd_ref, o_ref):
    h = h_ref[...]
    hn = _rms(h, g_ref[...]).astype(BF16)
    gate = jnp.dot(hn, wgu_ref[:, :D_FF], preferred_element_type=F32)
    up = jnp.dot(hn, wgu_ref[:, D_FF:], preferred_element_type=F32)
    act = (gate * jax.nn.sigmoid(gate) * up).astype(BF16)
    o_ref[...] = h + jnp.dot(act, wd_ref[...], preferred_element_type=F32)


def _ffn(h2, g_ffn, w_gate_up, w_down):
    n_tok = h2.shape[0]
    tm = 256
    tok = pl.BlockSpec((tm, D_MODEL), lambda i: (i, 0))
    return pl.pallas_call(
        _ffn_kernel,
        grid=(n_tok // tm,),
        in_specs=[tok, _resident((1, D_MODEL)), _resident((D_MODEL, 2 * D_FF)),
                  _resident((D_FF, D_MODEL))],
        out_specs=tok,
        out_shape=jax.ShapeDtypeStruct((n_tok, D_MODEL), F32),
        compiler_params=pltpu.CompilerParams(dimension_semantics=("parallel",),
                                             vmem_limit_bytes=VMEM_LIMIT),
        name="ffn",
    )(h2, g_ffn, w_gate_up, w_down)


def kernel(x, mem, positions, mix_norm_g, w_in, q_norm_g, k_norm_g, pool_w, pool_scale, w_out,
           cross_norm_g, mem_norm_g, w_cq, w_ckv, cq_norm_g, ck_norm_g, w_co,
           ffn_norm_g, w_gate_up, w_down):
    batch, seq, _ = x.shape
    depth = w_in.shape[0]
    scale = HEAD_DIM ** -0.5
    cos, sin = _rope_tables(positions)
    row = lambda a: a.reshape(1, -1)
    order = _rotary_lane_order()
    n_qk_heads = (len(DILATED_PAIRS) + 1) * ATT_HEADS
    h = x.reshape(batch * seq, D_MODEL)
    for layer in range(depth):
        w_qk = w_in[layer][:, :n_qk_heads * HEAD_DIM].reshape(D_MODEL, n_qk_heads, HEAD_DIM)[:, :, order]
        w_in_l = jnp.concatenate([w_qk.reshape(D_MODEL, -1), w_in[layer][:, n_qk_heads * HEAD_DIM:]],
                                 axis=1).astype(BF16)
        q1, k1, v1, q4, k4, v4, q16, k16, v16, y = _in_proj(
            h, row(mix_norm_g[layer]), w_in_l, row(q_norm_g[layer][order] * scale),
            row(k_norm_g[layer][order]), cos, sin, pool_w[layer].astype(BF16),
            row(pool_scale[layer]), seq)
        accs, stats = [], []
        for (qv, kv, vv), dilation in zip(((q1, k1, v1), (q4, k4, v4), (q16, k16, v16)), DILATIONS):
            a, s = _dilated_attn(qv, kv, vv, batch, seq, dilation)
            accs.append(a)
            stats.append(s)
        k_mem, v_mem = _mem_kv(mem, row(mem_norm_g[layer]), w_ckv[layer].astype(BF16),
                               row(ck_norm_g[layer]))
        h = _mix_cross(h, accs, stats, y, w_out[layer].astype(BF16), row(cross_norm_g[layer]),
                       w_cq[layer].astype(BF16), row(cq_norm_g[layer] * scale), k_mem, v_mem,
                       w_co[layer].astype(BF16), seq)
        h = _ffn(h, row(ffn_norm_g[layer]), w_gate_up[layer].astype(BF16),
                 w_down[layer].astype(BF16))
    return h.reshape(batch, seq, D_MODEL)
```

```python
import functools

import jax
import jax.numpy as jnp
import numpy as np
from jax import lax
from jax.experimental import pallas as pl
from jax.experimental.pallas import tpu as pltpu

D_MODEL = 1024
HEAD_DIM = 128
ATT_HEADS = 4
DILATED_PAIRS = ((128, 1), (512, 4), (2048, 16))
DILATIONS = tuple(d for _, d in DILATED_PAIRS)
KV_W = ATT_HEADS * HEAD_DIM
POOL_WINDOWS = (2, 4, 8, 16)
POOL_W = len(POOL_WINDOWS) * HEAD_DIM
IN_W = (len(DILATED_PAIRS) + 2) * KV_W + POOL_W
ROT_DIM = HEAD_DIM // 4
ROT_HALF = ROT_DIM // 2
ROPE_THETA = 500000.0
X_HEADS = 4
X_W = X_HEADS * HEAD_DIM
D_FF = 2816
EPS = 1e-6
NEG_INF = -1e30
ATT_BLOCK = 128
LANES = 128
SUBLANES = 8
ROT_PAIR_LANE = LANES // 2
ROW_CHUNK = 128
POOL_HALO = 32

F32 = jnp.float32
BF16 = jnp.bfloat16

VMEM_LIMIT = 56 * 1024 * 1024


def _rms(x, g):
    ms = jnp.mean(x * x, axis=-1, keepdims=True)
    return x * lax.rsqrt(ms + EPS) * g


def _resident(shape):
    nd = len(shape)
    return pl.BlockSpec(shape, lambda *_: (0,) * nd, pipeline_mode=pl.Buffered(1))


def _rotary_lane_order():
    rest = np.arange(ROT_DIM, HEAD_DIM)
    n_mid = ROT_PAIR_LANE - ROT_HALF
    return np.concatenate([np.arange(ROT_HALF), rest[:n_mid],
                           np.arange(ROT_HALF, ROT_DIM), rest[n_mid:]])


def _rope_table_kernel(pos_ref, invf_ref, cos_ref, sin_ref):
    ang = pos_ref[...] * invf_ref[...]
    cos_ref[...] = jnp.cos(ang)
    sin_ref[...] = jnp.sin(ang)


def _rope_tables(positions):
    n_tok = positions.size
    per_row = LANES // ROT_HALF
    rows = n_tok // per_row
    pos = jnp.repeat(positions.reshape(-1).astype(F32), ROT_HALF).reshape(rows, LANES)
    inv_freq = ROPE_THETA ** (-jnp.arange(0, ROT_DIM, 2, dtype=F32) / ROT_DIM)
    invf = jnp.tile(inv_freq, per_row).reshape(1, LANES)
    blk = 256
    cos, sin = pl.pallas_call(
        _rope_table_kernel,
        grid=(rows // blk,),
        in_specs=[pl.BlockSpec((blk, LANES), lambda i: (i, 0)),
                  pl.BlockSpec((1, LANES), lambda i: (0, 0))],
        out_specs=[pl.BlockSpec((blk, LANES), lambda i: (i, 0))] * 2,
        out_shape=[jax.ShapeDtypeStruct((rows, LANES), F32)] * 2,
        name="rope_tables",
    )(pos, invf)
    return cos.reshape(n_tok, ROT_HALF), sin.reshape(n_tok, ROT_HALF)


def _in_proj_kernel(x_ref, g_ref, w_ref, qg_ref, kg_ref, cos_ref, sin_ref, pw_ref, ps_ref,
                    q1_ref, k1_ref, v1_ref, q4_ref, k4_ref, v4_ref, q16_ref, k16_ref, v16_ref, y_ref,
                    xn_ref, tab_ref, pa_ref, pb_ref, hist_ref, lvl_ref, stage_ref, stage4_ref,
                    *, tm, tiles_per_seq):
    step = pl.program_id(0)
    rc = ROW_CHUNK
    n_rc = tm // rc

    def project(p_ref):
        for i in range(n_rc):
            rows = slice(i * rc, (i + 1) * rc)
            xn_ref[rows, :] = _rms(x_ref[rows, :], g_ref[...]).astype(BF16)
        for c in range(IN_W // KV_W):
            cols = slice(c * KV_W, (c + 1) * KV_W)
            p_ref[:, cols] = jnp.dot(xn_ref[...], w_ref[:, cols], preferred_element_type=F32)

    def finish(p_ref):
        lane = lax.broadcasted_iota(jnp.int32, (tm, LANES), 1)
        fill = jnp.zeros((tm, LANES - ROT_HALF), F32)
        cs = jnp.concatenate([cos_ref[...], fill], axis=1)
        sn = jnp.concatenate([sin_ref[...], fill], axis=1)
        first = lane < ROT_HALF
        second = jnp.logical_and(lane >= ROT_PAIR_LANE, lane < ROT_PAIR_LANE + ROT_HALF)
        c_tab = jnp.where(first, cs, jnp.where(second, pltpu.roll(cs, ROT_PAIR_LANE, 1), 1.0))
        s_tab = jnp.where(first, -sn, jnp.where(second, pltpu.roll(sn, ROT_PAIR_LANE, 1), 0.0))
        for gi, gain_ref in enumerate((qg_ref, kg_ref)):
            gain = gain_ref[...]
            tab_ref[2 * gi] = c_tab * gain
            tab_ref[2 * gi + 1] = s_tab * pltpu.roll(gain, ROT_PAIR_LANE, 1)

        def emit(c, gi, tok_ref, v4_ref, v16_ref, a):
            for h in range(ATT_HEADS):
                sl = slice(h * HEAD_DIM, (h + 1) * HEAD_DIM)
                src = slice(c * KV_W + h * HEAD_DIM, c * KV_W + (h + 1) * HEAD_DIM)
                for i in range(n_rc):
                    rows = slice(i * rc, (i + 1) * rc)
                    t = p_ref[rows, src]
                    if gi >= 0:
                        rs = lax.rsqrt(jnp.mean(t * t, axis=-1, keepdims=True) + EPS)
                        t = (t * tab_ref[2 * gi, rows, :]
                             + pltpu.roll(t, ROT_PAIR_LANE, 1) * tab_ref[2 * gi + 1, rows, :]) * rs
                    if tok_ref is not None:
                        tok_ref[rows, sl] = t.astype(BF16)
                    if v4_ref is not None or v16_ref is not None:
                        stage_ref[a, h, rows, :] = t
                if v4_ref is None and v16_ref is None:
                    continue
                n4 = tm // 4
                for b in range(4):
                    t4 = stage_ref[a, h, pl.ds(b, n4, stride=4), :]
                    if v4_ref is not None:
                        c0 = b * KV_W + h * HEAD_DIM
                        v4_ref[:, c0:c0 + HEAD_DIM] = t4.astype(BF16)
                    if v16_ref is not None:
                        stage4_ref[a, h, b] = t4
                if v16_ref is not None:
                    for b in range(4):
                        for a4 in range(4):
                            c0 = (4 * a4 + b) * KV_W + h * HEAD_DIM
                            v16_ref[:, c0:c0 + HEAD_DIM] = (
                                stage4_ref[a, h, b, pl.ds(a4, n4 // 4, stride=4), :].astype(BF16))

        emit(3, 1, k1_ref, k4_ref, k16_ref, 0)
        emit(4, -1, v1_ref, v4_ref, v16_ref, 1)
        emit(2, 0, None, None, q16_ref, 2)
        emit(1, 0, None, q4_ref, None, 3)
        emit(0, 0, q1_ref, None, None, 0)

        seq_tile = jnp.maximum(step - 1, 0) % tiles_per_seq
        end = POOL_HALO + tm
        hist_ref[0:POOL_HALO, :] = jnp.where(seq_tile == 0, 0.0, 1.0) * hist_ref[tm:end, :]
        u_cols = slice(IN_W - POOL_W, IN_W)
        hist_ref[POOL_HALO:end, :] = p_ref[:, u_cols]
        row16 = lax.broadcasted_iota(jnp.int32, (2 * SUBLANES, HEAD_DIM), 0)
        for g, w in enumerate(POOL_WINDOWS):
            sl = slice(g * HEAD_DIM, (g + 1) * HEAD_DIM)
            src, start, shift = hist_ref, SUBLANES, 1
            while True:
                cols = sl if src is hist_ref else slice(None)
                tot = src[start:end, cols] + src[start - shift:end - shift, cols]
                shift *= 2
                if shift == w:
                    break
                lvl_ref[g, start:end, :] = tot
                src, start = lvl_ref.at[g], start + SUBLANES
            tot = tot[POOL_HALO - start:, :]
            n_first = jnp.maximum(jnp.minimum(row16 + 1, w), jnp.where(seq_tile == 0, 0, w))
            inv_first = 1.0 / n_first.astype(F32)
            ug = hist_ref[POOL_HALO:end, sl]
            d = jnp.concatenate([tot[:2 * SUBLANES] * inv_first, tot[2 * SUBLANES:] * (1.0 / w)],
                                axis=0) - ug
            yg = jnp.dot(d.astype(BF16), pw_ref[g], preferred_element_type=F32) * ps_ref[:, sl]
            y_ref[:, sl] = yg.astype(BF16)

    @pl.when(step == 0)
    def _():
        pb_ref[...] = jnp.zeros(pb_ref.shape, F32)
        hist_ref[...] = jnp.zeros(hist_ref.shape, F32)

    @pl.when(step % 2 == 0)
    def _():
        project(pa_ref)
        finish(pb_ref)

    @pl.when(step % 2 == 1)
    def _():
        project(pb_ref)
        finish(pa_ref)


def _in_proj(x2, g_mix, w_in, qg, kg, cos, sin, pool_w, pool_scale, seq):
    n_tok = x2.shape[0]
    tm = 512
    n_tiles = n_tok // tm
    tiles_per_seq = seq // tm
    ahead = lambda i: (jnp.minimum(i, n_tiles - 1), 0)
    behind = lambda i: (jnp.maximum(i - 1, 0), 0)
    view = lambda d: pl.BlockSpec((tm // d, d * KV_W), behind)
    view_sds = lambda d: jax.ShapeDtypeStruct((n_tok // d, d * KV_W), BF16)
    out_d = (1, 1, 1, 4, 4, 4, 16, 16, 16, 1)
    return pl.pallas_call(
        functools.partial(_in_proj_kernel, tm=tm, tiles_per_seq=tiles_per_seq),
        grid=(n_tiles + 1,),
        in_specs=[pl.BlockSpec((tm, D_MODEL), ahead), _resident((1, D_MODEL)),
                  _resident((D_MODEL, IN_W)), _resident((1, HEAD_DIM)), _resident((1, HEAD_DIM)),
                  pl.BlockSpec((tm, ROT_HALF), behind), pl.BlockSpec((tm, ROT_HALF), behind),
                  _resident((len(POOL_WINDOWS), HEAD_DIM, HEAD_DIM)), _resident((1, POOL_W))],
        out_specs=[view(d) for d in out_d],
        out_shape=[view_sds(d) for d in out_d],
        scratch_shapes=[pltpu.VMEM((tm, D_MODEL), BF16),
                        pltpu.VMEM((4, tm, LANES), F32),
                        pltpu.VMEM((tm, IN_W), F32),
                        pltpu.VMEM((tm, IN_W), F32),
                        pltpu.VMEM((POOL_HALO + tm, POOL_W), F32),
                        pltpu.VMEM((len(POOL_WINDOWS), POOL_HALO + tm, HEAD_DIM), F32),
                        pltpu.VMEM((4, ATT_HEADS, tm, HEAD_DIM), F32),
                        pltpu.VMEM((3, ATT_HEADS, 4, tm // 4, HEAD_DIM), F32)],
        compiler_params=pltpu.CompilerParams(dimension_semantics=("arbitrary",),
                                             vmem_limit_bytes=VMEM_LIMIT),
        name="in_proj",
    )(x2, g_mix, w_in, qg, kg, cos, sin, pool_w, pool_scale)


def _dilated_attn_kernel(q_ref, k_ref, kh_ref, v_ref, vh_ref, o_ref, st_ref, *, tl, rb):
    n_sub = tl // ATT_BLOCK
    qi = lax.broadcasted_iota(jnp.int32, (ATT_BLOCK, 2 * ATT_BLOCK), 0)
    kj = lax.broadcasted_iota(jnp.int32, (ATT_BLOCK, 2 * ATT_BLOCK), 1)
    delta = qi + ATT_BLOCK - kj
    in_band = jnp.logical_and(delta >= 0, delta <= ATT_BLOCK)
    bias = jnp.where(in_band, 0.0, NEG_INF).astype(F32)
    n_missing = jnp.where(pl.program_id(2) == 0, ATT_BLOCK, 0)
    bias_first = jnp.where(kj < n_missing, NEG_INF, bias)
    lane = lax.broadcasted_iota(jnp.int32, (ATT_BLOCK, LANES), 1)

    for r in range(rb):
        for i in range(n_sub):
            rows = slice(i * ATT_BLOCK, (i + 1) * ATT_BLOCK)
            rows2 = slice((i - 1) * ATT_BLOCK, (i + 1) * ATT_BLOCK)
            st = jnp.zeros((ATT_BLOCK, LANES), F32)
            for h in range(ATT_HEADS):
                c0 = r * KV_W + h * HEAD_DIM
                sl = slice(c0, c0 + HEAD_DIM)
                if i == 0:
                    k_blk = jnp.concatenate([kh_ref[0, :, sl], k_ref[0, rows, sl]], axis=0)
                    v_blk = jnp.concatenate([vh_ref[0, :, sl], v_ref[0, rows, sl]], axis=0)
                    b = bias_first
                else:
                    k_blk, v_blk, b = k_ref[0, rows2, sl], v_ref[0, rows2, sl], bias
                s = lax.dot_general(q_ref[0, rows, sl], k_blk, (((1,), (1,)), ((), ())),
                                    preferred_element_type=F32) + b
                m = jnp.max(s, axis=-1, keepdims=True)
                p = jnp.exp(s - m)
                l = jnp.sum(p, axis=-1, keepdims=True)
                acc = jnp.dot(p.astype(BF16), v_blk, preferred_element_type=F32)
                o_ref[0, rows, sl] = acc.astype(BF16)
                st = jnp.where(lane == h, m, st)
                st = jnp.where(lane == ATT_HEADS + h, l, st)
            st_ref[0, rows, r * LANES:(r + 1) * LANES] = st


def _dilated_attn(q, k, v, batch, seq, dilation):
    sub_len = seq // dilation
    tl = min(sub_len, 512)
    rb = min(dilation, 1024 // tl)
    n_sub = tl // ATT_BLOCK
    view = lambda a: a.reshape(batch, sub_len, a.shape[1])
    cur = lambda w: pl.BlockSpec((1, tl, rb * w), lambda b, r, i: (b, i, r))
    halo = pl.BlockSpec((1, ATT_BLOCK, rb * KV_W),
                        lambda b, r, i: (b, jnp.maximum(i * n_sub - 1, 0), r))
    acc, st = pl.pallas_call(
        functools.partial(_dilated_attn_kernel, tl=tl, rb=rb),
        grid=(batch, dilation // rb, sub_len // tl),
        in_specs=[cur(KV_W), cur(KV_W), halo, cur(KV_W), halo],
        out_specs=[cur(KV_W), cur(LANES)],
        out_shape=[jax.ShapeDtypeStruct((batch, sub_len, dilation * KV_W), BF16),
                   jax.ShapeDtypeStruct((batch, sub_len, dilation * LANES), F32)],
        compiler_params=pltpu.CompilerParams(
            dimension_semantics=("parallel", "parallel", "parallel"), vmem_limit_bytes=VMEM_LIMIT),
        name=f"dilated_attn_d{dilation}",
    )(view(q), view(k), view(k), view(v), view(v))
    return (acc.reshape(batch * sub_len, dilation * KV_W), st.reshape(batch * sub_len, dilation * LANES))


def _mem_kv_kernel(mem_ref, g_ref, w_ref, kg_ref, k_ref, v_ref):
    mn = _rms(mem_ref[0], g_ref[...]).astype(BF16)
    kv = jnp.dot(mn, w_ref[...], preferred_element_type=F32)
    for h in range(X_HEADS):
        sl = slice(h * HEAD_DIM, (h + 1) * HEAD_DIM)
        k_ref[0, :, sl] = _rms(kv[:, sl], kg_ref[...]).astype(BF16)
    v_ref[0] = kv[:, X_W:].astype(BF16)


def _mem_kv(mem, g_mem, w_ckv, ckg):
    batch, n_mem, _ = mem.shape
    out = jax.ShapeDtypeStruct((batch, n_mem, X_W), BF16)
    blk = pl.BlockSpec((1, n_mem, X_W), lambda b: (b, 0, 0))
    return pl.pallas_call(
        _mem_kv_kernel,
        grid=(batch,),
        in_specs=[pl.BlockSpec((1, n_mem, D_MODEL), lambda b: (b, 0, 0)), _resident((1, D_MODEL)),
                  _resident((D_MODEL, 2 * X_W)), _resident((1, HEAD_DIM))],
        out_specs=[blk, blk],
        out_shape=[out, out],
        compiler_params=pltpu.CompilerParams(dimension_semantics=("parallel",),
                                             vmem_limit_bytes=VMEM_LIMIT),
        name="mem_kv",
    )(mem, g_mem, w_ckv, ckg)


def _mix_cross_kernel(x_ref, a1_ref, a4_ref, a16_ref, s1_ref, s4_ref, s16_ref, y_ref, wo_ref,
                      gc_ref, wq_ref, qg_ref, km_ref, vm_ref, wc_ref, h_ref,
                      acc_tok, st_tok, cat_ref, o_ref, *, tm):
    for gi, (a_ref, s_ref, d) in enumerate(((a4_ref, s4_ref, 4), (a16_ref, s16_ref, 16))):
        n = tm // d
        for r in range(d):
            dst = pl.ds(r, n, stride=d)
            st_tok[gi, dst, :] = s_ref[:, r * LANES:(r + 1) * LANES]
            for h in range(ATT_HEADS):
                c0 = r * KV_W + h * HEAD_DIM
                acc_tok[gi, h, dst, :] = a_ref[:, c0:c0 + HEAD_DIM].astype(F32)

    stats = [s1_ref[...], st_tok[0], st_tok[1]]
    m_all = jnp.maximum(jnp.maximum(stats[0], stats[1]), stats[2])
    wts = [jnp.exp(s - m_all) for s in stats]
    for h in range(ATT_HEADS):
        sl = slice(h * HEAD_DIM, (h + 1) * HEAD_DIM)
        accs = (a1_ref[:, sl].astype(F32), acc_tok[0, h], acc_tok[1, h])
        num = jnp.zeros((tm, HEAD_DIM), F32)
        den = jnp.zeros((tm, 1), F32)
        for g in range(len(DILATED_PAIRS)):
            w = wts[g][:, h:h + 1]
            num = num + w * accs[g]
            den = den + w * stats[g][:, ATT_HEADS + h:ATT_HEADS + h + 1]
        cat_ref[:, sl] = (num / den).astype(BF16)
    cat_ref[:, KV_W:] = y_ref[...]
    h1 = x_ref[...] + jnp.dot(cat_ref[...], wo_ref[...], preferred_element_type=F32)

    hn = _rms(h1, gc_ref[...]).astype(BF16)
    qc = jnp.dot(hn, wq_ref[...], preferred_element_type=F32)
    for h in range(X_HEADS):
        sl = slice(h * HEAD_DIM, (h + 1) * HEAD_DIM)
        qh = _rms(qc[:, sl], qg_ref[...]).astype(BF16)
        s = lax.dot_general(qh, km_ref[0, :, sl], (((1,), (1,)), ((), ())),
                            preferred_element_type=F32)
        p = jnp.exp(s - jnp.max(s, axis=-1, keepdims=True))
        l = jnp.sum(p, axis=-1, keepdims=True)
        o = jnp.dot(p.astype(BF16), vm_ref[0, :, sl], preferred_element_type=F32) / l
        o_ref[:, sl] = o.astype(BF16)
    h_ref[...] = h1 + jnp.dot(o_ref[...], wc_ref[...], preferred_element_type=F32)


def _mix_cross(x2, accs, stats, y, w_out, g_cross, w_cq, cqg, k_mem, v_mem, w_co, seq):
    n_tok = x2.shape[0]
    tm = 512
    tiles_per_seq = seq // tm
    n_mem = k_mem.shape[1]
    tok = lambda w: pl.BlockSpec((tm, w), lambda i: (i, 0))
    view = lambda d, w: pl.BlockSpec((tm // d, d * w), lambda i: (i, 0))
    mem_blk = pl.BlockSpec((1, n_mem, X_W), lambda i: (i // tiles_per_seq, 0, 0))
    return pl.pallas_call(
        functools.partial(_mix_cross_kernel, tm=tm),
        grid=(n_tok // tm,),
        in_specs=[tok(D_MODEL)] + [view(d, KV_W) for d in DILATIONS]
                 + [view(d, LANES) for d in DILATIONS]
                 + [tok(POOL_W), _resident((KV_W + POOL_W, D_MODEL)), _resident((1, D_MODEL)),
                    _resident((D_MODEL, X_W)), _resident((1, HEAD_DIM)), mem_blk, mem_blk,
                    _resident((X_W, D_MODEL))],
        out_specs=tok(D_MODEL),
        out_shape=jax.ShapeDtypeStruct((n_tok, D_MODEL), F32),
        scratch_shapes=[pltpu.VMEM((2, ATT_HEADS, tm, HEAD_DIM), F32),
                        pltpu.VMEM((2, tm, LANES), F32),
                        pltpu.VMEM((tm, KV_W + POOL_W), BF16),
                        pltpu.VMEM((tm, X_W), BF16)],
        compiler_params=pltpu.CompilerParams(dimension_semantics=("parallel",),
                                             vmem_limit_bytes=VMEM_LIMIT),
        name="mix_cross",
    )(x2, *accs, *stats, y, w_out, g_cross, w_cq, cqg, k_mem, v_mem, w_co)


def _ffn_kernel(h_ref, g_ref, wgu_ref, wd_ref, o_ref):
    h = h_ref[...]
    hn = _rms(h, g_ref[...]).astype(BF16)
    gate = jnp.dot(hn, wgu_ref[:, :D_FF], preferred_element_type=F32)
    up = jnp.dot(hn, wgu_ref[:, D_FF:], preferred_element_type=F32)
    act = (gate * jax.nn.sigmoid(gate) * up).astype(BF16)
    o_ref[...] = h + jnp.dot(act, wd_ref[...], preferred_element_type=F32)


def _ffn(h2, g_ffn, w_gate_up, w_down):
    n_tok = h2.shape[0]
    tm = 256
    tok = pl.BlockSpec((tm, D_MODEL), lambda i: (i, 0))
    return pl.pallas_call(
        _ffn_kernel,
        grid=(n_tok // tm,),
        in_specs=[tok, _resident((1, D_MODEL)), _resident((D_MODEL, 2 * D_FF)),
                  _resident((D_FF, D_MODEL))],
        out_specs=tok,
        out_shape=jax.ShapeDtypeStruct((n_tok, D_MODEL), F32),
        compiler_params=pltpu.CompilerParams(dimension_semantics=("parallel",),
                                             vmem_limit_bytes=VMEM_LIMIT),
        name="ffn",
    )(h2, g_ffn, w_gate_up, w_down)


def kernel(x, mem, positions, mix_norm_g, w_in, q_norm_g, k_norm_g, pool_w, pool_scale, w_out,
           cross_norm_g, mem_norm_g, w_cq, w_ckv, cq_norm_g, ck_norm_g, w_co,
           ffn_norm_g, w_gate_up, w_down):
    batch, seq, _ = x.shape
    depth = w_in.shape[0]
    scale = HEAD_DIM ** -0.5
    cos, sin = _rope_tables(positions)
    row = lambda a: a.reshape(1, -1)
    order = _rotary_lane_order()
    n_qk_heads = (len(DILATED_PAIRS) + 1) * ATT_HEADS
    h = x.reshape(batch * seq, D_MODEL)
    for layer in range(depth):
        w_qk = w_in[layer][:, :n_qk_heads * HEAD_DIM].reshape(D_MODEL, n_qk_heads, HEAD_DIM)[:, :, order]
        w_in_l = jnp.concatenate([w_qk.reshape(D_MODEL, -1), w_in[layer][:, n_qk_heads * HEAD_DIM:]],
                                 axis=1).astype(BF16)
        q1, k1, v1, q4, k4, v4, q16, k16, v16, y = _in_proj(
            h, row(mix_norm_g[layer]), w_in_l, row(q_norm_g[layer][order] * scale),
            row(k_norm_g[layer][order]), cos, sin, pool_w[layer].astype(BF16),
            row(pool_scale[layer]), seq)
        accs, stats = [], []
        for (qv, kv, vv), dilation in zip(((q1, k1, v1), (q4, k4, v4), (q16, k16, v16)), DILATIONS):
            a, s = _dilated_attn(qv, kv, vv, batch, seq, dilation)
            accs.append(a)
            stats.append(s)
        k_mem, v_mem = _mem_kv(mem, row(mem_norm_g[layer]), w_ckv[layer].astype(BF16),
                               row(ck_norm_g[layer]))
        h = _mix_cross(h, accs, stats, y, w_out[layer].astype(BF16), row(cross_norm_g[layer]),
                       w_cq[layer].astype(BF16), row(cq_norm_g[layer] * scale), k_mem, v_mem,
                       w_co[layer].astype(BF16), seq)
        h = _ffn(h, row(ffn_norm_g[layer]), w_gate_up[layer].astype(BF16),
                 w_down[layer].astype(BF16))
    return h.reshape(batch, seq, D_MODEL)
```

```python
import functools

import jax
import jax.numpy as jnp
import numpy as np
from jax import lax
from jax.experimental import pallas as pl
from jax.experimental.pallas import tpu as pltpu

D_MODEL = 1024
HEAD_DIM = 128
ATT_HEADS = 4
DILATED_PAIRS = ((128, 1), (512, 4), (2048, 16))
DILATIONS = tuple(d for _, d in DILATED_PAIRS)
KV_W = ATT_HEADS * HEAD_DIM
POOL_WINDOWS = (2, 4, 8, 16)
POOL_W = len(POOL_WINDOWS) * HEAD_DIM
IN_W = (len(DILATED_PAIRS) + 2) * KV_W + POOL_W
ROT_DIM = HEAD_DIM // 4
ROT_HALF = ROT_DIM // 2
ROPE_THETA = 500000.0
X_HEADS = 4
X_W = X_HEADS * HEAD_DIM
D_FF = 2816
EPS = 1e-6
NEG_INF = -1e30
ATT_BLOCK = 128
LANES = 128
SUBLANES = 8
ROT_PAIR_LANE = LANES // 2
ROW_CHUNK = 128
POOL_HALO = 32

F32 = jnp.float32
BF16 = jnp.bfloat16

VMEM_LIMIT = 56 * 1024 * 1024


def _rms(x, g):
    ms = jnp.mean(x * x, axis=-1, keepdims=True)
    return x * lax.rsqrt(ms + EPS) * g


def _resident(shape):
    nd = len(shape)
    return pl.BlockSpec(shape, lambda *_: (0,) * nd, pipeline_mode=pl.Buffered(1))


def _rotary_lane_order():
    rest = np.arange(ROT_DIM, HEAD_DIM)
    n_mid = ROT_PAIR_LANE - ROT_HALF
    return np.concatenate([np.arange(ROT_HALF), rest[:n_mid],
                           np.arange(ROT_HALF, ROT_DIM), rest[n_mid:]])


def _rope_table_kernel(pos_ref, invf_ref, cos_ref, sin_ref):
    ang = pos_ref[...] * invf_ref[...]
    cos_ref[...] = jnp.cos(ang)
    sin_ref[...] = jnp.sin(ang)


def _rope_tables(positions):
    n_tok = positions.size
    per_row = LANES // ROT_HALF
    rows = n_tok // per_row
    pos = jnp.repeat(positions.reshape(-1).astype(F32), ROT_HALF).reshape(rows, LANES)
    inv_freq = ROPE_THETA ** (-jnp.arange(0, ROT_DIM, 2, dtype=F32) / ROT_DIM)
    invf = jnp.tile(inv_freq, per_row).reshape(1, LANES)
    blk = 256
    cos, sin = pl.pallas_call(
        _rope_table_kernel,
        grid=(rows // blk,),
        in_specs=[pl.BlockSpec((blk, LANES), lambda i: (i, 0)),
                  pl.BlockSpec((1, LANES), lambda i: (0, 0))],
        out_specs=[pl.BlockSpec((blk, LANES), lambda i: (i, 0))] * 2,
        out_shape=[jax.ShapeDtypeStruct((rows, LANES), F32)] * 2,
        name="rope_tables",
    )(pos, invf)
    return cos.reshape(n_tok, ROT_HALF), sin.reshape(n_tok, ROT_HALF)


def _in_proj_kernel(x_ref, g_ref, w_ref, qg_ref, kg_ref, cos_ref, sin_ref, pw_ref, ps_ref,
                    q1_ref, k1_ref, v1_ref, q4_ref, k4_ref, v4_ref, q16_ref, k16_ref, v16_ref, y_ref,
                    xn_ref, tab_ref, pa_ref, pb_ref, hist_ref, lvl_ref, stage_ref, stage4_ref,
                    *, tm, tiles_per_seq):
    step = pl.program_id(0)
    rc = ROW_CHUNK
    n_rc = tm // rc

    def project(p_ref):
        for i in range(n_rc):
            rows = slice(i * rc, (i + 1) * rc)
            xn_ref[rows, :] = _rms(x_ref[rows, :], g_ref[...]).astype(BF16)
        for c in range(IN_W // KV_W):
            cols = slice(c * KV_W, (c + 1) * KV_W)
            p_ref[:, cols] = jnp.dot(xn_ref[...], w_ref[:, cols], preferred_element_type=F32)

    def finish(p_ref):
        lane = lax.broadcasted_iota(jnp.int32, (tm, LANES), 1)
        fill = jnp.zeros((tm, LANES - ROT_HALF), F32)
        cs = jnp.concatenate([cos_ref[...], fill], axis=1)
        sn = jnp.concatenate([sin_ref[...], fill], axis=1)
        first = lane < ROT_HALF
        second = jnp.logical_and(lane >= ROT_PAIR_LANE, lane < ROT_PAIR_LANE + ROT_HALF)
        c_tab = jnp.where(first, cs, jnp.where(second, pltpu.roll(cs, ROT_PAIR_LANE, 1), 1.0))
        s_tab = jnp.where(first, -sn, jnp.where(second, pltpu.roll(sn, ROT_PAIR_LANE, 1), 0.0))
        for gi, gain_ref in enumerate((qg_ref, kg_ref)):
            gain = gain_ref[...]
            tab_ref[2 * gi] = c_tab * gain
            tab_ref[2 * gi + 1] = s_tab * pltpu.roll(gain, ROT_PAIR_LANE, 1)

        def emit(c, gi, tok_ref, v4_ref, v16_ref, a):
            for h in range(ATT_HEADS):
                sl = slice(h * HEAD_DIM, (h + 1) * HEAD_DIM)
                src = slice(c * KV_W + h * HEAD_DIM, c * KV_W + (h + 1) * HEAD_DIM)
                for i in range(n_rc):
                    rows = slice(i * rc, (i + 1) * rc)
                    t = p_ref[rows, src]
                    if gi >= 0:
                        rs = lax.rsqrt(jnp.mean(t * t, axis=-1, keepdims=True) + EPS)
                        t = (t * tab_ref[2 * gi, rows, :]
                             + pltpu.roll(t, ROT_PAIR_LANE, 1) * tab_ref[2 * gi + 1, rows, :]) * rs
                    if tok_ref is not None:
                        tok_ref[rows, sl] = t.astype(BF16)
                    if v4_ref is not None or v16_ref is not None:
                        stage_ref[a, h, rows, :] = t
                if v4_ref is None and v16_ref is None:
                    continue
                n4 = tm // 4
                for b in range(4):
                    t4 = stage_ref[a, h, pl.ds(b, n4, stride=4), :]
                    if v4_ref is not None:
                        c0 = b * KV_W + h * HEAD_DIM
                        v4_ref[:, c0:c0 + HEAD_DIM] = t4.astype(BF16)
                    if v16_ref is not None:
                        stage4_ref[a, h, b] = t4
                if v16_ref is not None:
                    for b in range(4):
                        for a4 in range(4):
                            c0 = (4 * a4 + b) * KV_W + h * HEAD_DIM
                            v16_ref[:, c0:c0 + HEAD_DIM] = (
                                stage4_ref[a, h, b, pl.ds(a4, n4 // 4, stride=4), :].astype(BF16))

        emit(3, 1, k1_ref, k4_ref, k16_ref, 0)
        emit(4, -1, v1_ref, v4_ref, v16_ref, 1)
        emit(2, 0, None, None, q16_ref, 2)
        emit(1, 0, None, q4_ref, None, 3)
        emit(0, 0, q1_ref, None, None, 0)

        seq_tile = jnp.maximum(step - 1, 0) % tiles_per_seq
        end = POOL_HALO + tm
        hist_ref[0:POOL_HALO, :] = jnp.where(seq_tile == 0, 0.0, 1.0) * hist_ref[tm:end, :]
        u_cols = slice(IN_W - POOL_W, IN_W)
        hist_ref[POOL_HALO:end, :] = p_ref[:, u_cols]
        row16 = lax.broadcasted_iota(jnp.int32, (2 * SUBLANES, HEAD_DIM), 0)
        for g, w in enumerate(POOL_WINDOWS):
            sl = slice(g * HEAD_DIM, (g + 1) * HEAD_DIM)
            src, start, shift = hist_ref, SUBLANES, 1
            while True:
                cols = sl if src is hist_ref else slice(None)
                tot = src[start:end, cols] + src[start - shift:end - shift, cols]
                shift *= 2
                if shift == w:
                    break
                lvl_ref[g, start:end, :] = tot
                src, start = lvl_ref.at[g], start + SUBLANES
            tot = tot[POOL_HALO - start:, :]
            n_first = jnp.maximum(jnp.minimum(row16 + 1, w), jnp.where(seq_tile == 0, 0, w))
            inv_first = 1.0 / n_first.astype(F32)
            ug = hist_ref[POOL_HALO:end, sl]
            d = jnp.concatenate([tot[:2 * SUBLANES] * inv_first, tot[2 * SUBLANES:] * (1.0 / w)],
                                axis=0) - ug
            yg = jnp.dot(d.astype(BF16), pw_ref[g], preferred_element_type=F32) * ps_ref[:, sl]
            y_ref[:, sl] = yg.astype(BF16)

    @pl.when(step == 0)
    def _():
        pb_ref[...] = jnp.zeros(pb_ref.shape, F32)
        hist_ref[...] = jnp.zeros(hist_ref.shape, F32)

    @pl.when(step % 2 == 0)
    def _():
        project(pa_ref)
        finish(pb_ref)

    @pl.when(step % 2 == 1)
    def _():
        project(pb_ref)
        finish(pa_ref)


def _in_proj(x2, g_mix, w_in, qg, kg, cos, sin, pool_w, pool_scale, seq):
    n_tok = x2.shape[0]
    tm = 512
    n_tiles = n_tok // tm
    tiles_per_seq = seq // tm
    ahead = lambda i: (jnp.minimum(i, n_tiles - 1), 0)
    behind = lambda i: (jnp.maximum(i - 1, 0), 0)
    view = lambda d: pl.BlockSpec((tm // d, d * KV_W), behind)
    view_sds = lambda d: jax.ShapeDtypeStruct((n_tok // d, d * KV_W), BF16)
    out_d = (1, 1, 1, 4, 4, 4, 16, 16, 16, 1)
    return pl.pallas_call(
        functools.partial(_in_proj_kernel, tm=tm, tiles_per_seq=tiles_per_seq),
        grid=(n_tiles + 1,),
        in_specs=[pl.BlockSpec((tm, D_MODEL), ahead), _resident((1, D_MODEL)),
                  _resident((D_MODEL, IN_W)), _resident((1, HEAD_DIM)), _resident((1, HEAD_DIM)),
                  pl.BlockSpec((tm, ROT_HALF), behind), pl.BlockSpec((tm, ROT_HALF), behind),
                  _resident((len(POOL_WINDOWS), HEAD_DIM, HEAD_DIM)), _resident((1, POOL_W))],
        out_specs=[view(d) for d in out_d],
        out_shape=[view_sds(d) for d in out_d],
        scratch_shapes=[pltpu.VMEM((tm, D_MODEL), BF16),
                        pltpu.VMEM((4, tm, LANES), F32),
                        pltpu.VMEM((tm, IN_W), F32),
                        pltpu.VMEM((tm, IN_W), F32),
                        pltpu.VMEM((POOL_HALO + tm, POOL_W), F32),
                        pltpu.VMEM((len(POOL_WINDOWS), POOL_HALO + tm, HEAD_DIM), F32),
                        pltpu.VMEM((4, ATT_HEADS, tm, HEAD_DIM), F32),
                        pltpu.VMEM((3, ATT_HEADS, 4, tm // 4, HEAD_DIM), F32)],
        compiler_params=pltpu.CompilerParams(dimension_semantics=("arbitrary",),
                                             vmem_limit_bytes=VMEM_LIMIT),
        name="in_proj",
    )(x2, g_mix, w_in, qg, kg, cos, sin, pool_w, pool_scale)


def _dilated_attn_kernel(q_ref, k_ref, kh_ref, v_ref, vh_ref, o_ref, st_ref, *, tl, rb):
    n_sub = tl // ATT_BLOCK
    qi = lax.broadcasted_iota(jnp.int32, (ATT_BLOCK, 2 * ATT_BLOCK), 0)
    kj = lax.broadcasted_iota(jnp.int32, (ATT_BLOCK, 2 * ATT_BLOCK), 1)
    delta = qi + ATT_BLOCK - kj
    in_band = jnp.logical_and(delta >= 0, delta <= ATT_BLOCK)
    bias = jnp.where(in_band, 0.0, NEG_INF).astype(F32)
    n_missing = jnp.where(pl.program_id(2) == 0, ATT_BLOCK, 0)
    bias_first = jnp.where(kj < n_missing, NEG_INF, bias)
    lane = lax.broadcasted_iota(jnp.int32, (ATT_BLOCK, LANES), 1)

    for r in range(rb):
        for i in range(n_sub):
            rows = slice(i * ATT_BLOCK, (i + 1) * ATT_BLOCK)
            rows2 = slice((i - 1) * ATT_BLOCK, (i + 1) * ATT_BLOCK)
            st = jnp.zeros((ATT_BLOCK, LANES), F32)
            for h in range(ATT_HEADS):
                c0 = r * KV_W + h * HEAD_DIM
                sl = slice(c0, c0 + HEAD_DIM)
                if i == 0:
                    k_blk = jnp.concatenate([kh_ref[0, :, sl], k_ref[0, rows, sl]], axis=0)
                    v_blk = jnp.concatenate([vh_ref[0, :, sl], v_ref[0, rows, sl]], axis=0)
                    b = bias_first
                else:
                    k_blk, v_blk, b = k_ref[0, rows2, sl], v_ref[0, rows2, sl], bias
                s = lax.dot_general(q_ref[0, rows, sl], k_blk, (((1,), (1,)), ((), ())),
                                    preferred_element_type=F32) + b
                m = jnp.max(s, axis=-1, keepdims=True)
                p = jnp.exp(s - m)
                l = jnp.sum(p, axis=-1, keepdims=True)
                acc = jnp.dot(p.astype(BF16), v_blk, preferred_element_type=F32)
                o_ref[0, rows, sl] = acc.astype(BF16)
                st = jnp.where(lane == h, m, st)
                st = jnp.where(lane == ATT_HEADS + h, l, st)
            st_ref[0, rows, r * LANES:(r + 1) * LANES] = st


def _dilated_attn(q, k, v, batch, seq, dilation):
    sub_len = seq // dilation
    tl = min(sub_len, 512)
    rb = min(dilation, 1024 // tl)
    n_sub = tl // ATT_BLOCK
    view = lambda a: a.reshape(batch, sub_len, a.shape[1])
    cur = lambda w: pl.BlockSpec((1, tl, rb * w), lambda b, r, i: (b, i, r))
    halo = pl.BlockSpec((1, ATT_BLOCK, rb * KV_W),
                        lambda b, r, i: (b, jnp.maximum(i * n_sub - 1, 0), r))
    acc, st = pl.pallas_call(
        functools.partial(_dilated_attn_kernel, tl=tl, rb=rb),
        grid=(batch, dilation // rb, sub_len // tl),
        in_specs=[cur(KV_W), cur(KV_W), halo, cur(KV_W), halo],
        out_specs=[cur(KV_W), cur(LANES)],
        out_shape=[jax.ShapeDtypeStruct((batch, sub_len, dilation * KV_W), BF16),
                   jax.ShapeDtypeStruct((batch, sub_len, dilation * LANES), F32)],
        compiler_params=pltpu.CompilerParams(
            dimension_semantics=("parallel", "parallel", "parallel"), vmem_limit_bytes=VMEM_LIMIT),
        name=f"dilated_attn_d{dilation}",
    )(view(q), view(k), view(k), view(v), view(v))
    return (acc.reshape(batch * sub_len, dilation * KV_W), st.reshape(batch * sub_len, dilation * LANES))


def _mem_kv_kernel(mem_ref, g_ref, w_ref, kg_ref, k_ref, v_ref):
    mn = _rms(mem_ref[0], g_ref[...]).astype(BF16)
    kv = jnp.dot(mn, w_ref[...], preferred_element_type=F32)
    for h in range(X_HEADS):
        sl = slice(h * HEAD_DIM, (h + 1) * HEAD_DIM)
        k_ref[0, :, sl] = _rms(kv[:, sl], kg_ref[...]).astype(BF16)
    v_ref[0] = kv[:, X_W:].astype(BF16)


def _mem_kv(mem, g_mem, w_ckv, ckg):
    batch, n_mem, _ = mem.shape
    out = jax.ShapeDtypeStruct((batch, n_mem, X_W), BF16)
    blk = pl.BlockSpec((1, n_mem, X_W), lambda b: (b, 0, 0))
    return pl.pallas_call(
        _mem_kv_kernel,
        grid=(batch,),
        in_specs=[pl.BlockSpec((1, n_mem, D_MODEL), lambda b: (b, 0, 0)), _resident((1, D_MODEL)),
                  _resident((D_MODEL, 2 * X_W)), _resident((1, HEAD_DIM))],
        out_specs=[blk, blk],
        out_shape=[out, out],
        compiler_params=pltpu.CompilerParams(dimension_semantics=("parallel",),
                                             vmem_limit_bytes=VMEM_LIMIT),
        name="mem_kv",
    )(mem, g_mem, w_ckv, ckg)


def _mix_cross_kernel(x_ref, a1_ref, a4_ref, a16_ref, s1_ref, s4_ref, s16_ref, y_ref, wo_ref,
                      gc_ref, wq_ref, qg_ref, km_ref, vm_ref, wc_ref, h_ref,
                      acc_tok, st_tok, cat_a, cat_b, o_ref, *, tm):
    step = pl.program_id(0)

    def merge(cat_ref):
        for gi, (a_ref, s_ref, d) in enumerate(((a4_ref, s4_ref, 4), (a16_ref, s16_ref, 16))):
            n = tm // d
            for r in range(d):
                dst = pl.ds(r, n, stride=d)
                st_tok[gi, dst, :] = s_ref[:, r * LANES:(r + 1) * LANES]
                for h in range(ATT_HEADS):
                    c0 = r * KV_W + h * HEAD_DIM
                    acc_tok[gi, h, dst, :] = a_ref[:, c0:c0 + HEAD_DIM].astype(F32)

        stats = [s1_ref[...], st_tok[0], st_tok[1]]
        m_all = jnp.maximum(jnp.maximum(stats[0], stats[1]), stats[2])
        wts = [jnp.exp(s - m_all) for s in stats]
        for h in range(ATT_HEADS):
            sl = slice(h * HEAD_DIM, (h + 1) * HEAD_DIM)
            accs = (a1_ref[:, sl].astype(F32), acc_tok[0, h], acc_tok[1, h])
            num = jnp.zeros((tm, HEAD_DIM), F32)
            den = jnp.zeros((tm, 1), F32)
            for g in range(len(DILATED_PAIRS)):
                w = wts[g][:, h:h + 1]
                num = num + w * accs[g]
                den = den + w * stats[g][:, ATT_HEADS + h:ATT_HEADS + h + 1]
            cat_ref[:, sl] = (num / den).astype(BF16)
        cat_ref[:, KV_W:] = y_ref[...]

    def mix(cat_ref):
        h1 = x_ref[...] + jnp.dot(cat_ref[...], wo_ref[...], preferred_element_type=F32)
        hn = _rms(h1, gc_ref[...]).astype(BF16)
        qc = jnp.dot(hn, wq_ref[...], preferred_element_type=F32)
        for h in range(X_HEADS):
            sl = slice(h * HEAD_DIM, (h + 1) * HEAD_DIM)
            qh = _rms(qc[:, sl], qg_ref[...]).astype(BF16)
            s = lax.dot_general(qh, km_ref[0, :, sl], (((1,), (1,)), ((), ())),
                                preferred_element_type=F32)
            p = jnp.exp(s - jnp.max(s, axis=-1, keepdims=True))
            l = jnp.sum(p, axis=-1, keepdims=True)
            o = jnp.dot(p.astype(BF16), vm_ref[0, :, sl], preferred_element_type=F32) / l
            o_ref[:, sl] = o.astype(BF16)
        h_ref[...] = h1 + jnp.dot(o_ref[...], wc_ref[...], preferred_element_type=F32)

    @pl.when(step == 0)
    def _():
        cat_b[...] = jnp.zeros(cat_b.shape, BF16)

    @pl.when(step % 2 == 0)
    def _():
        merge(cat_a)
        mix(cat_b)

    @pl.when(step % 2 == 1)
    def _():
        merge(cat_b)
        mix(cat_a)


def _mix_cross(x2, accs, stats, y, w_out, g_cross, w_cq, cqg, k_mem, v_mem, w_co, seq):
    n_tok = x2.shape[0]
    tm = 512
    n_tiles = n_tok // tm
    tiles_per_seq = seq // tm
    n_mem = k_mem.shape[1]
    ahead = lambda i: (jnp.minimum(i, n_tiles - 1), 0)
    behind = lambda i: (jnp.maximum(i - 1, 0), 0)
    view = lambda d, w: pl.BlockSpec((tm // d, d * w), ahead)
    mem_blk = pl.BlockSpec((1, n_mem, X_W), lambda i: (jnp.maximum(i - 1, 0) // tiles_per_seq, 0, 0))
    return pl.pallas_call(
        functools.partial(_mix_cross_kernel, tm=tm),
        grid=(n_tiles + 1,),
        in_specs=[pl.BlockSpec((tm, D_MODEL), behind)] + [view(d, KV_W) for d in DILATIONS]
                 + [view(d, LANES) for d in DILATIONS]
                 + [view(1, POOL_W), _resident((KV_W + POOL_W, D_MODEL)), _resident((1, D_MODEL)),
                    _resident((D_MODEL, X_W)), _resident((1, HEAD_DIM)), mem_blk, mem_blk,
                    _resident((X_W, D_MODEL))],
        out_specs=pl.BlockSpec((tm, D_MODEL), behind),
        out_shape=jax.ShapeDtypeStruct((n_tok, D_MODEL), F32),
        scratch_shapes=[pltpu.VMEM((2, ATT_HEADS, tm, HEAD_DIM), F32),
                        pltpu.VMEM((2, tm, LANES), F32),
                        pltpu.VMEM((tm, KV_W + POOL_W), BF16),
                        pltpu.VMEM((tm, KV_W + POOL_W), BF16),
                        pltpu.VMEM((tm, X_W), BF16)],
        compiler_params=pltpu.CompilerParams(dimension_semantics=("arbitrary",),
                                             vmem_limit_bytes=VMEM_LIMIT),
        name="mix_cross",
    )(x2, *accs, *stats, y, w_out, g_cross, w_cq, cqg, k_mem, v_mem, w_co)


def _ffn_kernel(h_ref, g_ref, wgu_ref, wd_ref, o_ref):
    h = h_ref[...]
    hn = _rms(h, g_ref[...]).astype(BF16)
    gate = jnp.dot(hn, wgu_ref[:, :D_FF], preferred_element_type=F32)
    up = jnp.dot(hn, wgu_ref[:, D_FF:], preferred_element_type=F32)
    act = (gate * jax.nn.sigmoid(gate) * up).astype(BF16)
    o_ref[...] = h + jnp.dot(act, wd_ref[...], preferred_element_type=F32)


def _ffn(h2, g_ffn, w_gate_up, w_down):
    n_tok = h2.shape[0]
    tm = 512
    tok = pl.BlockSpec((tm, D_MODEL), lambda i: (i, 0))
    return pl.pallas_call(
        _ffn_kernel,
        grid=(n_tok // tm,),
        in_specs=[tok, _resident((1, D_MODEL)), _resident((D_MODEL, 2 * D_FF)),
                  _resident((D_FF, D_MODEL))],
        out_specs=tok,
        out_shape=jax.ShapeDtypeStruct((n_tok, D_MODEL), F32),
        compiler_params=pltpu.CompilerParams(dimension_semantics=("parallel",),
                                             vmem_limit_bytes=VMEM_LIMIT),
        name="ffn",
    )(h2, g_ffn, w_gate_up, w_down)


def kernel(x, mem, positions, mix_norm_g, w_in, q_norm_g, k_norm_g, pool_w, pool_scale, w_out,
           cross_norm_g, mem_norm_g, w_cq, w_ckv, cq_norm_g, ck_norm_g, w_co,
           ffn_norm_g, w_gate_up, w_down):
    batch, seq, _ = x.shape
    depth = w_in.shape[0]
    scale = HEAD_DIM ** -0.5
    cos, sin = _rope_tables(positions)
    row = lambda a: a.reshape(1, -1)
    order = _rotary_lane_order()
    n_qk_cols = (len(DILATED_PAIRS) + 1) * KV_W
    col_order = np.arange(IN_W)
    col_order[:n_qk_cols] = (np.arange(0, n_qk_cols, HEAD_DIM)[:, None] + order[None, :]).reshape(-1)
    h = x.reshape(batch * seq, D_MODEL)
    for layer in range(depth):
        w_in_l = w_in[layer][:, col_order].astype(BF16)
        q1, k1, v1, q4, k4, v4, q16, k16, v16, y = _in_proj(
            h, row(mix_norm_g[layer]), w_in_l, row(q_norm_g[layer][order] * scale),
            row(k_norm_g[layer][order]), cos, sin, pool_w[layer].astype(BF16),
            row(pool_scale[layer]), seq)
        accs, stats = [], []
        for (qv, kv, vv), dilation in zip(((q1, k1, v1), (q4, k4, v4), (q16, k16, v16)), DILATIONS):
            a, s = _dilated_attn(qv, kv, vv, batch, seq, dilation)
            accs.append(a)
            stats.append(s)
        k_mem, v_mem = _mem_kv(mem, row(mem_norm_g[layer]), w_ckv[layer].astype(BF16),
                               row(ck_norm_g[layer]))
        h = _mix_cross(h, accs, stats, y, w_out[layer].astype(BF16), row(cross_norm_g[layer]),
                       w_cq[layer].astype(BF16), row(cq_norm_g[layer] * scale), k_mem, v_mem,
                       w_co[layer].astype(BF16), seq)
        h = _ffn(h, row(ffn_norm_g[layer]), w_gate_up[layer].astype(BF16),
                 w_down[layer].astype(BF16))
    return h.reshape(batch, seq, D_MODEL)
```

```python
import functools

import jax
import jax.numpy as jnp
import numpy as np
from jax import lax
from jax.experimental import pallas as pl
from jax.experimental.pallas import tpu as pltpu

D_MODEL = 1024
HEAD_DIM = 128
ATT_HEADS = 4
DILATED_PAIRS = ((128, 1), (512, 4), (2048, 16))
DILATIONS = tuple(d for _, d in DILATED_PAIRS)
KV_W = ATT_HEADS * HEAD_DIM
POOL_WINDOWS = (2, 4, 8, 16)
POOL_W = len(POOL_WINDOWS) * HEAD_DIM
IN_W = (len(DILATED_PAIRS) + 2) * KV_W + POOL_W
ROT_DIM = HEAD_DIM // 4
ROT_HALF = ROT_DIM // 2
ROPE_THETA = 500000.0
X_HEADS = 4
X_W = X_HEADS * HEAD_DIM
D_FF = 2816
EPS = 1e-6
NEG_INF = -1e30
ATT_BLOCK = 128
LANES = 128
SUBLANES = 8
ROT_PAIR_LANE = LANES // 2
ROW_CHUNK = 128
POOL_HALO = 32

F32 = jnp.float32
BF16 = jnp.bfloat16

VMEM_LIMIT = 56 * 1024 * 1024


def _rms(x, g):
    ms = jnp.mean(x * x, axis=-1, keepdims=True)
    return x * lax.rsqrt(ms + EPS) * g


def _resident(shape):
    nd = len(shape)
    return pl.BlockSpec(shape, lambda *_: (0,) * nd, pipeline_mode=pl.Buffered(1))


def _rotary_lane_order():
    rest = np.arange(ROT_DIM, HEAD_DIM)
    n_mid = ROT_PAIR_LANE - ROT_HALF
    return np.concatenate([np.arange(ROT_HALF), rest[:n_mid],
                           np.arange(ROT_HALF, ROT_DIM), rest[n_mid:]])


def _rope_table_kernel(pos_ref, invf_ref, cos_ref, sin_ref):
    ang = pos_ref[...] * invf_ref[...]
    cos_ref[...] = jnp.cos(ang)
    sin_ref[...] = jnp.sin(ang)


def _rope_tables(positions):
    n_tok = positions.size
    per_row = LANES // ROT_HALF
    rows = n_tok // per_row
    pos = jnp.repeat(positions.reshape(-1).astype(F32), ROT_HALF).reshape(rows, LANES)
    inv_freq = ROPE_THETA ** (-jnp.arange(0, ROT_DIM, 2, dtype=F32) / ROT_DIM)
    invf = jnp.tile(inv_freq, per_row).reshape(1, LANES)
    blk = 256
    cos, sin = pl.pallas_call(
        _rope_table_kernel,
        grid=(rows // blk,),
        in_specs=[pl.BlockSpec((blk, LANES), lambda i: (i, 0)),
                  pl.BlockSpec((1, LANES), lambda i: (0, 0))],
        out_specs=[pl.BlockSpec((blk, LANES), lambda i: (i, 0))] * 2,
        out_shape=[jax.ShapeDtypeStruct((rows, LANES), F32)] * 2,
        name="rope_tables",
    )(pos, invf)
    return cos.reshape(n_tok, ROT_HALF), sin.reshape(n_tok, ROT_HALF)


def _in_proj_kernel(x_ref, g_ref, w_ref, qg_ref, kg_ref, cos_ref, sin_ref, pw_ref, ps_ref,
                    q1_ref, k1_ref, v1_ref, q4_ref, k4_ref, v4_ref, q16_ref, k16_ref, v16_ref, y_ref,
                    xn_ref, tab_ref, pa_ref, pb_ref, hist_ref, lvl_ref, stage_ref, stage4_ref,
                    *, tm, tiles_per_seq):
    step = pl.program_id(0)
    rc = ROW_CHUNK
    n_rc = tm // rc

    def project(p_ref):
        for i in range(n_rc):
            rows = slice(i * rc, (i + 1) * rc)
            xn_ref[rows, :] = _rms(x_ref[rows, :], g_ref[...]).astype(BF16)
        for c in range(IN_W // KV_W):
            cols = slice(c * KV_W, (c + 1) * KV_W)
            p_ref[:, cols] = jnp.dot(xn_ref[...], w_ref[:, cols], preferred_element_type=F32)

    def finish(p_ref):
        lane = lax.broadcasted_iota(jnp.int32, (tm, LANES), 1)
        fill = jnp.zeros((tm, LANES - ROT_HALF), F32)
        cs = jnp.concatenate([cos_ref[...], fill], axis=1)
        sn = jnp.concatenate([sin_ref[...], fill], axis=1)
        first = lane < ROT_HALF
        second = jnp.logical_and(lane >= ROT_PAIR_LANE, lane < ROT_PAIR_LANE + ROT_HALF)
        c_tab = jnp.where(first, cs, jnp.where(second, pltpu.roll(cs, ROT_PAIR_LANE, 1), 1.0))
        s_tab = jnp.where(first, -sn, jnp.where(second, pltpu.roll(sn, ROT_PAIR_LANE, 1), 0.0))
        for gi, gain_ref in enumerate((qg_ref, kg_ref)):
            gain = gain_ref[...]
            tab_ref[2 * gi] = c_tab * gain
            tab_ref[2 * gi + 1] = s_tab * pltpu.roll(gain, ROT_PAIR_LANE, 1)

        def emit(c, gi, tok_ref, v4_ref, v16_ref, a):
            for h in range(ATT_HEADS):
                sl = slice(h * HEAD_DIM, (h + 1) * HEAD_DIM)
                src = slice(c * KV_W + h * HEAD_DIM, c * KV_W + (h + 1) * HEAD_DIM)
                for i in range(n_rc):
                    rows = slice(i * rc, (i + 1) * rc)
                    t = p_ref[rows, src]
                    if gi >= 0:
                        rs = lax.rsqrt(jnp.mean(t * t, axis=-1, keepdims=True) + EPS)
                        t = (t * tab_ref[2 * gi, rows, :]
                             + pltpu.roll(t, ROT_PAIR_LANE, 1) * tab_ref[2 * gi + 1, rows, :]) * rs
                    if tok_ref is not None:
                        tok_ref[rows, sl] = t.astype(BF16)
                    if v4_ref is not None or v16_ref is not None:
                        stage_ref[a, h, rows, :] = t
                if v4_ref is None and v16_ref is None:
                    continue
                n4 = tm // 4
                for b in range(4):
                    t4 = stage_ref[a, h, pl.ds(b, n4, stride=4), :]
                    if v4_ref is not None:
                        c0 = b * KV_W + h * HEAD_DIM
                        v4_ref[:, c0:c0 + HEAD_DIM] = t4.astype(BF16)
                    if v16_ref is not None:
                        stage4_ref[a, h, b] = t4
                if v16_ref is not None:
                    for b in range(4):
                        for a4 in range(4):
                            c0 = (4 * a4 + b) * KV_W + h * HEAD_DIM
                            v16_ref[:, c0:c0 + HEAD_DIM] = (
                                stage4_ref[a, h, b, pl.ds(a4, n4 // 4, stride=4), :].astype(BF16))

        emit(3, 1, k1_ref, k4_ref, k16_ref, 0)
        emit(4, -1, v1_ref, v4_ref, v16_ref, 1)
        emit(2, 0, None, None, q16_ref, 2)
        emit(1, 0, None, q4_ref, None, 3)
        emit(0, 0, q1_ref, None, None, 0)

        seq_tile = jnp.maximum(step - 1, 0) % tiles_per_seq
        end = POOL_HALO + tm
        hist_ref[0:POOL_HALO, :] = jnp.where(seq_tile == 0, 0.0, 1.0) * hist_ref[tm:end, :]
        u_cols = slice(IN_W - POOL_W, IN_W)
        hist_ref[POOL_HALO:end, :] = p_ref[:, u_cols]
        row16 = lax.broadcasted_iota(jnp.int32, (2 * SUBLANES, HEAD_DIM), 0)
        for g, w in enumerate(POOL_WINDOWS):
            sl = slice(g * HEAD_DIM, (g + 1) * HEAD_DIM)
            src, start, shift = hist_ref, SUBLANES, 1
            while True:
                cols = sl if src is hist_ref else slice(None)
                tot = src[start:end, cols] + src[start - shift:end - shift, cols]
                shift *= 2
                if shift == w:
                    break
                lvl_ref[g, start:end, :] = tot
                src, start = lvl_ref.at[g], start + SUBLANES
            tot = tot[POOL_HALO - start:, :]
            n_first = jnp.maximum(jnp.minimum(row16 + 1, w), jnp.where(seq_tile == 0, 0, w))
            inv_first = 1.0 / n_first.astype(F32)
            ug = hist_ref[POOL_HALO:end, sl]
            d = jnp.concatenate([tot[:2 * SUBLANES] * inv_first, tot[2 * SUBLANES:] * (1.0 / w)],
                                axis=0) - ug
            yg = jnp.dot(d.astype(BF16), pw_ref[g], preferred_element_type=F32) * ps_ref[:, sl]
            y_ref[:, sl] = yg.astype(BF16)

    @pl.when(step == 0)
    def _():
        pb_ref[...] = jnp.zeros(pb_ref.shape, F32)
        hist_ref[...] = jnp.zeros(hist_ref.shape, F32)

    @pl.when(step % 2 == 0)
    def _():
        project(pa_ref)
        finish(pb_ref)

    @pl.when(step % 2 == 1)
    def _():
        project(pb_ref)
        finish(pa_ref)


def _in_proj(x2, g_mix, w_in, qg, kg, cos, sin, pool_w, pool_scale, seq):
    n_tok = x2.shape[0]
    tm = 512
    n_tiles = n_tok // tm
    tiles_per_seq = seq // tm
    ahead = lambda i: (jnp.minimum(i, n_tiles - 1), 0)
    behind = lambda i: (jnp.maximum(i - 1, 0), 0)
    view = lambda d: pl.BlockSpec((tm // d, d * KV_W), behind)
    view_sds = lambda d: jax.ShapeDtypeStruct((n_tok // d, d * KV_W), BF16)
    out_d = (1, 1, 1, 4, 4, 4, 16, 16, 16, 1)
    return pl.pallas_call(
        functools.partial(_in_proj_kernel, tm=tm, tiles_per_seq=tiles_per_seq),
        grid=(n_tiles + 1,),
        in_specs=[pl.BlockSpec((tm, D_MODEL), ahead), _resident((1, D_MODEL)),
                  _resident((D_MODEL, IN_W)), _resident((1, HEAD_DIM)), _resident((1, HEAD_DIM)),
                  pl.BlockSpec((tm, ROT_HALF), behind), pl.BlockSpec((tm, ROT_HALF), behind),
                  _resident((len(POOL_WINDOWS), HEAD_DIM, HEAD_DIM)), _resident((1, POOL_W))],
        out_specs=[view(d) for d in out_d],
        out_shape=[view_sds(d) for d in out_d],
        scratch_shapes=[pltpu.VMEM((tm, D_MODEL), BF16),
                        pltpu.VMEM((4, tm, LANES), F32),
                        pltpu.VMEM((tm, IN_W), F32),
                        pltpu.VMEM((tm, IN_W), F32),
                        pltpu.VMEM((POOL_HALO + tm, POOL_W), F32),
                        pltpu.VMEM((len(POOL_WINDOWS), POOL_HALO + tm, HEAD_DIM), F32),
                        pltpu.VMEM((4, ATT_HEADS, tm, HEAD_DIM), F32),
                        pltpu.VMEM((3, ATT_HEADS, 4, tm // 4, HEAD_DIM), F32)],
        compiler_params=pltpu.CompilerParams(dimension_semantics=("arbitrary",),
                                             vmem_limit_bytes=VMEM_LIMIT),
        name="in_proj",
    )(x2, g_mix, w_in, qg, kg, cos, sin, pool_w, pool_scale)


def _dilated_attn_kernel(q_ref, k_ref, kh_ref, v_ref, vh_ref, o_ref, st_ref, *, tl, rb):
    n_sub = tl // ATT_BLOCK
    qi = lax.broadcasted_iota(jnp.int32, (ATT_BLOCK, 2 * ATT_BLOCK), 0)
    kj = lax.broadcasted_iota(jnp.int32, (ATT_BLOCK, 2 * ATT_BLOCK), 1)
    delta = qi + ATT_BLOCK - kj
    in_band = jnp.logical_and(delta >= 0, delta <= ATT_BLOCK)
    bias = jnp.where(in_band, 0.0, NEG_INF).astype(F32)
    n_missing = jnp.where(pl.program_id(2) == 0, ATT_BLOCK, 0)
    bias_first = jnp.where(kj < n_missing, NEG_INF, bias)
    lane = lax.broadcasted_iota(jnp.int32, (ATT_BLOCK, LANES), 1)
    ones_blk = jnp.ones((2 * ATT_BLOCK, HEAD_DIM), BF16)

    for r in range(rb):
        for i in range(n_sub):
            rows = slice(i * ATT_BLOCK, (i + 1) * ATT_BLOCK)
            rows2 = slice((i - 1) * ATT_BLOCK, (i + 1) * ATT_BLOCK)
            st = jnp.zeros((ATT_BLOCK, LANES), F32)
            for h in range(ATT_HEADS):
                c0 = r * KV_W + h * HEAD_DIM
                sl = slice(c0, c0 + HEAD_DIM)
                if i == 0:
                    k_blk = jnp.concatenate([kh_ref[0, :, sl], k_ref[0, rows, sl]], axis=0)
                    v_blk = jnp.concatenate([vh_ref[0, :, sl], v_ref[0, rows, sl]], axis=0)
                    b = bias_first
                else:
                    k_blk, v_blk, b = k_ref[0, rows2, sl], v_ref[0, rows2, sl], bias
                s = lax.dot_general(q_ref[0, rows, sl], k_blk, (((1,), (1,)), ((), ())),
                                    preferred_element_type=F32) + b
                m = jnp.max(s, axis=-1, keepdims=True)
                p = jnp.exp(s - m).astype(BF16)
                acc_l = jnp.dot(p, jnp.concatenate([v_blk, ones_blk], axis=1),
                                preferred_element_type=F32)
                o_ref[0, rows, sl] = acc_l[:, :HEAD_DIM].astype(BF16)
                st = jnp.where(lane == h, m, st)
                st = jnp.where(lane == ATT_HEADS + h, acc_l[:, HEAD_DIM:], st)
            st_ref[0, rows, r * LANES:(r + 1) * LANES] = st


def _dilated_attn(q, k, v, batch, seq, dilation):
    sub_len = seq // dilation
    tl = min(sub_len, 1024)
    rb = min(dilation, 1024 // tl)
    n_sub = tl // ATT_BLOCK
    view = lambda a: a.reshape(batch, sub_len, a.shape[1])
    cur = lambda w: pl.BlockSpec((1, tl, rb * w), lambda b, r, i: (b, i, r))
    halo = pl.BlockSpec((1, ATT_BLOCK, rb * KV_W),
                        lambda b, r, i: (b, jnp.maximum(i * n_sub - 1, 0), r))
    acc, st = pl.pallas_call(
        functools.partial(_dilated_attn_kernel, tl=tl, rb=rb),
        grid=(batch, dilation // rb, sub_len // tl),
        in_specs=[cur(KV_W), cur(KV_W), halo, cur(KV_W), halo],
        out_specs=[cur(KV_W), cur(LANES)],
        out_shape=[jax.ShapeDtypeStruct((batch, sub_len, dilation * KV_W), BF16),
                   jax.ShapeDtypeStruct((batch, sub_len, dilation * LANES), F32)],
        compiler_params=pltpu.CompilerParams(
            dimension_semantics=("parallel", "parallel", "parallel"), vmem_limit_bytes=VMEM_LIMIT),
        name=f"dilated_attn_d{dilation}",
    )(view(q), view(k), view(k), view(v), view(v))
    return (acc.reshape(batch * sub_len, dilation * KV_W), st.reshape(batch * sub_len, dilation * LANES))


def _mem_kv_kernel(mem_ref, g_ref, w_ref, kg_ref, k_ref, v_ref):
    mn = _rms(mem_ref[0], g_ref[...]).astype(BF16)
    kv = jnp.dot(mn, w_ref[...], preferred_element_type=F32)
    for h in range(X_HEADS):
        sl = slice(h * HEAD_DIM, (h + 1) * HEAD_DIM)
        k_ref[0, :, sl] = _rms(kv[:, sl], kg_ref[...]).astype(BF16)
    v_ref[0] = kv[:, X_W:].astype(BF16)


def _mem_kv(mem, g_mem, w_ckv, ckg):
    batch, n_mem, _ = mem.shape
    out = jax.ShapeDtypeStruct((batch, n_mem, X_W), BF16)
    blk = pl.BlockSpec((1, n_mem, X_W), lambda b: (b, 0, 0))
    return pl.pallas_call(
        _mem_kv_kernel,
        grid=(batch,),
        in_specs=[pl.BlockSpec((1, n_mem, D_MODEL), lambda b: (b, 0, 0)), _resident((1, D_MODEL)),
                  _resident((D_MODEL, 2 * X_W)), _resident((1, HEAD_DIM))],
        out_specs=[blk, blk],
        out_shape=[out, out],
        compiler_params=pltpu.CompilerParams(dimension_semantics=("parallel",),
                                             vmem_limit_bytes=VMEM_LIMIT),
        name="mem_kv",
    )(mem, g_mem, w_ckv, ckg)


def _mix_cross_kernel(x_ref, a1_ref, a4_ref, a16_ref, s1_ref, s4_ref, s16_ref, y_ref, wo_ref,
                      gc_ref, wq_ref, qg_ref, km_ref, vm_ref, wc_ref, h_ref,
                      acc_tok, st_tok, cat_a, cat_b, o_ref, *, tm):
    step = pl.program_id(0)

    def merge(cat_ref):
        for gi, (a_ref, s_ref, d) in enumerate(((a4_ref, s4_ref, 4), (a16_ref, s16_ref, 16))):
            n = tm // d
            for r in range(d):
                dst = pl.ds(r, n, stride=d)
                st_tok[gi, dst, :] = s_ref[:, r * LANES:(r + 1) * LANES]
                for h in range(ATT_HEADS):
                    c0 = r * KV_W + h * HEAD_DIM
                    acc_tok[gi, h, dst, :] = a_ref[:, c0:c0 + HEAD_DIM].astype(F32)

        stats = [s1_ref[...], st_tok[0], st_tok[1]]
        m_all = jnp.maximum(jnp.maximum(stats[0], stats[1]), stats[2])
        wts = [jnp.exp(s - m_all) for s in stats]
        for h in range(ATT_HEADS):
            sl = slice(h * HEAD_DIM, (h + 1) * HEAD_DIM)
            accs = (a1_ref[:, sl].astype(F32), acc_tok[0, h], acc_tok[1, h])
            num = jnp.zeros((tm, HEAD_DIM), F32)
            den = jnp.zeros((tm, 1), F32)
            for g in range(len(DILATED_PAIRS)):
                w = wts[g][:, h:h + 1]
                num = num + w * accs[g]
                den = den + w * stats[g][:, ATT_HEADS + h:ATT_HEADS + h + 1]
            cat_ref[:, sl] = (num / den).astype(BF16)
        cat_ref[:, KV_W:] = y_ref[...]

    def mix(cat_ref):
        n_mem = km_ref.shape[1]
        h1 = x_ref[...] + jnp.dot(cat_ref[...], wo_ref[...], preferred_element_type=F32)
        hn = _rms(h1, gc_ref[...]).astype(BF16)
        qc = jnp.dot(hn, wq_ref[...], preferred_element_type=F32)
        for h in range(X_HEADS):
            sl = slice(h * HEAD_DIM, (h + 1) * HEAD_DIM)
            qh = _rms(qc[:, sl], qg_ref[...]).astype(BF16)
            s = lax.dot_general(qh, km_ref[0, :, sl], (((1,), (1,)), ((), ())),
                                preferred_element_type=F32)
            p = jnp.exp(s - jnp.max(s, axis=-1, keepdims=True)).astype(BF16)
            v_ones = jnp.concatenate([vm_ref[0, :, sl], jnp.ones((n_mem, HEAD_DIM), BF16)], axis=1)
            o_l = jnp.dot(p, v_ones, preferred_element_type=F32)
            o_ref[:, sl] = (o_l[:, :HEAD_DIM] / o_l[:, HEAD_DIM:]).astype(BF16)
        h_ref[...] = h1 + jnp.dot(o_ref[...], wc_ref[...], preferred_element_type=F32)

    @pl.when(step == 0)
    def _():
        cat_b[...] = jnp.zeros(cat_b.shape, BF16)

    @pl.when(step % 2 == 0)
    def _():
        merge(cat_a)
        mix(cat_b)

    @pl.when(step % 2 == 1)
    def _():
        merge(cat_b)
        mix(cat_a)


def _mix_cross(x2, accs, stats, y, w_out, g_cross, w_cq, cqg, k_mem, v_mem, w_co, seq):
    n_tok = x2.shape[0]
    tm = 512
    n_tiles = n_tok // tm
    tiles_per_seq = seq // tm
    n_mem = k_mem.shape[1]
    ahead = lambda i: (jnp.minimum(i, n_tiles - 1), 0)
    behind = lambda i: (jnp.maximum(i - 1, 0), 0)
    view = lambda d, w: pl.BlockSpec((tm // d, d * w), ahead)
    mem_blk = pl.BlockSpec((1, n_mem, X_W), lambda i: (jnp.maximum(i - 1, 0) // tiles_per_seq, 0, 0))
    return pl.pallas_call(
        functools.partial(_mix_cross_kernel, tm=tm),
        grid=(n_tiles + 1,),
        in_specs=[pl.BlockSpec((tm, D_MODEL), behind)] + [view(d, KV_W) for d in DILATIONS]
                 + [view(d, LANES) for d in DILATIONS]
                 + [view(1, POOL_W), _resident((KV_W + POOL_W, D_MODEL)), _resident((1, D_MODEL)),
                    _resident((D_MODEL, X_W)), _resident((1, HEAD_DIM)), mem_blk, mem_blk,
                    _resident((X_W, D_MODEL))],
        out_specs=pl.BlockSpec((tm, D_MODEL), behind),
        out_shape=jax.ShapeDtypeStruct((n_tok, D_MODEL), F32),
        scratch_shapes=[pltpu.VMEM((2, ATT_HEADS, tm, HEAD_DIM), F32),
                        pltpu.VMEM((2, tm, LANES), F32),
                        pltpu.VMEM((tm, KV_W + POOL_W), BF16),
                        pltpu.VMEM((tm, KV_W + POOL_W), BF16),
                        pltpu.VMEM((tm, X_W), BF16)],
        compiler_params=pltpu.CompilerParams(dimension_semantics=("arbitrary",),
                                             vmem_limit_bytes=VMEM_LIMIT),
        name="mix_cross",
    )(x2, *accs, *stats, y, w_out, g_cross, w_cq, cqg, k_mem, v_mem, w_co)


def _ffn_kernel(h_ref, g_ref, wgu_ref, wd_ref, o_ref):
    h = h_ref[...]
    hn = _rms(h, g_ref[...]).astype(BF16)
    gate = jnp.dot(hn, wgu_ref[:, :D_FF], preferred_element_type=F32)
    up = jnp.dot(hn, wgu_ref[:, D_FF:], preferred_element_type=F32)
    act = (gate * jax.nn.sigmoid(gate) * up).astype(BF16)
    o_ref[...] = h + jnp.dot(act, wd_ref[...], preferred_element_type=F32)


def _ffn(h2, g_ffn, w_gate_up, w_down):
    n_tok = h2.shape[0]
    tm = 512
    tok = pl.BlockSpec((tm, D_MODEL), lambda i: (i, 0))
    return pl.pallas_call(
        _ffn_kernel,
        grid=(n_tok // tm,),
        in_specs=[tok, _resident((1, D_MODEL)), _resident((D_MODEL, 2 * D_FF)),
                  _resident((D_FF, D_MODEL))],
        out_specs=tok,
        out_shape=jax.ShapeDtypeStruct((n_tok, D_MODEL), F32),
        compiler_params=pltpu.CompilerParams(dimension_semantics=("parallel",),
                                             vmem_limit_bytes=VMEM_LIMIT),
        name="ffn",
    )(h2, g_ffn, w_gate_up, w_down)


def kernel(x, mem, positions, mix_norm_g, w_in, q_norm_g, k_norm_g, pool_w, pool_scale, w_out,
           cross_norm_g, mem_norm_g, w_cq, w_ckv, cq_norm_g, ck_norm_g, w_co,
           ffn_norm_g, w_gate_up, w_down):
    batch, seq, _ = x.shape
    depth = w_in.shape[0]
    scale = HEAD_DIM ** -0.5
    cos, sin = _rope_tables(positions)
    row = lambda a: a.reshape(1, -1)
    order = _rotary_lane_order()
    n_qk_heads = (len(DILATED_PAIRS) + 1) * ATT_HEADS
    n_mid = ROT_PAIR_LANE - ROT_HALF
    h = x.reshape(batch * seq, D_MODEL)
    for layer in range(depth):
        w_bf = w_in[layer].astype(BF16)
        w_qk = w_bf[:, :n_qk_heads * HEAD_DIM].reshape(D_MODEL, n_qk_heads, HEAD_DIM)
        w_qk = jnp.concatenate([w_qk[..., :ROT_HALF], w_qk[..., ROT_DIM:ROT_DIM + n_mid],
                                w_qk[..., ROT_HALF:ROT_DIM], w_qk[..., ROT_DIM + n_mid:]], axis=-1)
        w_in_l = jnp.concatenate([w_qk.reshape(D_MODEL, -1), w_bf[:, n_qk_heads * HEAD_DIM:]], axis=1)
        q1, k1, v1, q4, k4, v4, q16, k16, v16, y = _in_proj(
            h, row(mix_norm_g[layer]), w_in_l, row(q_norm_g[layer][order] * scale),
            row(k_norm_g[layer][order]), cos, sin, pool_w[layer].astype(BF16),
            row(pool_scale[layer]), seq)
        accs, stats = [], []
        for (qv, kv, vv), dilation in zip(((q1, k1, v1), (q4, k4, v4), (q16, k16, v16)), DILATIONS):
            a, s = _dilated_attn(qv, kv, vv, batch, seq, dilation)
            accs.append(a)
            stats.append(s)
        k_mem, v_mem = _mem_kv(mem, row(mem_norm_g[layer]), w_ckv[layer].astype(BF16),
                               row(ck_norm_g[layer]))
        h = _mix_cross(h, accs, stats, y, w_out[layer].astype(BF16), row(cross_norm_g[layer]),
                       w_cq[layer].astype(BF16), row(cq_norm_g[layer] * scale), k_mem, v_mem,
                       w_co[layer].astype(BF16), seq)
        h = _ffn(h, row(ffn_norm_g[layer]), w_gate_up[layer].astype(BF16),
                 w_down[layer].astype(BF16))
    return h.reshape(batch, seq, D_MODEL)
```

```python
import functools

import jax
import jax.numpy as jnp
import numpy as np
from jax import lax
from jax.experimental import pallas as pl
from jax.experimental.pallas import tpu as pltpu

D_MODEL = 1024
HEAD_DIM = 128
ATT_HEADS = 4
DILATED_PAIRS = ((128, 1), (512, 4), (2048, 16))
DILATIONS = tuple(d for _, d in DILATED_PAIRS)
KV_W = ATT_HEADS * HEAD_DIM
POOL_WINDOWS = (2, 4, 8, 16)
POOL_W = len(POOL_WINDOWS) * HEAD_DIM
IN_W = (len(DILATED_PAIRS) + 2) * KV_W + POOL_W
ROT_DIM = HEAD_DIM // 4
ROT_HALF = ROT_DIM // 2
ROPE_THETA = 500000.0
X_HEADS = 4
X_W = X_HEADS * HEAD_DIM
D_FF = 2816
EPS = 1e-6
NEG_INF = -1e30
ATT_BLOCK = 128
LANES = 128
SUBLANES = 8
ROT_PAIR_LANE = LANES // 2
ROW_CHUNK = 128
POOL_HALO = 32

F32 = jnp.float32
BF16 = jnp.bfloat16

VMEM_LIMIT = 56 * 1024 * 1024


def _rms(x, g):
    ms = jnp.mean(x * x, axis=-1, keepdims=True)
    return x * lax.rsqrt(ms + EPS) * g


def _resident(shape):
    nd = len(shape)
    return pl.BlockSpec(shape, lambda *_: (0,) * nd, pipeline_mode=pl.Buffered(1))


def _rotary_lane_order():
    rest = np.arange(ROT_DIM, HEAD_DIM)
    n_mid = ROT_PAIR_LANE - ROT_HALF
    return np.concatenate([np.arange(ROT_HALF), rest[:n_mid],
                           np.arange(ROT_HALF, ROT_DIM), rest[n_mid:]])


def _prep_kernel(pos_ref, invf_ref, win_ref, wout_ref, wcq_ref, wckv_ref, wco_ref,
                 ctab_ref, stab_ref, win_o, wout_o, wcq_o, wckv_o, wco_o, *, n_qk_heads):
    ang = pos_ref[...] * invf_ref[...]
    cos = jnp.cos(ang)
    sin = jnp.sin(ang)
    n_rows = ang.shape[0]
    per_row = LANES // ROT_HALF
    lane = lax.broadcasted_iota(jnp.int32, (n_rows, LANES), 1)
    first = lane < ROT_HALF
    second = jnp.logical_and(lane >= ROT_PAIR_LANE, lane < ROT_PAIR_LANE + ROT_HALF)
    for j in range(per_row):
        shift = (LANES - ROT_HALF * j) % LANES
        cj = pltpu.roll(cos, shift, 1) if shift else cos
        sj = pltpu.roll(sin, shift, 1) if shift else sin
        dst = pl.ds(j, n_rows, stride=per_row)
        ctab_ref[dst, :] = jnp.where(first, cj, jnp.where(second, pltpu.roll(cj, ROT_PAIR_LANE, 1), 1.0))
        stab_ref[dst, :] = jnp.where(first, -sj, jnp.where(second, pltpu.roll(sj, ROT_PAIR_LANE, 1), 0.0))

    n_mid = ROT_PAIR_LANE - ROT_HALF
    lane_w = lax.broadcasted_iota(jnp.int32, (win_ref.shape[0], LANES), 1)
    keep = jnp.logical_or(lane_w < ROT_HALF, lane_w >= ROT_DIM + n_mid)
    from_low = jnp.logical_and(lane_w >= ROT_PAIR_LANE, lane_w < ROT_PAIR_LANE + ROT_HALF)
    for h in range(n_qk_heads):
        sl = slice(h * HEAD_DIM, (h + 1) * HEAD_DIM)
        w = win_ref[:, sl]
        moved = jnp.where(from_low, pltpu.roll(w, n_mid, 1), pltpu.roll(w, LANES - ROT_HALF, 1))
        win_o[:, sl] = jnp.where(keep, w, moved).astype(BF16)
    rest = slice(n_qk_heads * HEAD_DIM, IN_W)
    win_o[:, rest] = win_ref[:, rest].astype(BF16)
    for src, dst in ((wout_ref, wout_o), (wcq_ref, wcq_o), (wckv_ref, wckv_o), (wco_ref, wco_o)):
        dst[...] = src[...].astype(BF16)


def _prep(positions, w_in, w_out, w_cq, w_ckv, w_co):
    n_tok = positions.size
    per_row = LANES // ROT_HALF
    n_steps = 8
    rows = n_tok // per_row // n_steps
    pos = jnp.repeat(positions.reshape(-1).astype(F32), ROT_HALF).reshape(n_tok // per_row, LANES)
    inv_freq = ROPE_THETA ** (-jnp.arange(0, ROT_DIM, 2, dtype=F32) / ROT_DIM)
    invf = jnp.tile(inv_freq, per_row).reshape(1, LANES)
    weights = (w_in, w_out, w_cq, w_ckv, w_co)
    row_blk = lambda w: pl.BlockSpec((w.shape[0] // n_steps, w.shape[1]), lambda i: (i, 0))
    tab = pl.BlockSpec((rows * per_row, LANES), lambda i: (i, 0))
    return pl.pallas_call(
        functools.partial(_prep_kernel, n_qk_heads=(len(DILATED_PAIRS) + 1) * ATT_HEADS),
        grid=(n_steps,),
        in_specs=[pl.BlockSpec((rows, LANES), lambda i: (i, 0)), pl.BlockSpec((1, LANES), lambda i: (0, 0))]
                 + [row_blk(w) for w in weights],
        out_specs=[tab, tab] + [row_blk(w) for w in weights],
        out_shape=[jax.ShapeDtypeStruct((n_tok, LANES), F32)] * 2
                  + [jax.ShapeDtypeStruct(w.shape, BF16) for w in weights],
        compiler_params=pltpu.CompilerParams(dimension_semantics=("parallel",),
                                             vmem_limit_bytes=VMEM_LIMIT),
        name="prep",
    )(pos, invf, *weights)


def _in_proj_kernel(x_ref, g_ref, w_ref, qg_ref, kg_ref, cos_ref, sin_ref, pw_ref, ps_ref,
                    q1_ref, k1_ref, v1_ref, q4_ref, k4_ref, v4_ref, q16_ref, k16_ref, v16_ref, y_ref,
                    xn_ref, tab_ref, pa_ref, pb_ref, hist_ref, lvl_ref, stage_ref, stage4_ref,
                    *, tm, tiles_per_seq):
    step = pl.program_id(0)
    rc = ROW_CHUNK
    n_rc = tm // rc

    def project(p_ref):
        for i in range(n_rc):
            rows = slice(i * rc, (i + 1) * rc)
            xn_ref[rows, :] = _rms(x_ref[rows, :], g_ref[...]).astype(BF16)
        for c in range(IN_W // KV_W):
            cols = slice(c * KV_W, (c + 1) * KV_W)
            p_ref[:, cols] = jnp.dot(xn_ref[...], w_ref[:, cols], preferred_element_type=F32)

    def finish(p_ref):
        for gi, gain_ref in enumerate((qg_ref, kg_ref)):
            gain = gain_ref[...]
            tab_ref[2 * gi] = cos_ref[...] * gain
            tab_ref[2 * gi + 1] = sin_ref[...] * pltpu.roll(gain, ROT_PAIR_LANE, 1)

        def emit(c, gi, tok_ref, v4_ref, v16_ref, a):
            for h in range(ATT_HEADS):
                sl = slice(h * HEAD_DIM, (h + 1) * HEAD_DIM)
                src = slice(c * KV_W + h * HEAD_DIM, c * KV_W + (h + 1) * HEAD_DIM)
                for i in range(n_rc):
                    rows = slice(i * rc, (i + 1) * rc)
                    t = p_ref[rows, src]
                    if gi >= 0:
                        rs = lax.rsqrt(jnp.mean(t * t, axis=-1, keepdims=True) + EPS)
                        t = (t * tab_ref[2 * gi, rows, :]
                             + pltpu.roll(t, ROT_PAIR_LANE, 1) * tab_ref[2 * gi + 1, rows, :]) * rs
                    if tok_ref is not None:
                        tok_ref[rows, sl] = t.astype(BF16)
                    if v4_ref is not None or v16_ref is not None:
                        stage_ref[a, h, rows, :] = t
                if v4_ref is None and v16_ref is None:
                    continue
                n4 = tm // 4
                for b in range(4):
                    t4 = stage_ref[a, h, pl.ds(b, n4, stride=4), :]
                    if v4_ref is not None:
                        c0 = b * KV_W + h * HEAD_DIM
                        v4_ref[:, c0:c0 + HEAD_DIM] = t4.astype(BF16)
                    if v16_ref is not None:
                        stage4_ref[a, h, b] = t4
                if v16_ref is not None:
                    for b in range(4):
                        for a4 in range(4):
                            c0 = (4 * a4 + b) * KV_W + h * HEAD_DIM
                            v16_ref[:, c0:c0 + HEAD_DIM] = (
                                stage4_ref[a, h, b, pl.ds(a4, n4 // 4, stride=4), :].astype(BF16))

        emit(3, 1, k1_ref, k4_ref, k16_ref, 0)
        emit(4, -1, v1_ref, v4_ref, v16_ref, 1)
        emit(2, 0, None, None, q16_ref, 2)
        emit(1, 0, None, q4_ref, None, 3)
        emit(0, 0, q1_ref, None, None, 0)

        seq_tile = jnp.maximum(step - 1, 0) % tiles_per_seq
        end = POOL_HALO + tm
        hist_ref[0:POOL_HALO, :] = jnp.where(seq_tile == 0, 0.0, 1.0) * hist_ref[tm:end, :]
        u_cols = slice(IN_W - POOL_W, IN_W)
        hist_ref[POOL_HALO:end, :] = p_ref[:, u_cols]
        row16 = lax.broadcasted_iota(jnp.int32, (2 * SUBLANES, HEAD_DIM), 0)
        for g, w in enumerate(POOL_WINDOWS):
            sl = slice(g * HEAD_DIM, (g + 1) * HEAD_DIM)
            src, start, shift = hist_ref, SUBLANES, 1
            while True:
                cols = sl if src is hist_ref else slice(None)
                tot = src[start:end, cols] + src[start - shift:end - shift, cols]
                shift *= 2
                if shift == w:
                    break
                lvl_ref[g, start:end, :] = tot
                src, start = lvl_ref.at[g], start + SUBLANES
            tot = tot[POOL_HALO - start:, :]
            n_first = jnp.maximum(jnp.minimum(row16 + 1, w), jnp.where(seq_tile == 0, 0, w))
            inv_first = 1.0 / n_first.astype(F32)
            ug = hist_ref[POOL_HALO:end, sl]
            d = jnp.concatenate([tot[:2 * SUBLANES] * inv_first, tot[2 * SUBLANES:] * (1.0 / w)],
                                axis=0) - ug
            yg = jnp.dot(d.astype(BF16), pw_ref[g], preferred_element_type=F32) * ps_ref[:, sl]
            y_ref[:, sl] = yg.astype(BF16)

    @pl.when(step == 0)
    def _():
        pb_ref[...] = jnp.zeros(pb_ref.shape, F32)
        hist_ref[...] = jnp.zeros(hist_ref.shape, F32)

    @pl.when(step % 2 == 0)
    def _():
        project(pa_ref)
        finish(pb_ref)

    @pl.when(step % 2 == 1)
    def _():
        project(pb_ref)
        finish(pa_ref)


def _in_proj(x2, g_mix, w_in, qg, kg, cos, sin, pool_w, pool_scale, seq):
    n_tok = x2.shape[0]
    tm = 512
    n_tiles = n_tok // tm
    tiles_per_seq = seq // tm
    ahead = lambda i: (jnp.minimum(i, n_tiles - 1), 0)
    behind = lambda i: (jnp.maximum(i - 1, 0), 0)
    view = lambda d: pl.BlockSpec((tm // d, d * KV_W), behind)
    view_sds = lambda d: jax.ShapeDtypeStruct((n_tok // d, d * KV_W), BF16)
    out_d = (1, 1, 1, 4, 4, 4, 16, 16, 16, 1)
    return pl.pallas_call(
        functools.partial(_in_proj_kernel, tm=tm, tiles_per_seq=tiles_per_seq),
        grid=(n_tiles + 1,),
        in_specs=[pl.BlockSpec((tm, D_MODEL), ahead), _resident((1, D_MODEL)),
                  _resident((D_MODEL, IN_W)), _resident((1, HEAD_DIM)), _resident((1, HEAD_DIM)),
                  pl.BlockSpec((tm, LANES), behind), pl.BlockSpec((tm, LANES), behind),
                  _resident((len(POOL_WINDOWS), HEAD_DIM, HEAD_DIM)), _resident((1, POOL_W))],
        out_specs=[view(d) for d in out_d],
        out_shape=[view_sds(d) for d in out_d],
        scratch_shapes=[pltpu.VMEM((tm, D_MODEL), BF16),
                        pltpu.VMEM((4, tm, LANES), F32),
                        pltpu.VMEM((tm, IN_W), F32),
                        pltpu.VMEM((tm, IN_W), F32),
                        pltpu.VMEM((POOL_HALO + tm, POOL_W), F32),
                        pltpu.VMEM((len(POOL_WINDOWS), POOL_HALO + tm, HEAD_DIM), F32),
                        pltpu.VMEM((4, ATT_HEADS, tm, HEAD_DIM), F32),
                        pltpu.VMEM((3, ATT_HEADS, 4, tm // 4, HEAD_DIM), F32)],
        compiler_params=pltpu.CompilerParams(dimension_semantics=("arbitrary",),
                                             vmem_limit_bytes=VMEM_LIMIT),
        name="in_proj",
    )(x2, g_mix, w_in, qg, kg, cos, sin, pool_w, pool_scale)


def _dilated_attn_kernel(q_ref, k_ref, kh_ref, v_ref, vh_ref, o_ref, st_ref, *, tl, rb):
    n_sub = tl // ATT_BLOCK
    qi = lax.broadcasted_iota(jnp.int32, (ATT_BLOCK, 2 * ATT_BLOCK), 0)
    kj = lax.broadcasted_iota(jnp.int32, (ATT_BLOCK, 2 * ATT_BLOCK), 1)
    delta = qi + ATT_BLOCK - kj
    in_band = jnp.logical_and(delta >= 0, delta <= ATT_BLOCK)
    bias = jnp.where(in_band, 0.0, NEG_INF).astype(F32)
    n_missing = jnp.where(pl.program_id(2) == 0, ATT_BLOCK, 0)
    bias_first = jnp.where(kj < n_missing, NEG_INF, bias)
    lane = lax.broadcasted_iota(jnp.int32, (ATT_BLOCK, LANES), 1)
    ones_blk = jnp.ones((2 * ATT_BLOCK, HEAD_DIM), BF16)

    for r in range(rb):
        for i in range(n_sub):
            rows = slice(i * ATT_BLOCK, (i + 1) * ATT_BLOCK)
            rows2 = slice((i - 1) * ATT_BLOCK, (i + 1) * ATT_BLOCK)
            st = jnp.zeros((ATT_BLOCK, LANES), F32)
            for h in range(ATT_HEADS):
                c0 = r * KV_W + h * HEAD_DIM
                sl = slice(c0, c0 + HEAD_DIM)
                if i == 0:
                    k_blk = jnp.concatenate([kh_ref[0, :, sl], k_ref[0, rows, sl]], axis=0)
                    v_blk = jnp.concatenate([vh_ref[0, :, sl], v_ref[0, rows, sl]], axis=0)
                    b = bias_first
                else:
                    k_blk, v_blk, b = k_ref[0, rows2, sl], v_ref[0, rows2, sl], bias
                s = lax.dot_general(q_ref[0, rows, sl], k_blk, (((1,), (1,)), ((), ())),
                                    preferred_element_type=F32) + b
                m = jnp.max(s, axis=-1, keepdims=True)
                p = jnp.exp(s - m).astype(BF16)
                acc_l = jnp.dot(p, jnp.concatenate([v_blk, ones_blk], axis=1),
                                preferred_element_type=F32)
                o_ref[0, rows, sl] = acc_l[:, :HEAD_DIM].astype(BF16)
                st = jnp.where(lane == h, m, st)
                st = jnp.where(lane == ATT_HEADS + h, acc_l[:, HEAD_DIM:], st)
            st_ref[0, rows, r * LANES:(r + 1) * LANES] = st


def _dilated_attn(q, k, v, batch, seq, dilation):
    sub_len = seq // dilation
    tl = min(sub_len, 1024)
    rb = min(dilation, 1024 // tl)
    n_sub = tl // ATT_BLOCK
    view = lambda a: a.reshape(batch, sub_len, a.shape[1])
    cur = lambda w: pl.BlockSpec((1, tl, rb * w), lambda b, r, i: (b, i, r))
    halo = pl.BlockSpec((1, ATT_BLOCK, rb * KV_W),
                        lambda b, r, i: (b, jnp.maximum(i * n_sub - 1, 0), r))
    acc, st = pl.pallas_call(
        functools.partial(_dilated_attn_kernel, tl=tl, rb=rb),
        grid=(batch, dilation // rb, sub_len // tl),
        in_specs=[cur(KV_W), cur(KV_W), halo, cur(KV_W), halo],
        out_specs=[cur(KV_W), cur(LANES)],
        out_shape=[jax.ShapeDtypeStruct((batch, sub_len, dilation * KV_W), BF16),
                   jax.ShapeDtypeStruct((batch, sub_len, dilation * LANES), F32)],
        compiler_params=pltpu.CompilerParams(
            dimension_semantics=("parallel", "parallel", "parallel"), vmem_limit_bytes=VMEM_LIMIT),
        name=f"dilated_attn_d{dilation}",
    )(view(q), view(k), view(k), view(v), view(v))
    return (acc.reshape(batch * sub_len, dilation * KV_W), st.reshape(batch * sub_len, dilation * LANES))


def _mem_kv_kernel(mem_ref, g_ref, w_ref, kg_ref, k_ref, v_ref):
    mn = _rms(mem_ref[0], g_ref[...]).astype(BF16)
    kv = jnp.dot(mn, w_ref[...], preferred_element_type=F32)
    for h in range(X_HEADS):
        sl = slice(h * HEAD_DIM, (h + 1) * HEAD_DIM)
        k_ref[0, :, sl] = _rms(kv[:, sl], kg_ref[...]).astype(BF16)
    v_ref[0] = kv[:, X_W:].astype(BF16)


def _mem_kv(mem, g_mem, w_ckv, ckg):
    batch, n_mem, _ = mem.shape
    out = jax.ShapeDtypeStruct((batch, n_mem, X_W), BF16)
    blk = pl.BlockSpec((1, n_mem, X_W), lambda b: (b, 0, 0))
    return pl.pallas_call(
        _mem_kv_kernel,
        grid=(batch,),
        in_specs=[pl.BlockSpec((1, n_mem, D_MODEL), lambda b: (b, 0, 0)), _resident((1, D_MODEL)),
                  _resident((D_MODEL, 2 * X_W)), _resident((1, HEAD_DIM))],
        out_specs=[blk, blk],
        out_shape=[out, out],
        compiler_params=pltpu.CompilerParams(dimension_semantics=("parallel",),
                                             vmem_limit_bytes=VMEM_LIMIT),
        name="mem_kv",
    )(mem, g_mem, w_ckv, ckg)


def _mix_cross_kernel(x_ref, a1_ref, a4_ref, a16_ref, s1_ref, s4_ref, s16_ref, y_ref, wo_ref,
                      gc_ref, wq_ref, qg_ref, km_ref, vm_ref, wc_ref, wgu_ref, wd_ref,
                      h_ref, wgu_o, wd_o, acc_tok, st_tok, cat_a, cat_b, o_ref, *, tm):
    step = pl.program_id(0)
    wgu_o[...] = wgu_ref[...].astype(BF16)
    wd_o[...] = wd_ref[...].astype(BF16)

    def merge(cat_ref):
        for gi, (a_ref, s_ref, d) in enumerate(((a4_ref, s4_ref, 4), (a16_ref, s16_ref, 16))):
            n = tm // d
            for r in range(d):
                dst = pl.ds(r, n, stride=d)
                st_tok[gi, dst, :] = s_ref[:, r * LANES:(r + 1) * LANES]
                for h in range(ATT_HEADS):
                    c0 = r * KV_W + h * HEAD_DIM
                    acc_tok[gi, h, dst, :] = a_ref[:, c0:c0 + HEAD_DIM].astype(F32)

        stats = [s1_ref[...], st_tok[0], st_tok[1]]
        m_all = jnp.maximum(jnp.maximum(stats[0], stats[1]), stats[2])
        wts = [jnp.exp(s - m_all) for s in stats]
        for h in range(ATT_HEADS):
            sl = slice(h * HEAD_DIM, (h + 1) * HEAD_DIM)
            accs = (a1_ref[:, sl].astype(F32), acc_tok[0, h], acc_tok[1, h])
            num = jnp.zeros((tm, HEAD_DIM), F32)
            den = jnp.zeros((tm, 1), F32)
            for g in range(len(DILATED_PAIRS)):
                w = wts[g][:, h:h + 1]
                num = num + w * accs[g]
                den = den + w * stats[g][:, ATT_HEADS + h:ATT_HEADS + h + 1]
            cat_ref[:, sl] = (num / den).astype(BF16)
        cat_ref[:, KV_W:] = y_ref[...]

    def mix(cat_ref):
        n_mem = km_ref.shape[1]
        h1 = x_ref[...] + jnp.dot(cat_ref[...], wo_ref[...], preferred_element_type=F32)
        hn = _rms(h1, gc_ref[...]).astype(BF16)
        qc = jnp.dot(hn, wq_ref[...], preferred_element_type=F32)
        for h in range(X_HEADS):
            sl = slice(h * HEAD_DIM, (h + 1) * HEAD_DIM)
            qh = _rms(qc[:, sl], qg_ref[...]).astype(BF16)
            s = lax.dot_general(qh, km_ref[0, :, sl], (((1,), (1,)), ((), ())),
                                preferred_element_type=F32)
            p = jnp.exp(s - jnp.max(s, axis=-1, keepdims=True)).astype(BF16)
            v_ones = jnp.concatenate([vm_ref[0, :, sl], jnp.ones((n_mem, HEAD_DIM), BF16)], axis=1)
            o_l = jnp.dot(p, v_ones, preferred_element_type=F32)
            o_ref[:, sl] = (o_l[:, :HEAD_DIM] / o_l[:, HEAD_DIM:]).astype(BF16)
        h_ref[...] = h1 + jnp.dot(o_ref[...], wc_ref[...], preferred_element_type=F32)

    @pl.when(step == 0)
    def _():
        cat_b[...] = jnp.zeros(cat_b.shape, BF16)

    @pl.when(step % 2 == 0)
    def _():
        merge(cat_a)
        mix(cat_b)

    @pl.when(step % 2 == 1)
    def _():
        merge(cat_b)
        mix(cat_a)


def _mix_cross(x2, accs, stats, y, w_out, g_cross, w_cq, cqg, k_mem, v_mem, w_co, w_gate_up, w_down,
               seq):
    n_tok = x2.shape[0]
    tm = 512
    n_tiles = n_tok // tm
    tiles_per_seq = seq // tm
    n_mem = k_mem.shape[1]
    ahead = lambda i: (jnp.minimum(i, n_tiles - 1), 0)
    behind = lambda i: (jnp.maximum(i - 1, 0), 0)
    view = lambda d, w: pl.BlockSpec((tm // d, d * w), ahead)
    mem_blk = pl.BlockSpec((1, n_mem, X_W), lambda i: (jnp.maximum(i - 1, 0) // tiles_per_seq, 0, 0))

    def cast_blk(w, rows):
        n_blk = w.shape[0] // rows
        assert n_blk * rows == w.shape[0] and n_blk <= n_tiles + 1
        return pl.BlockSpec((rows, w.shape[1]), lambda i: (jnp.minimum(i, n_blk - 1), 0))

    cast_specs = [cast_blk(w_gate_up, 32), cast_blk(w_down, 128)]
    return pl.pallas_call(
        functools.partial(_mix_cross_kernel, tm=tm),
        grid=(n_tiles + 1,),
        in_specs=[pl.BlockSpec((tm, D_MODEL), behind)] + [view(d, KV_W) for d in DILATIONS]
                 + [view(d, LANES) for d in DILATIONS]
                 + [view(1, POOL_W), _resident((KV_W + POOL_W, D_MODEL)), _resident((1, D_MODEL)),
                    _resident((D_MODEL, X_W)), _resident((1, HEAD_DIM)), mem_blk, mem_blk,
                    _resident((X_W, D_MODEL))] + cast_specs,
        out_specs=[pl.BlockSpec((tm, D_MODEL), behind)] + cast_specs,
        out_shape=[jax.ShapeDtypeStruct((n_tok, D_MODEL), F32),
                   jax.ShapeDtypeStruct(w_gate_up.shape, BF16),
                   jax.ShapeDtypeStruct(w_down.shape, BF16)],
        scratch_shapes=[pltpu.VMEM((2, ATT_HEADS, tm, HEAD_DIM), F32),
                        pltpu.VMEM((2, tm, LANES), F32),
                        pltpu.VMEM((tm, KV_W + POOL_W), BF16),
                        pltpu.VMEM((tm, KV_W + POOL_W), BF16),
                        pltpu.VMEM((tm, X_W), BF16)],
        compiler_params=pltpu.CompilerParams(dimension_semantics=("arbitrary",),
                                             vmem_limit_bytes=VMEM_LIMIT),
        name="mix_cross",
    )(x2, *accs, *stats, y, w_out, g_cross, w_cq, cqg, k_mem, v_mem, w_co, w_gate_up, w_down)


def _ffn_kernel(h_ref, g_ref, wgu_ref, wd_ref, o_ref):
    h = h_ref[...]
    hn = _rms(h, g_ref[...]).astype(BF16)
    gate = jnp.dot(hn, wgu_ref[:, :D_FF], preferred_element_type=F32)
    up = jnp.dot(hn, wgu_ref[:, D_FF:], preferred_element_type=F32)
    act = (gate * jax.nn.sigmoid(gate) * up).astype(BF16)
    o_ref[...] = h + jnp.dot(act, wd_ref[...], preferred_element_type=F32)


def _ffn(h2, g_ffn, w_gate_up, w_down):
    n_tok = h2.shape[0]
    tm = 512
    tok = pl.BlockSpec((tm, D_MODEL), lambda i: (i, 0))
    return pl.pallas_call(
        _ffn_kernel,
        grid=(n_tok // tm,),
        in_specs=[tok, _resident((1, D_MODEL)), _resident((D_MODEL, 2 * D_FF)),
                  _resident((D_FF, D_MODEL))],
        out_specs=tok,
        out_shape=jax.ShapeDtypeStruct((n_tok, D_MODEL), F32),
        compiler_params=pltpu.CompilerParams(dimension_semantics=("parallel",),
                                             vmem_limit_bytes=VMEM_LIMIT),
        name="ffn",
    )(h2, g_ffn, w_gate_up, w_down)


def kernel(x, mem, positions, mix_norm_g, w_in, q_norm_g, k_norm_g, pool_w, pool_scale, w_out,
           cross_norm_g, mem_norm_g, w_cq, w_ckv, cq_norm_g, ck_norm_g, w_co,
           ffn_norm_g, w_gate_up, w_down):
    batch, seq, _ = x.shape
    depth = w_in.shape[0]
    scale = HEAD_DIM ** -0.5
    row = lambda a: a.reshape(1, -1)
    order = _rotary_lane_order()
    h = x.reshape(batch * seq, D_MODEL)
    for layer in range(depth):
        cos, sin, w_in_l, w_out_l, w_cq_l, w_ckv_l, w_co_l = _prep(
            positions, w_in[layer], w_out[layer], w_cq[layer], w_ckv[layer], w_co[layer])
        q1, k1, v1, q4, k4, v4, q16, k16, v16, y = _in_proj(
            h, row(mix_norm_g[layer]), w_in_l, row(q_norm_g[layer][order] * scale),
            row(k_norm_g[layer][order]), cos, sin, pool_w[layer].astype(BF16),
            row(pool_scale[layer]), seq)
        accs, stats = [], []
        for (qv, kv, vv), dilation in zip(((q1, k1, v1), (q4, k4, v4), (q16, k16, v16)), DILATIONS):
            a, s = _dilated_attn(qv, kv, vv, batch, seq, dilation)
            accs.append(a)
            stats.append(s)
        k_mem, v_mem = _mem_kv(mem, row(mem_norm_g[layer]), w_ckv_l, row(ck_norm_g[layer]))
        h, w_gu_l, w_down_l = _mix_cross(
            h, accs, stats, y, w_out_l, row(cross_norm_g[layer]), w_cq_l,
            row(cq_norm_g[layer] * scale), k_mem, v_mem, w_co_l, w_gate_up[layer], w_down[layer], seq)
        h = _ffn(h, row(ffn_norm_g[layer]), w_gu_l, w_down_l)
    return h.reshape(batch, seq, D_MODEL)
```

```python
import functools

import jax
import jax.numpy as jnp
import numpy as np
from jax import lax
from jax.experimental import pallas as pl
from jax.experimental.pallas import tpu as pltpu

D_MODEL = 1024
HEAD_DIM = 128
ATT_HEADS = 4
DILATED_PAIRS = ((128, 1), (512, 4), (2048, 16))
DILATIONS = tuple(d for _, d in DILATED_PAIRS)
KV_W = ATT_HEADS * HEAD_DIM
POOL_WINDOWS = (2, 4, 8, 16)
POOL_W = len(POOL_WINDOWS) * HEAD_DIM
IN_W = (len(DILATED_PAIRS) + 2) * KV_W + POOL_W
ROT_DIM = HEAD_DIM // 4
ROT_HALF = ROT_DIM // 2
ROPE_THETA = 500000.0
X_HEADS = 4
X_W = X_HEADS * HEAD_DIM
D_FF = 2816
EPS = 1e-6
NEG_INF = -1e30
ATT_BLOCK = 128
LANES = 128
SUBLANES = 8
ROT_PAIR_LANE = LANES // 2
ROW_CHUNK = 256
MERGE_CHUNK = 512
POOL_HALO = 32

F32 = jnp.float32
BF16 = jnp.bfloat16

VMEM_LIMIT = 56 * 1024 * 1024


def _rms(x, g):
    ms = jnp.mean(x * x, axis=-1, keepdims=True)
    return x * lax.rsqrt(ms + EPS) * g


def _resident(shape):
    nd = len(shape)
    return pl.BlockSpec(shape, lambda *_: (0,) * nd, pipeline_mode=pl.Buffered(1))


def _rotary_lane_order():
    rest = np.arange(ROT_DIM, HEAD_DIM)
    n_mid = ROT_PAIR_LANE - ROT_HALF
    return np.concatenate([np.arange(ROT_HALF), rest[:n_mid],
                           np.arange(ROT_HALF, ROT_DIM), rest[n_mid:]])


def _prep_kernel(pos_ref, invf_ref, win_ref, wout_ref, wcq_ref, wckv_ref, wco_ref,
                 ctab_ref, stab_ref, win_o, wout_o, wcq_o, wckv_o, wco_o, *, n_qk_heads):
    ang = pos_ref[...] * invf_ref[...]
    cos = jnp.cos(ang)
    sin = jnp.sin(ang)
    n_rows = ang.shape[0]
    per_row = LANES // ROT_HALF
    lane = lax.broadcasted_iota(jnp.int32, (n_rows, LANES), 1)
    first = lane < ROT_HALF
    second = jnp.logical_and(lane >= ROT_PAIR_LANE, lane < ROT_PAIR_LANE + ROT_HALF)
    for j in range(per_row):
        shift = (LANES - ROT_HALF * j) % LANES
        cj = pltpu.roll(cos, shift, 1) if shift else cos
        sj = pltpu.roll(sin, shift, 1) if shift else sin
        dst = pl.ds(j, n_rows, stride=per_row)
        ctab_ref[dst, :] = jnp.where(first, cj, jnp.where(second, pltpu.roll(cj, ROT_PAIR_LANE, 1), 1.0))
        stab_ref[dst, :] = jnp.where(first, -sj, jnp.where(second, pltpu.roll(sj, ROT_PAIR_LANE, 1), 0.0))

    n_mid = ROT_PAIR_LANE - ROT_HALF
    lane_w = lax.broadcasted_iota(jnp.int32, (win_ref.shape[0], LANES), 1)
    keep = jnp.logical_or(lane_w < ROT_HALF, lane_w >= ROT_DIM + n_mid)
    from_low = jnp.logical_and(lane_w >= ROT_PAIR_LANE, lane_w < ROT_PAIR_LANE + ROT_HALF)
    for h in range(n_qk_heads):
        sl = slice(h * HEAD_DIM, (h + 1) * HEAD_DIM)
        w = win_ref[:, sl]
        moved = jnp.where(from_low, pltpu.roll(w, n_mid, 1), pltpu.roll(w, LANES - ROT_HALF, 1))
        win_o[:, sl] = jnp.where(keep, w, moved).astype(BF16)
    rest = slice(n_qk_heads * HEAD_DIM, IN_W)
    win_o[:, rest] = win_ref[:, rest].astype(BF16)
    for src, dst in ((wout_ref, wout_o), (wcq_ref, wcq_o), (wckv_ref, wckv_o), (wco_ref, wco_o)):
        dst[...] = src[...].astype(BF16)


def _prep(positions, w_in, w_out, w_cq, w_ckv, w_co):
    n_tok = positions.size
    per_row = LANES // ROT_HALF
    n_steps = 8
    rows = n_tok // per_row // n_steps
    pos = jnp.repeat(positions.reshape(-1).astype(F32), ROT_HALF).reshape(n_tok // per_row, LANES)
    inv_freq = ROPE_THETA ** (-jnp.arange(0, ROT_DIM, 2, dtype=F32) / ROT_DIM)
    invf = jnp.tile(inv_freq, per_row).reshape(1, LANES)
    weights = (w_in, w_out, w_cq, w_ckv, w_co)
    row_blk = lambda w: pl.BlockSpec((w.shape[0] // n_steps, w.shape[1]), lambda i: (i, 0))
    tab = pl.BlockSpec((rows * per_row, LANES), lambda i: (i, 0))
    return pl.pallas_call(
        functools.partial(_prep_kernel, n_qk_heads=(len(DILATED_PAIRS) + 1) * ATT_HEADS),
        grid=(n_steps,),
        in_specs=[pl.BlockSpec((rows, LANES), lambda i: (i, 0)), pl.BlockSpec((1, LANES), lambda i: (0, 0))]
                 + [row_blk(w) for w in weights],
        out_specs=[tab, tab] + [row_blk(w) for w in weights],
        out_shape=[jax.ShapeDtypeStruct((n_tok, LANES), F32)] * 2
                  + [jax.ShapeDtypeStruct(w.shape, BF16) for w in weights],
        compiler_params=pltpu.CompilerParams(dimension_semantics=("parallel",),
                                             vmem_limit_bytes=VMEM_LIMIT),
        name="prep",
    )(pos, invf, *weights)


def _in_proj_kernel(x_ref, g_ref, w_ref, qg_ref, kg_ref, cos_ref, sin_ref, pw_ref, ps_ref,
                    q1_ref, k1_ref, v1_ref, q4_ref, k4_ref, v4_ref, q16_ref, k16_ref, v16_ref, y_ref,
                    xn_ref, tab_ref, pa_ref, pb_ref, hist_ref, lvl_ref, stage_ref, stage4_ref,
                    *, tm, tiles_per_seq):
    step = pl.program_id(0)
    rc = ROW_CHUNK
    n_rc = tm // rc

    def project(p_ref):
        for i in range(n_rc):
            rows = slice(i * rc, (i + 1) * rc)
            xn_ref[rows, :] = _rms(x_ref[rows, :], g_ref[...]).astype(BF16)
        for c in range(IN_W // KV_W):
            cols = slice(c * KV_W, (c + 1) * KV_W)
            p_ref[:, cols] = jnp.dot(xn_ref[...], w_ref[:, cols], preferred_element_type=F32)

    def finish(p_ref):
        for gi, gain_ref in enumerate((qg_ref, kg_ref)):
            gain = gain_ref[...]
            tab_ref[2 * gi] = cos_ref[...] * gain
            tab_ref[2 * gi + 1] = sin_ref[...] * pltpu.roll(gain, ROT_PAIR_LANE, 1)

        def emit(c, gi, tok_ref, v4_ref, v16_ref, a):
            for h in range(ATT_HEADS):
                sl = slice(h * HEAD_DIM, (h + 1) * HEAD_DIM)
                src = slice(c * KV_W + h * HEAD_DIM, c * KV_W + (h + 1) * HEAD_DIM)
                for i in range(n_rc):
                    rows = slice(i * rc, (i + 1) * rc)
                    t = p_ref[rows, src]
                    if gi >= 0:
                        rs = lax.rsqrt(jnp.mean(t * t, axis=-1, keepdims=True) + EPS)
                        t = (t * tab_ref[2 * gi, rows, :]
                             + pltpu.roll(t, ROT_PAIR_LANE, 1) * tab_ref[2 * gi + 1, rows, :]) * rs
                    if tok_ref is not None:
                        tok_ref[rows, sl] = t.astype(BF16)
                    if v4_ref is not None or v16_ref is not None:
                        stage_ref[a, h, rows, :] = t
                if v4_ref is None and v16_ref is None:
                    continue
                n4 = tm // 4
                for b in range(4):
                    t4 = stage_ref[a, h, pl.ds(b, n4, stride=4), :]
                    if v4_ref is not None:
                        c0 = b * KV_W + h * HEAD_DIM
                        v4_ref[:, c0:c0 + HEAD_DIM] = t4.astype(BF16)
                    if v16_ref is not None:
                        stage4_ref[a, h, b] = t4
                if v16_ref is not None:
                    for b in range(4):
                        for a4 in range(4):
                            c0 = (4 * a4 + b) * KV_W + h * HEAD_DIM
                            v16_ref[:, c0:c0 + HEAD_DIM] = (
                                stage4_ref[a, h, b, pl.ds(a4, n4 // 4, stride=4), :].astype(BF16))

        emit(3, 1, k1_ref, k4_ref, k16_ref, 0)
        emit(4, -1, v1_ref, v4_ref, v16_ref, 1)
        emit(2, 0, None, None, q16_ref, 2)
        emit(1, 0, None, q4_ref, None, 3)
        emit(0, 0, q1_ref, None, None, 0)

        seq_tile = jnp.maximum(step - 1, 0) % tiles_per_seq
        end = POOL_HALO + tm
        hist_ref[0:POOL_HALO, :] = jnp.where(seq_tile == 0, 0.0, 1.0) * hist_ref[tm:end, :]
        u_cols = slice(IN_W - POOL_W, IN_W)
        hist_ref[POOL_HALO:end, :] = p_ref[:, u_cols]
        row16 = lax.broadcasted_iota(jnp.int32, (2 * SUBLANES, HEAD_DIM), 0)
        for g, w in enumerate(POOL_WINDOWS):
            sl = slice(g * HEAD_DIM, (g + 1) * HEAD_DIM)
            src, start, shift = hist_ref, SUBLANES, 1
            while True:
                cols = sl if src is hist_ref else slice(None)
                tot = src[start:end, cols] + src[start - shift:end - shift, cols]
                shift *= 2
                if shift == w:
                    break
                lvl_ref[g, start:end, :] = tot
                src, start = lvl_ref.at[g], start + SUBLANES
            tot = tot[POOL_HALO - start:, :]
            n_first = jnp.maximum(jnp.minimum(row16 + 1, w), jnp.where(seq_tile == 0, 0, w))
            inv_first = 1.0 / n_first.astype(F32)
            ug = hist_ref[POOL_HALO:end, sl]
            d = jnp.concatenate([tot[:2 * SUBLANES] * inv_first, tot[2 * SUBLANES:] * (1.0 / w)],
                                axis=0) - ug
            yg = jnp.dot(d.astype(BF16), pw_ref[g], preferred_element_type=F32) * ps_ref[:, sl]
            y_ref[:, sl] = yg.astype(BF16)

    @pl.when(step == 0)
    def _():
        pb_ref[...] = jnp.zeros(pb_ref.shape, F32)
        hist_ref[...] = jnp.zeros(hist_ref.shape, F32)

    @pl.when(step % 2 == 0)
    def _():
        project(pa_ref)
        finish(pb_ref)

    @pl.when(step % 2 == 1)
    def _():
        project(pb_ref)
        finish(pa_ref)


def _in_proj(x2, g_mix, w_in, qg, kg, cos, sin, pool_w, pool_scale, seq):
    n_tok = x2.shape[0]
    tm = 512
    n_tiles = n_tok // tm
    tiles_per_seq = seq // tm
    ahead = lambda i: (jnp.minimum(i, n_tiles - 1), 0)
    behind = lambda i: (jnp.maximum(i - 1, 0), 0)
    view = lambda d: pl.BlockSpec((tm // d, d * KV_W), behind)
    view_sds = lambda d: jax.ShapeDtypeStruct((n_tok // d, d * KV_W), BF16)
    out_d = (1, 1, 1, 4, 4, 4, 16, 16, 16, 1)
    return pl.pallas_call(
        functools.partial(_in_proj_kernel, tm=tm, tiles_per_seq=tiles_per_seq),
        grid=(n_tiles + 1,),
        in_specs=[pl.BlockSpec((tm, D_MODEL), ahead), _resident((1, D_MODEL)),
                  _resident((D_MODEL, IN_W)), _resident((1, HEAD_DIM)), _resident((1, HEAD_DIM)),
                  pl.BlockSpec((tm, LANES), behind), pl.BlockSpec((tm, LANES), behind),
                  _resident((len(POOL_WINDOWS), HEAD_DIM, HEAD_DIM)), _resident((1, POOL_W))],
        out_specs=[view(d) for d in out_d],
        out_shape=[view_sds(d) for d in out_d],
        scratch_shapes=[pltpu.VMEM((tm, D_MODEL), BF16),
                        pltpu.VMEM((4, tm, LANES), F32),
                        pltpu.VMEM((tm, IN_W), F32),
                        pltpu.VMEM((tm, IN_W), F32),
                        pltpu.VMEM((POOL_HALO + tm, POOL_W), F32),
                        pltpu.VMEM((len(POOL_WINDOWS), POOL_HALO + tm, HEAD_DIM), F32),
                        pltpu.VMEM((4, ATT_HEADS, tm, HEAD_DIM), F32),
                        pltpu.VMEM((3, ATT_HEADS, 4, tm // 4, HEAD_DIM), F32)],
        compiler_params=pltpu.CompilerParams(dimension_semantics=("arbitrary",),
                                             vmem_limit_bytes=VMEM_LIMIT),
        name="in_proj",
    )(x2, g_mix, w_in, qg, kg, cos, sin, pool_w, pool_scale)


def _dilated_attn_kernel(q_ref, k_ref, kh_ref, v_ref, vh_ref, o_ref, st_ref, *, tl, rb):
    n_sub = tl // ATT_BLOCK
    qi = lax.broadcasted_iota(jnp.int32, (ATT_BLOCK, 2 * ATT_BLOCK), 0)
    kj = lax.broadcasted_iota(jnp.int32, (ATT_BLOCK, 2 * ATT_BLOCK), 1)
    delta = qi + ATT_BLOCK - kj
    in_band = jnp.logical_and(delta >= 0, delta <= ATT_BLOCK)
    bias = jnp.where(in_band, 0.0, NEG_INF).astype(F32)
    n_missing = jnp.where(pl.program_id(2) == 0, ATT_BLOCK, 0)
    bias_first = jnp.where(kj < n_missing, NEG_INF, bias)
    lane = lax.broadcasted_iota(jnp.int32, (ATT_BLOCK, LANES), 1)
    ones_blk = jnp.ones((2 * ATT_BLOCK, HEAD_DIM), BF16)

    for r in range(rb):
        for i in range(n_sub):
            rows = slice(i * ATT_BLOCK, (i + 1) * ATT_BLOCK)
            rows2 = slice((i - 1) * ATT_BLOCK, (i + 1) * ATT_BLOCK)
            st = jnp.zeros((ATT_BLOCK, LANES), F32)
            for h in range(ATT_HEADS):
                c0 = r * KV_W + h * HEAD_DIM
                sl = slice(c0, c0 + HEAD_DIM)
                if i == 0:
                    k_blk = jnp.concatenate([kh_ref[0, :, sl], k_ref[0, rows, sl]], axis=0)
                    v_blk = jnp.concatenate([vh_ref[0, :, sl], v_ref[0, rows, sl]], axis=0)
                    b = bias_first
                else:
                    k_blk, v_blk, b = k_ref[0, rows2, sl], v_ref[0, rows2, sl], bias
                s = lax.dot_general(q_ref[0, rows, sl], k_blk, (((1,), (1,)), ((), ())),
                                    preferred_element_type=F32) + b
                m = jnp.max(s, axis=-1, keepdims=True)
                p = jnp.exp(s - m).astype(BF16)
                acc_l = jnp.dot(p, jnp.concatenate([v_blk, ones_blk], axis=1),
                                preferred_element_type=F32)
                o_ref[0, rows, sl] = acc_l[:, :HEAD_DIM].astype(BF16)
                st = jnp.where(lane == h, m, st)
                st = jnp.where(lane == ATT_HEADS + h, acc_l[:, HEAD_DIM:], st)
            st_ref[0, rows, r * LANES:(r + 1) * LANES] = st


def _dilated_attn(q, k, v, batch, seq, dilation):
    sub_len = seq // dilation
    tl = min(sub_len, 1024)
    rb = min(dilation, 1024 // tl)
    n_sub = tl // ATT_BLOCK
    view = lambda a: a.reshape(batch, sub_len, a.shape[1])
    cur = lambda w: pl.BlockSpec((1, tl, rb * w), lambda b, r, i: (b, i, r))
    halo = pl.BlockSpec((1, ATT_BLOCK, rb * KV_W),
                        lambda b, r, i: (b, jnp.maximum(i * n_sub - 1, 0), r))
    acc, st = pl.pallas_call(
        functools.partial(_dilated_attn_kernel, tl=tl, rb=rb),
        grid=(batch, dilation // rb, sub_len // tl),
        in_specs=[cur(KV_W), cur(KV_W), halo, cur(KV_W), halo],
        out_specs=[cur(KV_W), cur(LANES)],
        out_shape=[jax.ShapeDtypeStruct((batch, sub_len, dilation * KV_W), BF16),
                   jax.ShapeDtypeStruct((batch, sub_len, dilation * LANES), F32)],
        compiler_params=pltpu.CompilerParams(
            dimension_semantics=("parallel", "parallel", "parallel"), vmem_limit_bytes=VMEM_LIMIT),
        name=f"dilated_attn_d{dilation}",
    )(view(q), view(k), view(k), view(v), view(v))
    return (acc.reshape(batch * sub_len, dilation * KV_W), st.reshape(batch * sub_len, dilation * LANES))


def _mem_kv_kernel(mem_ref, g_ref, w_ref, kg_ref, k_ref, v_ref):
    mn = _rms(mem_ref[0], g_ref[...]).astype(BF16)
    kv = jnp.dot(mn, w_ref[...], preferred_element_type=F32)
    for h in range(X_HEADS):
        sl = slice(h * HEAD_DIM, (h + 1) * HEAD_DIM)
        k_ref[0, :, sl] = _rms(kv[:, sl], kg_ref[...]).astype(BF16)
    v_ref[0] = kv[:, X_W:].astype(BF16)


def _mem_kv(mem, g_mem, w_ckv, ckg):
    batch, n_mem, _ = mem.shape
    out = jax.ShapeDtypeStruct((batch, n_mem, X_W), BF16)
    blk = pl.BlockSpec((1, n_mem, X_W), lambda b: (b, 0, 0))
    return pl.pallas_call(
        _mem_kv_kernel,
        grid=(batch,),
        in_specs=[pl.BlockSpec((1, n_mem, D_MODEL), lambda b: (b, 0, 0)), _resident((1, D_MODEL)),
                  _resident((D_MODEL, 2 * X_W)), _resident((1, HEAD_DIM))],
        out_specs=[blk, blk],
        out_shape=[out, out],
        compiler_params=pltpu.CompilerParams(dimension_semantics=("parallel",),
                                             vmem_limit_bytes=VMEM_LIMIT),
        name="mem_kv",
    )(mem, g_mem, w_ckv, ckg)


def _mix_cross_kernel(x_ref, a1_ref, a4_ref, a16_ref, s1_ref, s4_ref, s16_ref, y_ref, wo_ref,
                      gc_ref, wq_ref, qg_ref, km_ref, vm_ref, wc_ref, wgu_ref, wd_ref,
                      h_ref, wgu_o, wd_o, acc_tok, st_tok, cat_a, cat_b, o_ref, *, tm):
    step = pl.program_id(0)
    wgu_o[...] = wgu_ref[...].astype(BF16)
    wd_o[...] = wd_ref[...].astype(BF16)

    def merge(cat_ref):
        for gi, (a_ref, s_ref, d) in enumerate(((a4_ref, s4_ref, 4), (a16_ref, s16_ref, 16))):
            n = tm // d
            for r in range(d):
                dst = pl.ds(r, n, stride=d)
                st_tok[gi, dst, :] = s_ref[:, r * LANES:(r + 1) * LANES]
                for h in range(ATT_HEADS):
                    c0 = r * KV_W + h * HEAD_DIM
                    acc_tok[gi, h, dst, :] = a_ref[:, c0:c0 + HEAD_DIM].astype(F32)

        for i in range(tm // MERGE_CHUNK):
            rows = slice(i * MERGE_CHUNK, (i + 1) * MERGE_CHUNK)
            stats = [s1_ref[rows, :], st_tok[0, rows, :], st_tok[1, rows, :]]
            m_all = jnp.maximum(jnp.maximum(stats[0], stats[1]), stats[2])
            wts = [jnp.exp(s - m_all) for s in stats]
            for h in range(ATT_HEADS):
                sl = slice(h * HEAD_DIM, (h + 1) * HEAD_DIM)
                accs = (a1_ref[rows, sl].astype(F32), acc_tok[0, h, rows, :], acc_tok[1, h, rows, :])
                num = jnp.zeros((MERGE_CHUNK, HEAD_DIM), F32)
                den = jnp.zeros((MERGE_CHUNK, 1), F32)
                for g in range(len(DILATED_PAIRS)):
                    w = wts[g][:, h:h + 1]
                    num = num + w * accs[g]
                    den = den + w * stats[g][:, ATT_HEADS + h:ATT_HEADS + h + 1]
                cat_ref[rows, sl] = (num / den).astype(BF16)
        cat_ref[:, KV_W:] = y_ref[...]

    def mix(cat_ref):
        n_mem = km_ref.shape[1]
        h1 = x_ref[...] + jnp.dot(cat_ref[...], wo_ref[...], preferred_element_type=F32)
        hn = _rms(h1, gc_ref[...]).astype(BF16)
        qc = jnp.dot(hn, wq_ref[...], preferred_element_type=F32)
        for h in range(X_HEADS):
            sl = slice(h * HEAD_DIM, (h + 1) * HEAD_DIM)
            qh = _rms(qc[:, sl], qg_ref[...]).astype(BF16)
            s = lax.dot_general(qh, km_ref[0, :, sl], (((1,), (1,)), ((), ())),
                                preferred_element_type=F32)
            p = jnp.exp(s - jnp.max(s, axis=-1, keepdims=True)).astype(BF16)
            v_ones = jnp.concatenate([vm_ref[0, :, sl], jnp.ones((n_mem, HEAD_DIM), BF16)], axis=1)
            o_l = jnp.dot(p, v_ones, preferred_element_type=F32)
            o_ref[:, sl] = (o_l[:, :HEAD_DIM] / o_l[:, HEAD_DIM:]).astype(BF16)
        h_ref[...] = h1 + jnp.dot(o_ref[...], wc_ref[...], preferred_element_type=F32)

    @pl.when(step == 0)
    def _():
        cat_b[...] = jnp.zeros(cat_b.shape, BF16)

    @pl.when(step % 2 == 0)
    def _():
        merge(cat_a)
        mix(cat_b)

    @pl.when(step % 2 == 1)
    def _():
        merge(cat_b)
        mix(cat_a)


def _mix_cross(x2, accs, stats, y, w_out, g_cross, w_cq, cqg, k_mem, v_mem, w_co, w_gate_up, w_down,
               seq):
    n_tok = x2.shape[0]
    tm = 512
    n_tiles = n_tok // tm
    tiles_per_seq = seq // tm
    n_mem = k_mem.shape[1]
    ahead = lambda i: (jnp.minimum(i, n_tiles - 1), 0)
    behind = lambda i: (jnp.maximum(i - 1, 0), 0)
    view = lambda d, w: pl.BlockSpec((tm // d, d * w), ahead)
    mem_blk = pl.BlockSpec((1, n_mem, X_W), lambda i: (jnp.maximum(i - 1, 0) // tiles_per_seq, 0, 0))

    def cast_blk(w, rows):
        n_blk = w.shape[0] // rows
        assert n_blk * rows == w.shape[0] and n_blk <= n_tiles + 1
        return pl.BlockSpec((rows, w.shape[1]), lambda i: (jnp.minimum(i, n_blk - 1), 0))

    cast_specs = [cast_blk(w_gate_up, 32), cast_blk(w_down, 128)]
    return pl.pallas_call(
        functools.partial(_mix_cross_kernel, tm=tm),
        grid=(n_tiles + 1,),
        in_specs=[pl.BlockSpec((tm, D_MODEL), behind)] + [view(d, KV_W) for d in DILATIONS]
                 + [view(d, LANES) for d in DILATIONS]
                 + [view(1, POOL_W), _resident((KV_W + POOL_W, D_MODEL)), _resident((1, D_MODEL)),
                    _resident((D_MODEL, X_W)), _resident((1, HEAD_DIM)), mem_blk, mem_blk,
                    _resident((X_W, D_MODEL))] + cast_specs,
        out_specs=[pl.BlockSpec((tm, D_MODEL), behind)] + cast_specs,
        out_shape=[jax.ShapeDtypeStruct((n_tok, D_MODEL), F32),
                   jax.ShapeDtypeStruct(w_gate_up.shape, BF16),
                   jax.ShapeDtypeStruct(w_down.shape, BF16)],
        scratch_shapes=[pltpu.VMEM((2, ATT_HEADS, tm, HEAD_DIM), F32),
                        pltpu.VMEM((2, tm, LANES), F32),
                        pltpu.VMEM((tm, KV_W + POOL_W), BF16),
                        pltpu.VMEM((tm, KV_W + POOL_W), BF16),
                        pltpu.VMEM((tm, X_W), BF16)],
        compiler_params=pltpu.CompilerParams(dimension_semantics=("arbitrary",),
                                             vmem_limit_bytes=VMEM_LIMIT),
        name="mix_cross",
    )(x2, *accs, *stats, y, w_out, g_cross, w_cq, cqg, k_mem, v_mem, w_co, w_gate_up, w_down)


def _ffn_kernel(h_ref, g_ref, wgu_ref, wd_ref, o_ref):
    h = h_ref[...]
    hn = _rms(h, g_ref[...]).astype(BF16)
    gate = jnp.dot(hn, wgu_ref[:, :D_FF], preferred_element_type=F32)
    up = jnp.dot(hn, wgu_ref[:, D_FF:], preferred_element_type=F32)
    act = (gate * jax.nn.sigmoid(gate) * up).astype(BF16)
    o_ref[...] = h + jnp.dot(act, wd_ref[...], preferred_element_type=F32)


def _ffn(h2, g_ffn, w_gate_up, w_down):
    n_tok = h2.shape[0]
    tm = 512
    tok = pl.BlockSpec((tm, D_MODEL), lambda i: (i, 0))
    return pl.pallas_call(
        _ffn_kernel,
        grid=(n_tok // tm,),
        in_specs=[tok, _resident((1, D_MODEL)), _resident((D_MODEL, 2 * D_FF)),
                  _resident((D_FF, D_MODEL))],
        out_specs=tok,
        out_shape=jax.ShapeDtypeStruct((n_tok, D_MODEL), F32),
        compiler_params=pltpu.CompilerParams(dimension_semantics=("parallel",),
                                             vmem_limit_bytes=VMEM_LIMIT),
        name="ffn",
    )(h2, g_ffn, w_gate_up, w_down)


def kernel(x, mem, positions, mix_norm_g, w_in, q_norm_g, k_norm_g, pool_w, pool_scale, w_out,
           cross_norm_g, mem_norm_g, w_cq, w_ckv, cq_norm_g, ck_norm_g, w_co,
           ffn_norm_g, w_gate_up, w_down):
    batch, seq, _ = x.shape
    depth = w_in.shape[0]
    scale = HEAD_DIM ** -0.5
    row = lambda a: a.reshape(1, -1)
    order = _rotary_lane_order()
    h = x.reshape(batch * seq, D_MODEL)
    for layer in range(depth):
        cos, sin, w_in_l, w_out_l, w_cq_l, w_ckv_l, w_co_l = _prep(
            positions, w_in[layer], w_out[layer], w_cq[layer], w_ckv[layer], w_co[layer])
        q1, k1, v1, q4, k4, v4, q16, k16, v16, y = _in_proj(
            h, row(mix_norm_g[layer]), w_in_l, row(q_norm_g[layer][order] * scale),
            row(k_norm_g[layer][order]), cos, sin, pool_w[layer].astype(BF16),
            row(pool_scale[layer]), seq)
        accs, stats = [], []
        for (qv, kv, vv), dilation in zip(((q1, k1, v1), (q4, k4, v4), (q16, k16, v16)), DILATIONS):
            a, s = _dilated_attn(qv, kv, vv, batch, seq, dilation)
            accs.append(a)
            stats.append(s)
        k_mem, v_mem = _mem_kv(mem, row(mem_norm_g[layer]), w_ckv_l, row(ck_norm_g[layer]))
        h, w_gu_l, w_down_l = _mix_cross(
            h, accs, stats, y, w_out_l, row(cross_norm_g[layer]), w_cq_l,
            row(cq_norm_g[layer] * scale), k_mem, v_mem, w_co_l, w_gate_up[layer], w_down[layer], seq)
        h = _ffn(h, row(ffn_norm_g[layer]), w_gu_l, w_down_l)
    return h.reshape(batch, seq, D_MODEL)
```

```python
import functools

import jax
import jax.numpy as jnp
import numpy as np
from jax import lax
from jax.experimental import pallas as pl
from jax.experimental.pallas import tpu as pltpu

D_MODEL = 1024
HEAD_DIM = 128
ATT_HEADS = 4
DILATED_PAIRS = ((128, 1), (512, 4), (2048, 16))
DILATIONS = tuple(d for _, d in DILATED_PAIRS)
KV_W = ATT_HEADS * HEAD_DIM
POOL_WINDOWS = (2, 4, 8, 16)
POOL_W = len(POOL_WINDOWS) * HEAD_DIM
IN_W = (len(DILATED_PAIRS) + 2) * KV_W + POOL_W
ROT_DIM = HEAD_DIM // 4
ROT_HALF = ROT_DIM // 2
ROPE_THETA = 500000.0
X_HEADS = 4
X_W = X_HEADS * HEAD_DIM
D_FF = 2816
EPS = 1e-6
NEG_INF = -1e30
ATT_BLOCK = 128
LANES = 128
SUBLANES = 8
ROT_PAIR_LANE = LANES // 2
ROW_CHUNK = 256
MERGE_CHUNK = 512
POOL_HALO = 32

F32 = jnp.float32
BF16 = jnp.bfloat16

VMEM_LIMIT = 56 * 1024 * 1024


def _rms(x, g):
    ms = jnp.mean(x * x, axis=-1, keepdims=True)
    return x * lax.rsqrt(ms + EPS) * g


def _resident(shape):
    nd = len(shape)
    return pl.BlockSpec(shape, lambda *_: (0,) * nd, pipeline_mode=pl.Buffered(1))


def _rotary_lane_order():
    rest = np.arange(ROT_DIM, HEAD_DIM)
    n_mid = ROT_PAIR_LANE - ROT_HALF
    return np.concatenate([np.arange(ROT_HALF), rest[:n_mid],
                           np.arange(ROT_HALF, ROT_DIM), rest[n_mid:]])


def _prep_kernel(pos_ref, invf_ref, win_ref, wout_ref, wcq_ref, wckv_ref, wco_ref,
                 ctab_ref, stab_ref, win_o, wout_o, wcq_o, wckv_o, wco_o, *, n_qk_heads):
    ang = pos_ref[...] * invf_ref[...]
    cos = jnp.cos(ang)
    sin = jnp.sin(ang)
    n_rows = ang.shape[0]
    per_row = LANES // ROT_HALF
    lane = lax.broadcasted_iota(jnp.int32, (n_rows, LANES), 1)
    first = lane < ROT_HALF
    second = jnp.logical_and(lane >= ROT_PAIR_LANE, lane < ROT_PAIR_LANE + ROT_HALF)
    for j in range(per_row):
        shift = (LANES - ROT_HALF * j) % LANES
        cj = pltpu.roll(cos, shift, 1) if shift else cos
        sj = pltpu.roll(sin, shift, 1) if shift else sin
        dst = pl.ds(j, n_rows, stride=per_row)
        ctab_ref[dst, :] = jnp.where(first, cj, jnp.where(second, pltpu.roll(cj, ROT_PAIR_LANE, 1), 1.0))
        stab_ref[dst, :] = jnp.where(first, -sj, jnp.where(second, pltpu.roll(sj, ROT_PAIR_LANE, 1), 0.0))

    n_mid = ROT_PAIR_LANE - ROT_HALF
    lane_w = lax.broadcasted_iota(jnp.int32, (win_ref.shape[0], LANES), 1)
    keep = jnp.logical_or(lane_w < ROT_HALF, lane_w >= ROT_DIM + n_mid)
    from_low = jnp.logical_and(lane_w >= ROT_PAIR_LANE, lane_w < ROT_PAIR_LANE + ROT_HALF)
    for h in range(n_qk_heads):
        sl = slice(h * HEAD_DIM, (h + 1) * HEAD_DIM)
        w = win_ref[:, sl]
        moved = jnp.where(from_low, pltpu.roll(w, n_mid, 1), pltpu.roll(w, LANES - ROT_HALF, 1))
        win_o[:, sl] = jnp.where(keep, w, moved).astype(BF16)
    rest = slice(n_qk_heads * HEAD_DIM, IN_W)
    win_o[:, rest] = win_ref[:, rest].astype(BF16)
    for src, dst in ((wout_ref, wout_o), (wcq_ref, wcq_o), (wckv_ref, wckv_o), (wco_ref, wco_o)):
        dst[...] = src[...].astype(BF16)


def _prep(positions, w_in, w_out, w_cq, w_ckv, w_co):
    n_tok = positions.size
    per_row = LANES // ROT_HALF
    n_steps = 8
    rows = n_tok // per_row // n_steps
    pos = jnp.repeat(positions.reshape(-1).astype(F32), ROT_HALF).reshape(n_tok // per_row, LANES)
    inv_freq = ROPE_THETA ** (-jnp.arange(0, ROT_DIM, 2, dtype=F32) / ROT_DIM)
    invf = jnp.tile(inv_freq, per_row).reshape(1, LANES)
    weights = (w_in, w_out, w_cq, w_ckv, w_co)
    row_blk = lambda w: pl.BlockSpec((w.shape[0] // n_steps, w.shape[1]), lambda i: (i, 0))
    tab = pl.BlockSpec((rows * per_row, LANES), lambda i: (i, 0))
    return pl.pallas_call(
        functools.partial(_prep_kernel, n_qk_heads=(len(DILATED_PAIRS) + 1) * ATT_HEADS),
        grid=(n_steps,),
        in_specs=[pl.BlockSpec((rows, LANES), lambda i: (i, 0)), pl.BlockSpec((1, LANES), lambda i: (0, 0))]
                 + [row_blk(w) for w in weights],
        out_specs=[tab, tab] + [row_blk(w) for w in weights],
        out_shape=[jax.ShapeDtypeStruct((n_tok, LANES), F32)] * 2
                  + [jax.ShapeDtypeStruct(w.shape, BF16) for w in weights],
        compiler_params=pltpu.CompilerParams(dimension_semantics=("parallel",),
                                             vmem_limit_bytes=VMEM_LIMIT),
        name="prep",
    )(pos, invf, *weights)


def _in_proj_kernel(x_ref, g_ref, w_ref, qg_ref, kg_ref, cos_ref, sin_ref, pw_ref, ps_ref,
                    q1_ref, k1_ref, v1_ref, q4_ref, k4_ref, v4_ref, q16_ref, k16_ref, v16_ref, y_ref,
                    xn_ref, tab_ref, pa_ref, pb_ref, hist_ref, lvl_ref, stage_ref, stage4_ref,
                    *, tm, tiles_per_seq):
    step = pl.program_id(0)
    rc = ROW_CHUNK
    n_rc = tm // rc

    def project(p_ref):
        for i in range(n_rc):
            rows = slice(i * rc, (i + 1) * rc)
            xn_ref[rows, :] = _rms(x_ref[rows, :], g_ref[...]).astype(BF16)
        for c in range(IN_W // KV_W):
            cols = slice(c * KV_W, (c + 1) * KV_W)
            p_ref[:, cols] = jnp.dot(xn_ref[...], w_ref[:, cols], preferred_element_type=F32)

    def finish(p_ref):
        for gi, gain_ref in enumerate((qg_ref, kg_ref)):
            gain = gain_ref[...]
            tab_ref[2 * gi] = cos_ref[...] * gain
            tab_ref[2 * gi + 1] = sin_ref[...] * pltpu.roll(gain, ROT_PAIR_LANE, 1)

        def emit(c, gi, tok_ref, v4_ref, v16_ref, a):
            for h in range(ATT_HEADS):
                sl = slice(h * HEAD_DIM, (h + 1) * HEAD_DIM)
                src = slice(c * KV_W + h * HEAD_DIM, c * KV_W + (h + 1) * HEAD_DIM)
                for i in range(n_rc):
                    rows = slice(i * rc, (i + 1) * rc)
                    t = p_ref[rows, src]
                    if gi >= 0:
                        rs = lax.rsqrt(jnp.mean(t * t, axis=-1, keepdims=True) + EPS)
                        t = (t * tab_ref[2 * gi, rows, :]
                             + pltpu.roll(t, ROT_PAIR_LANE, 1) * tab_ref[2 * gi + 1, rows, :]) * rs
                    if tok_ref is not None:
                        tok_ref[rows, sl] = t.astype(BF16)
                    if v4_ref is not None or v16_ref is not None:
                        stage_ref[a, h, rows, :] = t
                if v4_ref is None and v16_ref is None:
                    continue
                n4 = tm // 4
                for b in range(4):
                    t4 = stage_ref[a, h, pl.ds(b, n4, stride=4), :]
                    if v4_ref is not None:
                        c0 = b * KV_W + h * HEAD_DIM
                        v4_ref[:, c0:c0 + HEAD_DIM] = t4.astype(BF16)
                    if v16_ref is not None:
                        stage4_ref[a, h, b] = t4
                if v16_ref is not None:
                    for b in range(4):
                        for a4 in range(4):
                            c0 = (4 * a4 + b) * KV_W + h * HEAD_DIM
                            v16_ref[:, c0:c0 + HEAD_DIM] = (
                                stage4_ref[a, h, b, pl.ds(a4, n4 // 4, stride=4), :].astype(BF16))

        emit(3, 1, k1_ref, k4_ref, k16_ref, 0)
        emit(4, -1, v1_ref, v4_ref, v16_ref, 1)
        emit(2, 0, None, None, q16_ref, 2)
        emit(1, 0, None, q4_ref, None, 3)
        emit(0, 0, q1_ref, None, None, 0)

        seq_tile = jnp.maximum(step - 1, 0) % tiles_per_seq
        end = POOL_HALO + tm
        hist_ref[0:POOL_HALO, :] = jnp.where(seq_tile == 0, 0.0, 1.0) * hist_ref[tm:end, :]
        u_cols = slice(IN_W - POOL_W, IN_W)
        hist_ref[POOL_HALO:end, :] = p_ref[:, u_cols]
        row16 = lax.broadcasted_iota(jnp.int32, (2 * SUBLANES, HEAD_DIM), 0)
        for g, w in enumerate(POOL_WINDOWS):
            sl = slice(g * HEAD_DIM, (g + 1) * HEAD_DIM)
            src, start, shift = hist_ref, SUBLANES, 1
            while True:
                cols = sl if src is hist_ref else slice(None)
                tot = src[start:end, cols] + src[start - shift:end - shift, cols]
                shift *= 2
                if shift == w:
                    break
                lvl_ref[g, start:end, :] = tot
                src, start = lvl_ref.at[g], start + SUBLANES
            tot = tot[POOL_HALO - start:, :]
            n_first = jnp.maximum(jnp.minimum(row16 + 1, w), jnp.where(seq_tile == 0, 0, w))
            inv_first = 1.0 / n_first.astype(F32)
            ug = hist_ref[POOL_HALO:end, sl]
            d = jnp.concatenate([tot[:2 * SUBLANES] * inv_first, tot[2 * SUBLANES:] * (1.0 / w)],
                                axis=0) - ug
            yg = jnp.dot(d.astype(BF16), pw_ref[g], preferred_element_type=F32) * ps_ref[:, sl]
            y_ref[:, sl] = yg.astype(BF16)

    @pl.when(step == 0)
    def _():
        pb_ref[...] = jnp.zeros(pb_ref.shape, F32)
        hist_ref[...] = jnp.zeros(hist_ref.shape, F32)

    @pl.when(step % 2 == 0)
    def _():
        project(pa_ref)
        finish(pb_ref)

    @pl.when(step % 2 == 1)
    def _():
        project(pb_ref)
        finish(pa_ref)


def _in_proj(x2, g_mix, w_in, qg, kg, cos, sin, pool_w, pool_scale, seq):
    n_tok = x2.shape[0]
    tm = 512
    n_tiles = n_tok // tm
    tiles_per_seq = seq // tm
    ahead = lambda i: (jnp.minimum(i, n_tiles - 1), 0)
    behind = lambda i: (jnp.maximum(i - 1, 0), 0)
    view = lambda d: pl.BlockSpec((tm // d, d * KV_W), behind)
    view_sds = lambda d: jax.ShapeDtypeStruct((n_tok // d, d * KV_W), BF16)
    out_d = (1, 1, 1, 4, 4, 4, 16, 16, 16, 1)
    return pl.pallas_call(
        functools.partial(_in_proj_kernel, tm=tm, tiles_per_seq=tiles_per_seq),
        grid=(n_tiles + 1,),
        in_specs=[pl.BlockSpec((tm, D_MODEL), ahead), _resident((1, D_MODEL)),
                  _resident((D_MODEL, IN_W)), _resident((1, HEAD_DIM)), _resident((1, HEAD_DIM)),
                  pl.BlockSpec((tm, LANES), behind), pl.BlockSpec((tm, LANES), behind),
                  _resident((len(POOL_WINDOWS), HEAD_DIM, HEAD_DIM)), _resident((1, POOL_W))],
        out_specs=[view(d) for d in out_d],
        out_shape=[view_sds(d) for d in out_d],
        scratch_shapes=[pltpu.VMEM((tm, D_MODEL), BF16),
                        pltpu.VMEM((4, tm, LANES), F32),
                        pltpu.VMEM((tm, IN_W), F32),
                        pltpu.VMEM((tm, IN_W), F32),
                        pltpu.VMEM((POOL_HALO + tm, POOL_W), F32),
                        pltpu.VMEM((len(POOL_WINDOWS), POOL_HALO + tm, HEAD_DIM), F32),
                        pltpu.VMEM((4, ATT_HEADS, tm, HEAD_DIM), F32),
                        pltpu.VMEM((3, ATT_HEADS, 4, tm // 4, HEAD_DIM), F32)],
        compiler_params=pltpu.CompilerParams(dimension_semantics=("arbitrary",),
                                             vmem_limit_bytes=VMEM_LIMIT),
        name="in_proj",
    )(x2, g_mix, w_in, qg, kg, cos, sin, pool_w, pool_scale)


def _dilated_attn_kernel(q_ref, k_ref, kh_ref, v_ref, vh_ref, o_ref, st_ref, *, tl, rb):
    n_sub = tl // ATT_BLOCK
    qi = lax.broadcasted_iota(jnp.int32, (ATT_BLOCK, 2 * ATT_BLOCK), 0)
    kj = lax.broadcasted_iota(jnp.int32, (ATT_BLOCK, 2 * ATT_BLOCK), 1)
    delta = qi + ATT_BLOCK - kj
    in_band = jnp.logical_and(delta >= 0, delta <= ATT_BLOCK)
    bias = jnp.where(in_band, 0.0, NEG_INF).astype(F32)
    n_missing = jnp.where(pl.program_id(2) == 0, ATT_BLOCK, 0)
    bias_first = jnp.where(kj < n_missing, NEG_INF, bias)
    lane = lax.broadcasted_iota(jnp.int32, (ATT_BLOCK, LANES), 1)
    ones_blk = jnp.ones((2 * ATT_BLOCK, HEAD_DIM), BF16)

    for r in range(rb):
        for i in range(n_sub):
            rows = slice(i * ATT_BLOCK, (i + 1) * ATT_BLOCK)
            rows2 = slice((i - 1) * ATT_BLOCK, (i + 1) * ATT_BLOCK)
            st = jnp.zeros((ATT_BLOCK, LANES), F32)
            for h in range(ATT_HEADS):
                c0 = r * KV_W + h * HEAD_DIM
                sl = slice(c0, c0 + HEAD_DIM)
                if i == 0:
                    k_blk = jnp.concatenate([kh_ref[0, :, sl], k_ref[0, rows, sl]], axis=0)
                    v_blk = jnp.concatenate([vh_ref[0, :, sl], v_ref[0, rows, sl]], axis=0)
                    b = bias_first
                else:
                    k_blk, v_blk, b = k_ref[0, rows2, sl], v_ref[0, rows2, sl], bias
                s = lax.dot_general(q_ref[0, rows, sl], k_blk, (((1,), (1,)), ((), ())),
                                    preferred_element_type=F32) + b
                m = jnp.max(s, axis=-1, keepdims=True).astype(BF16).astype(F32)
                p = jnp.exp(s - m).astype(BF16)
                acc_l = jnp.dot(p, jnp.concatenate([v_blk, ones_blk], axis=1),
                                preferred_element_type=F32)
                o_ref[0, rows, sl] = acc_l[:, :HEAD_DIM].astype(BF16)
                l_hi = acc_l[:, HEAD_DIM:].astype(BF16).astype(F32)
                st = jnp.where(lane == h, m, st)
                st = jnp.where(lane == ATT_HEADS + h, l_hi, st)
                st = jnp.where(lane == 2 * ATT_HEADS + h, acc_l[:, HEAD_DIM:] - l_hi, st)
            st_ref[0, rows, r * LANES:(r + 1) * LANES] = st.astype(BF16)


def _dilated_attn(q, k, v, batch, seq, dilation):
    sub_len = seq // dilation
    tl = min(sub_len, 1024)
    rb = min(dilation, 1024 // tl)
    n_sub = tl // ATT_BLOCK
    view = lambda a: a.reshape(batch, sub_len, a.shape[1])
    cur = lambda w: pl.BlockSpec((1, tl, rb * w), lambda b, r, i: (b, i, r))
    halo = pl.BlockSpec((1, ATT_BLOCK, rb * KV_W),
                        lambda b, r, i: (b, jnp.maximum(i * n_sub - 1, 0), r))
    acc, st = pl.pallas_call(
        functools.partial(_dilated_attn_kernel, tl=tl, rb=rb),
        grid=(batch, dilation // rb, sub_len // tl),
        in_specs=[cur(KV_W), cur(KV_W), halo, cur(KV_W), halo],
        out_specs=[cur(KV_W), cur(LANES)],
        out_shape=[jax.ShapeDtypeStruct((batch, sub_len, dilation * KV_W), BF16),
                   jax.ShapeDtypeStruct((batch, sub_len, dilation * LANES), BF16)],
        compiler_params=pltpu.CompilerParams(
            dimension_semantics=("parallel", "parallel", "parallel"), vmem_limit_bytes=VMEM_LIMIT),
        name=f"dilated_attn_d{dilation}",
    )(view(q), view(k), view(k), view(v), view(v))
    return (acc.reshape(batch * sub_len, dilation * KV_W), st.reshape(batch * sub_len, dilation * LANES))


def _mem_kv_kernel(mem_ref, g_ref, w_ref, kg_ref, k_ref, v_ref):
    mn = _rms(mem_ref[0], g_ref[...]).astype(BF16)
    kv = jnp.dot(mn, w_ref[...], preferred_element_type=F32)
    for h in range(X_HEADS):
        sl = slice(h * HEAD_DIM, (h + 1) * HEAD_DIM)
        k_ref[0, :, sl] = _rms(kv[:, sl], kg_ref[...]).astype(BF16)
    v_ref[0] = kv[:, X_W:].astype(BF16)


def _mem_kv(mem, g_mem, w_ckv, ckg):
    batch, n_mem, _ = mem.shape
    out = jax.ShapeDtypeStruct((batch, n_mem, X_W), BF16)
    blk = pl.BlockSpec((1, n_mem, X_W), lambda b: (b, 0, 0))
    return pl.pallas_call(
        _mem_kv_kernel,
        grid=(batch,),
        in_specs=[pl.BlockSpec((1, n_mem, D_MODEL), lambda b: (b, 0, 0)), _resident((1, D_MODEL)),
                  _resident((D_MODEL, 2 * X_W)), _resident((1, HEAD_DIM))],
        out_specs=[blk, blk],
        out_shape=[out, out],
        compiler_params=pltpu.CompilerParams(dimension_semantics=("parallel",),
                                             vmem_limit_bytes=VMEM_LIMIT),
        name="mem_kv",
    )(mem, g_mem, w_ckv, ckg)


def _mix_cross_kernel(x_ref, a1_ref, a4_ref, a16_ref, s1_ref, s4_ref, s16_ref, y_ref, wo_ref,
                      gc_ref, wq_ref, qg_ref, km_ref, vm_ref, wc_ref, wgu_ref, wd_ref,
                      h_ref, wgu_o, wd_o, acc_tok, st_tok, cat_a, cat_b, o_ref, *, tm):
    step = pl.program_id(0)
    wgu_o[...] = wgu_ref[...].astype(BF16)
    wd_o[...] = wd_ref[...].astype(BF16)

    def merge(cat_ref):
        for gi, (a_ref, s_ref, d) in enumerate(((a4_ref, s4_ref, 4), (a16_ref, s16_ref, 16))):
            n = tm // d
            for r in range(d):
                dst = pl.ds(r, n, stride=d)
                st_tok[gi, dst, :] = s_ref[:, r * LANES:(r + 1) * LANES].astype(F32)
                for h in range(ATT_HEADS):
                    c0 = r * KV_W + h * HEAD_DIM
                    acc_tok[gi, h, dst, :] = a_ref[:, c0:c0 + HEAD_DIM].astype(F32)

        for i in range(tm // MERGE_CHUNK):
            rows = slice(i * MERGE_CHUNK, (i + 1) * MERGE_CHUNK)
            stats = [s1_ref[rows, :].astype(F32), st_tok[0, rows, :], st_tok[1, rows, :]]
            m_all = jnp.maximum(jnp.maximum(stats[0], stats[1]), stats[2])
            wts = [jnp.exp(s - m_all) for s in stats]
            dens = [s + pltpu.roll(s, LANES - ATT_HEADS, 1) for s in stats]
            for h in range(ATT_HEADS):
                sl = slice(h * HEAD_DIM, (h + 1) * HEAD_DIM)
                accs = (a1_ref[rows, sl].astype(F32), acc_tok[0, h, rows, :], acc_tok[1, h, rows, :])
                num = jnp.zeros((MERGE_CHUNK, HEAD_DIM), F32)
                den = jnp.zeros((MERGE_CHUNK, 1), F32)
                for g in range(len(DILATED_PAIRS)):
                    w = wts[g][:, h:h + 1]
                    num = num + w * accs[g]
                    den = den + w * dens[g][:, ATT_HEADS + h:ATT_HEADS + h + 1]
                cat_ref[rows, sl] = (num / den).astype(BF16)
        cat_ref[:, KV_W:] = y_ref[...]

    def mix(cat_ref):
        n_mem = km_ref.shape[1]
        h1 = x_ref[...] + jnp.dot(cat_ref[...], wo_ref[...], preferred_element_type=F32)
        hn = _rms(h1, gc_ref[...]).astype(BF16)
        qc = jnp.dot(hn, wq_ref[...], preferred_element_type=F32)
        for h in range(X_HEADS):
            sl = slice(h * HEAD_DIM, (h + 1) * HEAD_DIM)
            qh = _rms(qc[:, sl], qg_ref[...]).astype(BF16)
            s = lax.dot_general(qh, km_ref[0, :, sl], (((1,), (1,)), ((), ())),
                                preferred_element_type=F32)
            p = jnp.exp(s - jnp.max(s, axis=-1, keepdims=True)).astype(BF16)
            v_ones = jnp.concatenate([vm_ref[0, :, sl], jnp.ones((n_mem, HEAD_DIM), BF16)], axis=1)
            o_l = jnp.dot(p, v_ones, preferred_element_type=F32)
            o_ref[:, sl] = (o_l[:, :HEAD_DIM] / o_l[:, HEAD_DIM:]).astype(BF16)
        h_ref[...] = h1 + jnp.dot(o_ref[...], wc_ref[...], preferred_element_type=F32)

    @pl.when(step == 0)
    def _():
        cat_b[...] = jnp.zeros(cat_b.shape, BF16)

    @pl.when(step % 2 == 0)
    def _():
        merge(cat_a)
        mix(cat_b)

    @pl.when(step % 2 == 1)
    def _():
        merge(cat_b)
        mix(cat_a)


def _mix_cross(x2, accs, stats, y, w_out, g_cross, w_cq, cqg, k_mem, v_mem, w_co, w_gate_up, w_down,
               seq):
    n_tok = x2.shape[0]
    tm = 512
    n_tiles = n_tok // tm
    tiles_per_seq = seq // tm
    n_mem = k_mem.shape[1]
    ahead = lambda i: (jnp.minimum(i, n_tiles - 1), 0)
    behind = lambda i: (jnp.maximum(i - 1, 0), 0)
    view = lambda d, w: pl.BlockSpec((tm // d, d * w), ahead)
    mem_blk = pl.BlockSpec((1, n_mem, X_W), lambda i: (jnp.maximum(i - 1, 0) // tiles_per_seq, 0, 0))

    def cast_blk(w, rows):
        n_blk = w.shape[0] // rows
        assert n_blk * rows == w.shape[0] and n_blk <= n_tiles + 1
        return pl.BlockSpec((rows, w.shape[1]), lambda i: (jnp.minimum(i, n_blk - 1), 0))

    cast_specs = [cast_blk(w_gate_up, 32), cast_blk(w_down, 128)]
    return pl.pallas_call(
        functools.partial(_mix_cross_kernel, tm=tm),
        grid=(n_tiles + 1,),
        in_specs=[pl.BlockSpec((tm, D_MODEL), behind)] + [view(d, KV_W) for d in DILATIONS]
                 + [view(d, LANES) for d in DILATIONS]
                 + [view(1, POOL_W), _resident((KV_W + POOL_W, D_MODEL)), _resident((1, D_MODEL)),
                    _resident((D_MODEL, X_W)), _resident((1, HEAD_DIM)), mem_blk, mem_blk,
                    _resident((X_W, D_MODEL))] + cast_specs,
        out_specs=[pl.BlockSpec((tm, D_MODEL), behind)] + cast_specs,
        out_shape=[jax.ShapeDtypeStruct((n_tok, D_MODEL), F32),
                   jax.ShapeDtypeStruct(w_gate_up.shape, BF16),
                   jax.ShapeDtypeStruct(w_down.shape, BF16)],
        scratch_shapes=[pltpu.VMEM((2, ATT_HEADS, tm, HEAD_DIM), F32),
                        pltpu.VMEM((2, tm, LANES), F32),
                        pltpu.VMEM((tm, KV_W + POOL_W), BF16),
                        pltpu.VMEM((tm, KV_W + POOL_W), BF16),
                        pltpu.VMEM((tm, X_W), BF16)],
        compiler_params=pltpu.CompilerParams(dimension_semantics=("arbitrary",),
                                             vmem_limit_bytes=VMEM_LIMIT),
        name="mix_cross",
    )(x2, *accs, *stats, y, w_out, g_cross, w_cq, cqg, k_mem, v_mem, w_co, w_gate_up, w_down)


def _ffn_kernel(h_ref, g_ref, wgu_ref, wd_ref, o_ref):
    h = h_ref[...]
    hn = _rms(h, g_ref[...]).astype(BF16)
    gate = jnp.dot(hn, wgu_ref[:, :D_FF], preferred_element_type=F32)
    up = jnp.dot(hn, wgu_ref[:, D_FF:], preferred_element_type=F32)
    act = (gate * jax.nn.sigmoid(gate) * up).astype(BF16)
    o_ref[...] = h + jnp.dot(act, wd_ref[...], preferred_element_type=F32)


def _ffn(h2, g_ffn, w_gate_up, w_down):
    n_tok = h2.shape[0]
    tm = 512
    tok = pl.BlockSpec((tm, D_MODEL), lambda i: (i, 0))
    return pl.pallas_call(
        _ffn_kernel,
        grid=(n_tok // tm,),
        in_specs=[tok, _resident((1, D_MODEL)), _resident((D_MODEL, 2 * D_FF)),
                  _resident((D_FF, D_MODEL))],
        out_specs=tok,
        out_shape=jax.ShapeDtypeStruct((n_tok, D_MODEL), F32),
        compiler_params=pltpu.CompilerParams(dimension_semantics=("parallel",),
                                             vmem_limit_bytes=VMEM_LIMIT),
        name="ffn",
    )(h2, g_ffn, w_gate_up, w_down)


def kernel(x, mem, positions, mix_norm_g, w_in, q_norm_g, k_norm_g, pool_w, pool_scale, w_out,
           cross_norm_g, mem_norm_g, w_cq, w_ckv, cq_norm_g, ck_norm_g, w_co,
           ffn_norm_g, w_gate_up, w_down):
    batch, seq, _ = x.shape
    depth = w_in.shape[0]
    scale = HEAD_DIM ** -0.5
    row = lambda a: a.reshape(1, -1)
    order = _rotary_lane_order()
    h = x.reshape(batch * seq, D_MODEL)
    for layer in range(depth):
        cos, sin, w_in_l, w_out_l, w_cq_l, w_ckv_l, w_co_l = _prep(
            positions, w_in[layer], w_out[layer], w_cq[layer], w_ckv[layer], w_co[layer])
        q1, k1, v1, q4, k4, v4, q16, k16, v16, y = _in_proj(
            h, row(mix_norm_g[layer]), w_in_l, row(q_norm_g[layer][order] * scale),
            row(k_norm_g[layer][order]), cos, sin, pool_w[layer].astype(BF16),
            row(pool_scale[layer]), seq)
        accs, stats = [], []
        for (qv, kv, vv), dilation in zip(((q1, k1, v1), (q4, k4, v4), (q16, k16, v16)), DILATIONS):
            a, s = _dilated_attn(qv, kv, vv, batch, seq, dilation)
            accs.append(a)
            stats.append(s)
        k_mem, v_mem = _mem_kv(mem, row(mem_norm_g[layer]), w_ckv_l, row(ck_norm_g[layer]))
        h, w_gu_l, w_down_l = _mix_cross(
            h, accs, stats, y, w_out_l, row(cross_norm_g[layer]), w_cq_l,
            row(cq_norm_g[layer] * scale), k_mem, v_mem, w_co_l, w_gate_up[layer], w_down[layer], seq)
        h = _ffn(h, row(ffn_norm_g[layer]), w_gu_l, w_down_l)
    return h.reshape(batch, seq, D_MODEL)
```

```python
import functools

import jax
import jax.numpy as jnp
import numpy as np
from jax import lax
from jax.experimental import pallas as pl
from jax.experimental.pallas import tpu as pltpu

D_MODEL = 1024
HEAD_DIM = 128
ATT_HEADS = 4
DILATED_PAIRS = ((128, 1), (512, 4), (2048, 16))
DILATIONS = tuple(d for _, d in DILATED_PAIRS)
KV_W = ATT_HEADS * HEAD_DIM
POOL_WINDOWS = (2, 4, 8, 16)
POOL_W = len(POOL_WINDOWS) * HEAD_DIM
IN_W = (len(DILATED_PAIRS) + 2) * KV_W + POOL_W
ROT_DIM = HEAD_DIM // 4
ROT_HALF = ROT_DIM // 2
ROPE_THETA = 500000.0
X_HEADS = 4
X_W = X_HEADS * HEAD_DIM
D_FF = 2816
EPS = 1e-6
NEG_INF = -1e30
ATT_BLOCK = 128
ATT_STEP_QUERIES = 2048
LANES = 128
SUBLANES = 8
ROT_PAIR_LANE = LANES // 2
ROW_CHUNK = 256
MERGE_CHUNK = 512
POOL_HALO = 32

F32 = jnp.float32
BF16 = jnp.bfloat16

VMEM_LIMIT = 56 * 1024 * 1024


def _rms(x, g):
    ms = jnp.mean(x * x, axis=-1, keepdims=True)
    return x * lax.rsqrt(ms + EPS) * g


def _resident(shape):
    nd = len(shape)
    return pl.BlockSpec(shape, lambda *_: (0,) * nd, pipeline_mode=pl.Buffered(1))


def _rotary_lane_order():
    rest = np.arange(ROT_DIM, HEAD_DIM)
    n_mid = ROT_PAIR_LANE - ROT_HALF
    return np.concatenate([np.arange(ROT_HALF), rest[:n_mid],
                           np.arange(ROT_HALF, ROT_DIM), rest[n_mid:]])


def _prep_kernel(pos_ref, invf_ref, win_ref, wout_ref, wcq_ref, wckv_ref, wco_ref,
                 ctab_ref, stab_ref, win_o, wout_o, wcq_o, wckv_o, wco_o, *, n_qk_heads):
    ang = pos_ref[...] * invf_ref[...]
    cos = jnp.cos(ang)
    sin = jnp.sin(ang)
    n_rows = ang.shape[0]
    per_row = LANES // ROT_HALF
    lane = lax.broadcasted_iota(jnp.int32, (n_rows, LANES), 1)
    first = lane < ROT_HALF
    second = jnp.logical_and(lane >= ROT_PAIR_LANE, lane < ROT_PAIR_LANE + ROT_HALF)
    for j in range(per_row):
        shift = (LANES - ROT_HALF * j) % LANES
        cj = pltpu.roll(cos, shift, 1) if shift else cos
        sj = pltpu.roll(sin, shift, 1) if shift else sin
        dst = pl.ds(j, n_rows, stride=per_row)
        ctab_ref[dst, :] = jnp.where(first, cj, jnp.where(second, pltpu.roll(cj, ROT_PAIR_LANE, 1), 1.0))
        stab_ref[dst, :] = jnp.where(first, -sj, jnp.where(second, pltpu.roll(sj, ROT_PAIR_LANE, 1), 0.0))

    n_mid = ROT_PAIR_LANE - ROT_HALF
    lane_w = lax.broadcasted_iota(jnp.int32, (win_ref.shape[0], LANES), 1)
    keep = jnp.logical_or(lane_w < ROT_HALF, lane_w >= ROT_DIM + n_mid)
    from_low = jnp.logical_and(lane_w >= ROT_PAIR_LANE, lane_w < ROT_PAIR_LANE + ROT_HALF)
    for h in range(n_qk_heads):
        sl = slice(h * HEAD_DIM, (h + 1) * HEAD_DIM)
        w = win_ref[:, sl]
        moved = jnp.where(from_low, pltpu.roll(w, n_mid, 1), pltpu.roll(w, LANES - ROT_HALF, 1))
        win_o[:, sl] = jnp.where(keep, w, moved).astype(BF16)
    rest = slice(n_qk_heads * HEAD_DIM, IN_W)
    win_o[:, rest] = win_ref[:, rest].astype(BF16)
    for src, dst in ((wout_ref, wout_o), (wcq_ref, wcq_o), (wckv_ref, wckv_o), (wco_ref, wco_o)):
        dst[...] = src[...].astype(BF16)


def _prep(positions, w_in, w_out, w_cq, w_ckv, w_co):
    n_tok = positions.size
    per_row = LANES // ROT_HALF
    n_steps = 8
    rows = n_tok // per_row // n_steps
    pos = jnp.repeat(positions.reshape(-1).astype(F32), ROT_HALF).reshape(n_tok // per_row, LANES)
    inv_freq = ROPE_THETA ** (-jnp.arange(0, ROT_DIM, 2, dtype=F32) / ROT_DIM)
    invf = jnp.tile(inv_freq, per_row).reshape(1, LANES)
    weights = (w_in, w_out, w_cq, w_ckv, w_co)
    row_blk = lambda w: pl.BlockSpec((w.shape[0] // n_steps, w.shape[1]), lambda i: (i, 0))
    tab = pl.BlockSpec((rows * per_row, LANES), lambda i: (i, 0))
    return pl.pallas_call(
        functools.partial(_prep_kernel, n_qk_heads=(len(DILATED_PAIRS) + 1) * ATT_HEADS),
        grid=(n_steps,),
        in_specs=[pl.BlockSpec((rows, LANES), lambda i: (i, 0)), pl.BlockSpec((1, LANES), lambda i: (0, 0))]
                 + [row_blk(w) for w in weights],
        out_specs=[tab, tab] + [row_blk(w) for w in weights],
        out_shape=[jax.ShapeDtypeStruct((n_tok, LANES), F32)] * 2
                  + [jax.ShapeDtypeStruct(w.shape, BF16) for w in weights],
        compiler_params=pltpu.CompilerParams(dimension_semantics=("parallel",),
                                             vmem_limit_bytes=VMEM_LIMIT),
        name="prep",
    )(pos, invf, *weights)


def _in_proj_kernel(x_ref, g_ref, w_ref, qg_ref, kg_ref, cos_ref, sin_ref, pw_ref, ps_ref,
                    q1_ref, k1_ref, v1_ref, q4_ref, k4_ref, v4_ref, q16_ref, k16_ref, v16_ref, y_ref,
                    xn_ref, tab_ref, pa_ref, pb_ref, hist_ref, lvl_ref, stage_ref, stage4_ref,
                    *, tm, tiles_per_seq):
    step = pl.program_id(0)
    rc = ROW_CHUNK
    n_rc = tm // rc

    def project(p_ref):
        for i in range(n_rc):
            rows = slice(i * rc, (i + 1) * rc)
            xn_ref[rows, :] = _rms(x_ref[rows, :], g_ref[...]).astype(BF16)
        for c in range(IN_W // KV_W):
            cols = slice(c * KV_W, (c + 1) * KV_W)
            p_ref[:, cols] = jnp.dot(xn_ref[...], w_ref[:, cols], preferred_element_type=F32)

    def finish(p_ref):
        for gi, gain_ref in enumerate((qg_ref, kg_ref)):
            gain = gain_ref[...]
            tab_ref[2 * gi] = cos_ref[...] * gain
            tab_ref[2 * gi + 1] = sin_ref[...] * pltpu.roll(gain, ROT_PAIR_LANE, 1)

        def emit(c, gi, tok_ref, v4_ref, v16_ref, a):
            for h in range(ATT_HEADS):
                sl = slice(h * HEAD_DIM, (h + 1) * HEAD_DIM)
                src = slice(c * KV_W + h * HEAD_DIM, c * KV_W + (h + 1) * HEAD_DIM)
                for i in range(n_rc):
                    rows = slice(i * rc, (i + 1) * rc)
                    t = p_ref[rows, src]
                    if gi >= 0:
                        rs = lax.rsqrt(jnp.mean(t * t, axis=-1, keepdims=True) + EPS)
                        t = (t * tab_ref[2 * gi, rows, :]
                             + pltpu.roll(t, ROT_PAIR_LANE, 1) * tab_ref[2 * gi + 1, rows, :]) * rs
                    if tok_ref is not None:
                        tok_ref[rows, sl] = t.astype(BF16)
                    if v4_ref is not None or v16_ref is not None:
                        stage_ref[a, h, rows, :] = t
                if v4_ref is None and v16_ref is None:
                    continue
                n4 = tm // 4
                for b in range(4):
                    t4 = stage_ref[a, h, pl.ds(b, n4, stride=4), :]
                    if v4_ref is not None:
                        c0 = b * KV_W + h * HEAD_DIM
                        v4_ref[:, c0:c0 + HEAD_DIM] = t4.astype(BF16)
                    if v16_ref is not None:
                        stage4_ref[a, h, b] = t4
                if v16_ref is not None:
                    for b in range(4):
                        for a4 in range(4):
                            c0 = (4 * a4 + b) * KV_W + h * HEAD_DIM
                            v16_ref[:, c0:c0 + HEAD_DIM] = (
                                stage4_ref[a, h, b, pl.ds(a4, n4 // 4, stride=4), :].astype(BF16))

        emit(3, 1, k1_ref, k4_ref, k16_ref, 0)
        emit(4, -1, v1_ref, v4_ref, v16_ref, 1)
        emit(2, 0, None, None, q16_ref, 2)
        emit(1, 0, None, q4_ref, None, 3)
        emit(0, 0, q1_ref, None, None, 0)

        seq_tile = jnp.maximum(step - 1, 0) % tiles_per_seq
        end = POOL_HALO + tm
        hist_ref[0:POOL_HALO, :] = jnp.where(seq_tile == 0, 0.0, 1.0) * hist_ref[tm:end, :]
        u_cols = slice(IN_W - POOL_W, IN_W)
        hist_ref[POOL_HALO:end, :] = p_ref[:, u_cols]
        row16 = lax.broadcasted_iota(jnp.int32, (2 * SUBLANES, HEAD_DIM), 0)
        for g, w in enumerate(POOL_WINDOWS):
            sl = slice(g * HEAD_DIM, (g + 1) * HEAD_DIM)
            src, start, shift = hist_ref, SUBLANES, 1
            while True:
                cols = sl if src is hist_ref else slice(None)
                tot = src[start:end, cols] + src[start - shift:end - shift, cols]
                shift *= 2
                if shift == w:
                    break
                lvl_ref[g, start:end, :] = tot
                src, start = lvl_ref.at[g], start + SUBLANES
            tot = tot[POOL_HALO - start:, :]
            n_first = jnp.maximum(jnp.minimum(row16 + 1, w), jnp.where(seq_tile == 0, 0, w))
            inv_first = 1.0 / n_first.astype(F32)
            ug = hist_ref[POOL_HALO:end, sl]
            d = jnp.concatenate([tot[:2 * SUBLANES] * inv_first, tot[2 * SUBLANES:] * (1.0 / w)],
                                axis=0) - ug
            yg = jnp.dot(d.astype(BF16), pw_ref[g], preferred_element_type=F32) * ps_ref[:, sl]
            y_ref[:, sl] = yg.astype(BF16)

    @pl.when(step == 0)
    def _():
        pb_ref[...] = jnp.zeros(pb_ref.shape, F32)
        hist_ref[...] = jnp.zeros(hist_ref.shape, F32)

    @pl.when(step % 2 == 0)
    def _():
        project(pa_ref)
        finish(pb_ref)

    @pl.when(step % 2 == 1)
    def _():
        project(pb_ref)
        finish(pa_ref)


def _in_proj(x2, g_mix, w_in, qg, kg, cos, sin, pool_w, pool_scale, seq):
    n_tok = x2.shape[0]
    tm = 512
    n_tiles = n_tok // tm
    tiles_per_seq = seq // tm
    ahead = lambda i: (jnp.minimum(i, n_tiles - 1), 0)
    behind = lambda i: (jnp.maximum(i - 1, 0), 0)
    view = lambda d: pl.BlockSpec((tm // d, d * KV_W), behind)
    view_sds = lambda d: jax.ShapeDtypeStruct((n_tok // d, d * KV_W), BF16)
    out_d = (1, 1, 1, 4, 4, 4, 16, 16, 16, 1)
    return pl.pallas_call(
        functools.partial(_in_proj_kernel, tm=tm, tiles_per_seq=tiles_per_seq),
        grid=(n_tiles + 1,),
        in_specs=[pl.BlockSpec((tm, D_MODEL), ahead), _resident((1, D_MODEL)),
                  _resident((D_MODEL, IN_W)), _resident((1, HEAD_DIM)), _resident((1, HEAD_DIM)),
                  pl.BlockSpec((tm, LANES), behind), pl.BlockSpec((tm, LANES), behind),
                  _resident((len(POOL_WINDOWS), HEAD_DIM, HEAD_DIM)), _resident((1, POOL_W))],
        out_specs=[view(d) for d in out_d],
        out_shape=[view_sds(d) for d in out_d],
        scratch_shapes=[pltpu.VMEM((tm, D_MODEL), BF16),
                        pltpu.VMEM((4, tm, LANES), F32),
                        pltpu.VMEM((tm, IN_W), F32),
                        pltpu.VMEM((tm, IN_W), F32),
                        pltpu.VMEM((POOL_HALO + tm, POOL_W), F32),
                        pltpu.VMEM((len(POOL_WINDOWS), POOL_HALO + tm, HEAD_DIM), F32),
                        pltpu.VMEM((4, ATT_HEADS, tm, HEAD_DIM), F32),
                        pltpu.VMEM((3, ATT_HEADS, 4, tm // 4, HEAD_DIM), F32)],
        compiler_params=pltpu.CompilerParams(dimension_semantics=("arbitrary",),
                                             vmem_limit_bytes=VMEM_LIMIT),
        name="in_proj",
    )(x2, g_mix, w_in, qg, kg, cos, sin, pool_w, pool_scale)


def _dilated_attn_kernel(q_ref, k_ref, kh_ref, v_ref, vh_ref, o_ref, st_ref, *, tl, rb):
    n_sub = tl // ATT_BLOCK
    qi = lax.broadcasted_iota(jnp.int32, (ATT_BLOCK, 2 * ATT_BLOCK), 0)
    kj = lax.broadcasted_iota(jnp.int32, (ATT_BLOCK, 2 * ATT_BLOCK), 1)
    delta = qi + ATT_BLOCK - kj
    in_band = jnp.logical_and(delta >= 0, delta <= ATT_BLOCK)
    bias = jnp.where(in_band, 0.0, NEG_INF).astype(F32)
    n_missing = jnp.where(pl.program_id(2) == 0, ATT_BLOCK, 0)
    bias_first = jnp.where(kj < n_missing, NEG_INF, bias)
    lane = lax.broadcasted_iota(jnp.int32, (ATT_BLOCK, LANES), 1)
    ones_blk = jnp.ones((2 * ATT_BLOCK, HEAD_DIM), BF16)

    for r in range(rb):
        for i in range(n_sub):
            rows = slice(i * ATT_BLOCK, (i + 1) * ATT_BLOCK)
            rows2 = slice((i - 1) * ATT_BLOCK, (i + 1) * ATT_BLOCK)
            st = jnp.zeros((ATT_BLOCK, LANES), F32)
            for h in range(ATT_HEADS):
                c0 = r * KV_W + h * HEAD_DIM
                sl = slice(c0, c0 + HEAD_DIM)
                if i == 0:
                    k_blk = jnp.concatenate([kh_ref[0, :, sl], k_ref[0, rows, sl]], axis=0)
                    v_blk = jnp.concatenate([vh_ref[0, :, sl], v_ref[0, rows, sl]], axis=0)
                    b = bias_first
                else:
                    k_blk, v_blk, b = k_ref[0, rows2, sl], v_ref[0, rows2, sl], bias
                s = lax.dot_general(q_ref[0, rows, sl], k_blk, (((1,), (1,)), ((), ())),
                                    preferred_element_type=F32) + b
                m = jnp.max(s, axis=-1, keepdims=True).astype(BF16).astype(F32)
                p = jnp.exp(s - m).astype(BF16)
                acc_l = jnp.dot(p, jnp.concatenate([v_blk, ones_blk], axis=1),
                                preferred_element_type=F32)
                o_ref[0, rows, sl] = acc_l[:, :HEAD_DIM].astype(BF16)
                l_hi = acc_l[:, HEAD_DIM:].astype(BF16).astype(F32)
                st = jnp.where(lane == h, m, st)
                st = jnp.where(lane == ATT_HEADS + h, l_hi, st)
                st = jnp.where(lane == 2 * ATT_HEADS + h, acc_l[:, HEAD_DIM:] - l_hi, st)
            st_ref[0, rows, r * LANES:(r + 1) * LANES] = st.astype(BF16)


def _dilated_attn(q, k, v, batch, seq, dilation):
    sub_len = seq // dilation
    tl = min(sub_len, ATT_STEP_QUERIES)
    rb = min(dilation, ATT_STEP_QUERIES // tl)
    n_sub = tl // ATT_BLOCK
    view = lambda a: a.reshape(batch, sub_len, a.shape[1])
    cur = lambda w: pl.BlockSpec((1, tl, rb * w), lambda b, r, i: (b, i, r))
    halo = pl.BlockSpec((1, ATT_BLOCK, rb * KV_W),
                        lambda b, r, i: (b, jnp.maximum(i * n_sub - 1, 0), r))
    acc, st = pl.pallas_call(
        functools.partial(_dilated_attn_kernel, tl=tl, rb=rb),
        grid=(batch, dilation // rb, sub_len // tl),
        in_specs=[cur(KV_W), cur(KV_W), halo, cur(KV_W), halo],
        out_specs=[cur(KV_W), cur(LANES)],
        out_shape=[jax.ShapeDtypeStruct((batch, sub_len, dilation * KV_W), BF16),
                   jax.ShapeDtypeStruct((batch, sub_len, dilation * LANES), BF16)],
        compiler_params=pltpu.CompilerParams(
            dimension_semantics=("parallel", "parallel", "parallel"), vmem_limit_bytes=VMEM_LIMIT),
        name=f"dilated_attn_d{dilation}",
    )(view(q), view(k), view(k), view(v), view(v))
    return (acc.reshape(batch * sub_len, dilation * KV_W), st.reshape(batch * sub_len, dilation * LANES))


def _mem_kv_kernel(mem_ref, g_ref, w_ref, kg_ref, k_ref, v_ref):
    mn = _rms(mem_ref[0], g_ref[...]).astype(BF16)
    kv = jnp.dot(mn, w_ref[...], preferred_element_type=F32)
    for h in range(X_HEADS):
        sl = slice(h * HEAD_DIM, (h + 1) * HEAD_DIM)
        k_ref[0, :, sl] = _rms(kv[:, sl], kg_ref[...]).astype(BF16)
    v_ref[0] = kv[:, X_W:].astype(BF16)


def _mem_kv(mem, g_mem, w_ckv, ckg):
    batch, n_mem, _ = mem.shape
    out = jax.ShapeDtypeStruct((batch, n_mem, X_W), BF16)
    blk = pl.BlockSpec((1, n_mem, X_W), lambda b: (b, 0, 0))
    return pl.pallas_call(
        _mem_kv_kernel,
        grid=(batch,),
        in_specs=[pl.BlockSpec((1, n_mem, D_MODEL), lambda b: (b, 0, 0)), _resident((1, D_MODEL)),
                  _resident((D_MODEL, 2 * X_W)), _resident((1, HEAD_DIM))],
        out_specs=[blk, blk],
        out_shape=[out, out],
        compiler_params=pltpu.CompilerParams(dimension_semantics=("parallel",),
                                             vmem_limit_bytes=VMEM_LIMIT),
        name="mem_kv",
    )(mem, g_mem, w_ckv, ckg)


def _mix_cross_kernel(x_ref, a1_ref, a4_ref, a16_ref, s1_ref, s4_ref, s16_ref, y_ref, wo_ref,
                      gc_ref, wq_ref, qg_ref, km_ref, vm_ref, wc_ref, wgu_ref, wd_ref,
                      h_ref, wgu_o, wd_o, acc_tok, st_tok, cat_a, cat_b, o_ref, *, tm):
    step = pl.program_id(0)
    wgu_o[...] = wgu_ref[...].astype(BF16)
    wd_o[...] = wd_ref[...].astype(BF16)

    def merge(cat_ref):
        for gi, (a_ref, s_ref, d) in enumerate(((a4_ref, s4_ref, 4), (a16_ref, s16_ref, 16))):
            n = tm // d
            for r in range(d):
                dst = pl.ds(r, n, stride=d)
                st_tok[gi, dst, :] = s_ref[:, r * LANES:(r + 1) * LANES].astype(F32)
                for h in range(ATT_HEADS):
                    c0 = r * KV_W + h * HEAD_DIM
                    acc_tok[gi, h, dst, :] = a_ref[:, c0:c0 + HEAD_DIM].astype(F32)

        for i in range(tm // MERGE_CHUNK):
            rows = slice(i * MERGE_CHUNK, (i + 1) * MERGE_CHUNK)
            stats = [s1_ref[rows, :].astype(F32), st_tok[0, rows, :], st_tok[1, rows, :]]
            m_all = jnp.maximum(jnp.maximum(stats[0], stats[1]), stats[2])
            wts = [jnp.exp(s - m_all) for s in stats]
            dens = [s + pltpu.roll(s, LANES - ATT_HEADS, 1) for s in stats]
            for h in range(ATT_HEADS):
                sl = slice(h * HEAD_DIM, (h + 1) * HEAD_DIM)
                accs = (a1_ref[rows, sl].astype(F32), acc_tok[0, h, rows, :], acc_tok[1, h, rows, :])
                num = jnp.zeros((MERGE_CHUNK, HEAD_DIM), F32)
                den = jnp.zeros((MERGE_CHUNK, 1), F32)
                for g in range(len(DILATED_PAIRS)):
                    w = wts[g][:, h:h + 1]
                    num = num + w * accs[g]
                    den = den + w * dens[g][:, ATT_HEADS + h:ATT_HEADS + h + 1]
                cat_ref[rows, sl] = (num / den).astype(BF16)
        cat_ref[:, KV_W:] = y_ref[...]

    def mix(cat_ref):
        n_mem = km_ref.shape[1]
        h1 = x_ref[...] + jnp.dot(cat_ref[...], wo_ref[...], preferred_element_type=F32)
        hn = _rms(h1, gc_ref[...]).astype(BF16)
        qc = jnp.dot(hn, wq_ref[...], preferred_element_type=F32)
        for h in range(X_HEADS):
            sl = slice(h * HEAD_DIM, (h + 1) * HEAD_DIM)
            qh = _rms(qc[:, sl], qg_ref[...]).astype(BF16)
            s = lax.dot_general(qh, km_ref[0, :, sl], (((1,), (1,)), ((), ())),
                                preferred_element_type=F32)
            p = jnp.exp(s - jnp.max(s, axis=-1, keepdims=True)).astype(BF16)
            v_ones = jnp.concatenate([vm_ref[0, :, sl], jnp.ones((n_mem, HEAD_DIM), BF16)], axis=1)
            o_l = jnp.dot(p, v_ones, preferred_element_type=F32)
            o_ref[:, sl] = (o_l[:, :HEAD_DIM] / o_l[:, HEAD_DIM:]).astype(BF16)
        h_ref[...] = h1 + jnp.dot(o_ref[...], wc_ref[...], preferred_element_type=F32)

    @pl.when(step == 0)
    def _():
        cat_b[...] = jnp.zeros(cat_b.shape, BF16)

    @pl.when(step % 2 == 0)
    def _():
        merge(cat_a)
        mix(cat_b)

    @pl.when(step % 2 == 1)
    def _():
        merge(cat_b)
        mix(cat_a)


def _mix_cross(x2, accs, stats, y, w_out, g_cross, w_cq, cqg, k_mem, v_mem, w_co, w_gate_up, w_down,
               seq):
    n_tok = x2.shape[0]
    tm = 512
    n_tiles = n_tok // tm
    tiles_per_seq = seq // tm
    n_mem = k_mem.shape[1]
    ahead = lambda i: (jnp.minimum(i, n_tiles - 1), 0)
    behind = lambda i: (jnp.maximum(i - 1, 0), 0)
    view = lambda d, w: pl.BlockSpec((tm // d, d * w), ahead)
    mem_blk = pl.BlockSpec((1, n_mem, X_W), lambda i: (jnp.maximum(i - 1, 0) // tiles_per_seq, 0, 0))

    def cast_blk(w, rows):
        n_blk = w.shape[0] // rows
        assert n_blk * rows == w.shape[0] and n_blk <= n_tiles + 1
        return pl.BlockSpec((rows, w.shape[1]), lambda i: (jnp.minimum(i, n_blk - 1), 0))

    cast_specs = [cast_blk(w_gate_up, 32), cast_blk(w_down, 128)]
    return pl.pallas_call(
        functools.partial(_mix_cross_kernel, tm=tm),
        grid=(n_tiles + 1,),
        in_specs=[pl.BlockSpec((tm, D_MODEL), behind)] + [view(d, KV_W) for d in DILATIONS]
                 + [view(d, LANES) for d in DILATIONS]
                 + [view(1, POOL_W), _resident((KV_W + POOL_W, D_MODEL)), _resident((1, D_MODEL)),
                    _resident((D_MODEL, X_W)), _resident((1, HEAD_DIM)), mem_blk, mem_blk,
                    _resident((X_W, D_MODEL))] + cast_specs,
        out_specs=[pl.BlockSpec((tm, D_MODEL), behind)] + cast_specs,
        out_shape=[jax.ShapeDtypeStruct((n_tok, D_MODEL), F32),
                   jax.ShapeDtypeStruct(w_gate_up.shape, BF16),
                   jax.ShapeDtypeStruct(w_down.shape, BF16)],
        scratch_shapes=[pltpu.VMEM((2, ATT_HEADS, tm, HEAD_DIM), F32),
                        pltpu.VMEM((2, tm, LANES), F32),
                        pltpu.VMEM((tm, KV_W + POOL_W), BF16),
                        pltpu.VMEM((tm, KV_W + POOL_W), BF16),
                        pltpu.VMEM((tm, X_W), BF16)],
        compiler_params=pltpu.CompilerParams(dimension_semantics=("arbitrary",),
                                             vmem_limit_bytes=VMEM_LIMIT),
        name="mix_cross",
    )(x2, *accs, *stats, y, w_out, g_cross, w_cq, cqg, k_mem, v_mem, w_co, w_gate_up, w_down)


def _ffn_kernel(h_ref, g_ref, wgu_ref, wd_ref, o_ref):
    h = h_ref[...]
    hn = _rms(h, g_ref[...]).astype(BF16)
    gate = jnp.dot(hn, wgu_ref[:, :D_FF], preferred_element_type=F32)
    up = jnp.dot(hn, wgu_ref[:, D_FF:], preferred_element_type=F32)
    act = (gate * jax.nn.sigmoid(gate) * up).astype(BF16)
    o_ref[...] = h + jnp.dot(act, wd_ref[...], preferred_element_type=F32)


def _ffn(h2, g_ffn, w_gate_up, w_down):
    n_tok = h2.shape[0]
    tm = 512
    tok = pl.BlockSpec((tm, D_MODEL), lambda i: (i, 0))
    return pl.pallas_call(
        _ffn_kernel,
        grid=(n_tok // tm,),
        in_specs=[tok, _resident((1, D_MODEL)), _resident((D_MODEL, 2 * D_FF)),
                  _resident((D_FF, D_MODEL))],
        out_specs=tok,
        out_shape=jax.ShapeDtypeStruct((n_tok, D_MODEL), F32),
        compiler_params=pltpu.CompilerParams(dimension_semantics=("parallel",),
                                             vmem_limit_bytes=VMEM_LIMIT),
        name="ffn",
    )(h2, g_ffn, w_gate_up, w_down)


def kernel(x, mem, positions, mix_norm_g, w_in, q_norm_g, k_norm_g, pool_w, pool_scale, w_out,
           cross_norm_g, mem_norm_g, w_cq, w_ckv, cq_norm_g, ck_norm_g, w_co,
           ffn_norm_g, w_gate_up, w_down):
    batch, seq, _ = x.shape
    depth = w_in.shape[0]
    scale = HEAD_DIM ** -0.5
    row = lambda a: a.reshape(1, -1)
    order = _rotary_lane_order()
    h = x.reshape(batch * seq, D_MODEL)
    for layer in range(depth):
        cos, sin, w_in_l, w_out_l, w_cq_l, w_ckv_l, w_co_l = _prep(
            positions, w_in[layer], w_out[layer], w_cq[layer], w_ckv[layer], w_co[layer])
        q1, k1, v1, q4, k4, v4, q16, k16, v16, y = _in_proj(
            h, row(mix_norm_g[layer]), w_in_l, row(q_norm_g[layer][order] * scale),
            row(k_norm_g[layer][order]), cos, sin, pool_w[layer].astype(BF16),
            row(pool_scale[layer]), seq)
        accs, stats = [], []
        for (qv, kv, vv), dilation in zip(((q1, k1, v1), (q4, k4, v4), (q16, k16, v16)), DILATIONS):
            a, s = _dilated_attn(qv, kv, vv, batch, seq, dilation)
            accs.append(a)
            stats.append(s)
        k_mem, v_mem = _mem_kv(mem, row(mem_norm_g[layer]), w_ckv_l, row(ck_norm_g[layer]))
        h, w_gu_l, w_down_l = _mix_cross(
            h, accs, stats, y, w_out_l, row(cross_norm_g[layer]), w_cq_l,
            row(cq_norm_g[layer] * scale), k_mem, v_mem, w_co_l, w_gate_up[layer], w_down[layer], seq)
        h = _ffn(h, row(ffn_norm_g[layer]), w_gu_l, w_down_l)
    return h.reshape(batch, seq, D_MODEL)
```

```python
import functools

import jax
import jax.numpy as jnp
import numpy as np
from jax import lax
from jax.experimental import pallas as pl
from jax.experimental.pallas import tpu as pltpu

D_MODEL = 1024
HEAD_DIM = 128
ATT_HEADS = 4
DILATED_PAIRS = ((128, 1), (512, 4), (2048, 16))
DILATIONS = tuple(d for _, d in DILATED_PAIRS)
KV_W = ATT_HEADS * HEAD_DIM
POOL_WINDOWS = (2, 4, 8, 16)
POOL_W = len(POOL_WINDOWS) * HEAD_DIM
IN_W = (len(DILATED_PAIRS) + 2) * KV_W + POOL_W
ROT_DIM = HEAD_DIM // 4
ROT_HALF = ROT_DIM // 2
ROPE_THETA = 500000.0
X_HEADS = 4
X_W = X_HEADS * HEAD_DIM
D_FF = 2816
EPS = 1e-6
NEG_INF = -1e30
ATT_BLOCK = 128
ATT_STEP_QUERIES = 2048
LANES = 128
ATT_OUT_W = KV_W + LANES
SUBLANES = 8
ROT_PAIR_LANE = LANES // 2
ROW_CHUNK = 256
MERGE_CHUNK = 512
POOL_HALO = 32

F32 = jnp.float32
BF16 = jnp.bfloat16

VMEM_LIMIT = 56 * 1024 * 1024


def _rms(x, g):
    ms = jnp.mean(x * x, axis=-1, keepdims=True)
    return x * lax.rsqrt(ms + EPS) * g


def _resident(shape):
    nd = len(shape)
    return pl.BlockSpec(shape, lambda *_: (0,) * nd, pipeline_mode=pl.Buffered(1))


def _rotary_lane_order():
    rest = np.arange(ROT_DIM, HEAD_DIM)
    n_mid = ROT_PAIR_LANE - ROT_HALF
    return np.concatenate([np.arange(ROT_HALF), rest[:n_mid],
                           np.arange(ROT_HALF, ROT_DIM), rest[n_mid:]])


def _prep_kernel(pos_ref, invf_ref, win_ref, wout_ref, wcq_ref, wckv_ref, wco_ref,
                 ctab_ref, stab_ref, win_o, wout_o, wcq_o, wckv_o, wco_o, *, n_qk_heads):
    ang = pos_ref[...] * invf_ref[...]
    cos = jnp.cos(ang)
    sin = jnp.sin(ang)
    n_rows = ang.shape[0]
    per_row = LANES // ROT_HALF
    lane = lax.broadcasted_iota(jnp.int32, (n_rows, LANES), 1)
    first = lane < ROT_HALF
    second = jnp.logical_and(lane >= ROT_PAIR_LANE, lane < ROT_PAIR_LANE + ROT_HALF)
    for j in range(per_row):
        shift = (LANES - ROT_HALF * j) % LANES
        cj = pltpu.roll(cos, shift, 1) if shift else cos
        sj = pltpu.roll(sin, shift, 1) if shift else sin
        dst = pl.ds(j, n_rows, stride=per_row)
        ctab_ref[dst, :] = jnp.where(first, cj, jnp.where(second, pltpu.roll(cj, ROT_PAIR_LANE, 1), 1.0))
        stab_ref[dst, :] = jnp.where(first, -sj, jnp.where(second, pltpu.roll(sj, ROT_PAIR_LANE, 1), 0.0))

    n_mid = ROT_PAIR_LANE - ROT_HALF
    lane_w = lax.broadcasted_iota(jnp.int32, (win_ref.shape[0], LANES), 1)
    keep = jnp.logical_or(lane_w < ROT_HALF, lane_w >= ROT_DIM + n_mid)
    from_low = jnp.logical_and(lane_w >= ROT_PAIR_LANE, lane_w < ROT_PAIR_LANE + ROT_HALF)
    for h in range(n_qk_heads):
        sl = slice(h * HEAD_DIM, (h + 1) * HEAD_DIM)
        w = win_ref[:, sl]
        moved = jnp.where(from_low, pltpu.roll(w, n_mid, 1), pltpu.roll(w, LANES - ROT_HALF, 1))
        win_o[:, sl] = jnp.where(keep, w, moved).astype(BF16)
    rest = slice(n_qk_heads * HEAD_DIM, IN_W)
    win_o[:, rest] = win_ref[:, rest].astype(BF16)
    for src, dst in ((wout_ref, wout_o), (wcq_ref, wcq_o), (wckv_ref, wckv_o), (wco_ref, wco_o)):
        dst[...] = src[...].astype(BF16)


def _prep(positions, w_in, w_out, w_cq, w_ckv, w_co):
    n_tok = positions.size
    per_row = LANES // ROT_HALF
    n_steps = 8
    rows = n_tok // per_row // n_steps
    pos = jnp.repeat(positions.reshape(-1).astype(F32), ROT_HALF).reshape(n_tok // per_row, LANES)
    inv_freq = ROPE_THETA ** (-jnp.arange(0, ROT_DIM, 2, dtype=F32) / ROT_DIM)
    invf = jnp.tile(inv_freq, per_row).reshape(1, LANES)
    weights = (w_in, w_out, w_cq, w_ckv, w_co)
    row_blk = lambda w: pl.BlockSpec((w.shape[0] // n_steps, w.shape[1]), lambda i: (i, 0))
    tab = pl.BlockSpec((rows * per_row, LANES), lambda i: (i, 0))
    return pl.pallas_call(
        functools.partial(_prep_kernel, n_qk_heads=(len(DILATED_PAIRS) + 1) * ATT_HEADS),
        grid=(n_steps,),
        in_specs=[pl.BlockSpec((rows, LANES), lambda i: (i, 0)), pl.BlockSpec((1, LANES), lambda i: (0, 0))]
                 + [row_blk(w) for w in weights],
        out_specs=[tab, tab] + [row_blk(w) for w in weights],
        out_shape=[jax.ShapeDtypeStruct((n_tok, LANES), F32)] * 2
                  + [jax.ShapeDtypeStruct(w.shape, BF16) for w in weights],
        compiler_params=pltpu.CompilerParams(dimension_semantics=("parallel",),
                                             vmem_limit_bytes=VMEM_LIMIT),
        name="prep",
    )(pos, invf, *weights)


def _in_proj_kernel(x_ref, g_ref, w_ref, qg_ref, kg_ref, cos_ref, sin_ref, pw_ref, ps_ref,
                    tok_ref, view4_ref, view16_ref, y_ref,
                    xn_ref, tab_ref, pa_ref, pb_ref, hist_ref, lvl_ref, stage_ref, stage4_ref,
                    *, tm, tiles_per_seq):
    step = pl.program_id(0)
    rc = ROW_CHUNK
    n_rc = tm // rc

    def project(p_ref):
        for i in range(n_rc):
            rows = slice(i * rc, (i + 1) * rc)
            xn_ref[rows, :] = _rms(x_ref[rows, :], g_ref[...]).astype(BF16)
        for c in range(IN_W // KV_W):
            cols = slice(c * KV_W, (c + 1) * KV_W)
            p_ref[:, cols] = jnp.dot(xn_ref[...], w_ref[:, cols], preferred_element_type=F32)

    def finish(p_ref):
        for gi, gain_ref in enumerate((qg_ref, kg_ref)):
            gain = gain_ref[...]
            tab_ref[2 * gi] = cos_ref[...] * gain
            tab_ref[2 * gi + 1] = sin_ref[...] * pltpu.roll(gain, ROT_PAIR_LANE, 1)

        def emit(c, gi, part, to_tok, to_v4, to_v16, a):
            for h in range(ATT_HEADS):
                src = slice(c * KV_W + h * HEAD_DIM, c * KV_W + (h + 1) * HEAD_DIM)
                for i in range(n_rc):
                    rows = slice(i * rc, (i + 1) * rc)
                    t = p_ref[rows, src]
                    if gi >= 0:
                        rs = lax.rsqrt(jnp.mean(t * t, axis=-1, keepdims=True) + EPS)
                        t = (t * tab_ref[2 * gi, rows, :]
                             + pltpu.roll(t, ROT_PAIR_LANE, 1) * tab_ref[2 * gi + 1, rows, :]) * rs
                    if to_tok:
                        c0 = part * KV_W + h * HEAD_DIM
                        tok_ref[rows, c0:c0 + HEAD_DIM] = t.astype(BF16)
                    if to_v4 or to_v16:
                        stage_ref[a, h, rows, :] = t
                if not (to_v4 or to_v16):
                    continue
                n4 = tm // 4
                for b in range(4):
                    t4 = stage_ref[a, h, pl.ds(b, n4, stride=4), :]
                    if to_v4:
                        c0 = (part * 4 + b) * KV_W + h * HEAD_DIM
                        view4_ref[:, c0:c0 + HEAD_DIM] = t4.astype(BF16)
                    if to_v16:
                        stage4_ref[a, h, b] = t4
                if to_v16:
                    for b in range(4):
                        for a4 in range(4):
                            c0 = (part * 16 + 4 * a4 + b) * KV_W + h * HEAD_DIM
                            view16_ref[:, c0:c0 + HEAD_DIM] = (
                                stage4_ref[a, h, b, pl.ds(a4, n4 // 4, stride=4), :].astype(BF16))

        emit(3, 1, 1, True, True, True, 0)
        emit(4, -1, 2, True, True, True, 1)
        emit(2, 0, 0, False, False, True, 2)
        emit(1, 0, 0, False, True, False, 3)
        emit(0, 0, 0, True, False, False, 0)

        seq_tile = jnp.maximum(step - 1, 0) % tiles_per_seq
        end = POOL_HALO + tm
        hist_ref[0:POOL_HALO, :] = jnp.where(seq_tile == 0, 0.0, 1.0) * hist_ref[tm:end, :]
        u_cols = slice(IN_W - POOL_W, IN_W)
        hist_ref[POOL_HALO:end, :] = p_ref[:, u_cols]
        row16 = lax.broadcasted_iota(jnp.int32, (2 * SUBLANES, HEAD_DIM), 0)
        for g, w in enumerate(POOL_WINDOWS):
            sl = slice(g * HEAD_DIM, (g + 1) * HEAD_DIM)
            src, start, shift = hist_ref, SUBLANES, 1
            while True:
                cols = sl if src is hist_ref else slice(None)
                tot = src[start:end, cols] + src[start - shift:end - shift, cols]
                shift *= 2
                if shift == w:
                    break
                lvl_ref[g, start:end, :] = tot
                src, start = lvl_ref.at[g], start + SUBLANES
            tot = tot[POOL_HALO - start:, :]
            n_first = jnp.maximum(jnp.minimum(row16 + 1, w), jnp.where(seq_tile == 0, 0, w))
            inv_first = 1.0 / n_first.astype(F32)
            ug = hist_ref[POOL_HALO:end, sl]
            d = jnp.concatenate([tot[:2 * SUBLANES] * inv_first, tot[2 * SUBLANES:] * (1.0 / w)],
                                axis=0) - ug
            yg = jnp.dot(d.astype(BF16), pw_ref[g], preferred_element_type=F32) * ps_ref[:, sl]
            y_ref[:, sl] = yg.astype(BF16)

    @pl.when(step == 0)
    def _():
        pb_ref[...] = jnp.zeros(pb_ref.shape, F32)
        hist_ref[...] = jnp.zeros(hist_ref.shape, F32)

    @pl.when(step % 2 == 0)
    def _():
        project(pa_ref)
        finish(pb_ref)

    @pl.when(step % 2 == 1)
    def _():
        project(pb_ref)
        finish(pa_ref)


def _in_proj(x2, g_mix, w_in, qg, kg, cos, sin, pool_w, pool_scale, seq):
    n_tok = x2.shape[0]
    tm = 512
    n_tiles = n_tok // tm
    tiles_per_seq = seq // tm
    ahead = lambda i: (jnp.minimum(i, n_tiles - 1), 0)
    behind = lambda i: (jnp.maximum(i - 1, 0), 0)
    out_dw = [(d, 3 * KV_W) for d in DILATIONS] + [(1, POOL_W)]
    view = lambda d, w: pl.BlockSpec((tm // d, d * w), behind)
    view_sds = lambda d, w: jax.ShapeDtypeStruct((n_tok // d, d * w), BF16)
    return pl.pallas_call(
        functools.partial(_in_proj_kernel, tm=tm, tiles_per_seq=tiles_per_seq),
        grid=(n_tiles + 1,),
        in_specs=[pl.BlockSpec((tm, D_MODEL), ahead), _resident((1, D_MODEL)),
                  _resident((D_MODEL, IN_W)), _resident((1, HEAD_DIM)), _resident((1, HEAD_DIM)),
                  pl.BlockSpec((tm, LANES), behind), pl.BlockSpec((tm, LANES), behind),
                  _resident((len(POOL_WINDOWS), HEAD_DIM, HEAD_DIM)), _resident((1, POOL_W))],
        out_specs=[view(d, w) for d, w in out_dw],
        out_shape=[view_sds(d, w) for d, w in out_dw],
        scratch_shapes=[pltpu.VMEM((tm, D_MODEL), BF16),
                        pltpu.VMEM((4, tm, LANES), F32),
                        pltpu.VMEM((tm, IN_W), F32),
                        pltpu.VMEM((tm, IN_W), F32),
                        pltpu.VMEM((POOL_HALO + tm, POOL_W), F32),
                        pltpu.VMEM((len(POOL_WINDOWS), POOL_HALO + tm, HEAD_DIM), F32),
                        pltpu.VMEM((4, ATT_HEADS, tm, HEAD_DIM), F32),
                        pltpu.VMEM((3, ATT_HEADS, 4, tm // 4, HEAD_DIM), F32)],
        compiler_params=pltpu.CompilerParams(dimension_semantics=("arbitrary",),
                                             vmem_limit_bytes=VMEM_LIMIT),
        name="in_proj",
    )(x2, g_mix, w_in, qg, kg, cos, sin, pool_w, pool_scale)


def _dilated_attn_kernel(q_ref, k_ref, kh_ref, v_ref, vh_ref, o_ref, *, tl, rb):
    n_sub = tl // ATT_BLOCK
    qi = lax.broadcasted_iota(jnp.int32, (ATT_BLOCK, 2 * ATT_BLOCK), 0)
    kj = lax.broadcasted_iota(jnp.int32, (ATT_BLOCK, 2 * ATT_BLOCK), 1)
    delta = qi + ATT_BLOCK - kj
    in_band = jnp.logical_and(delta >= 0, delta <= ATT_BLOCK)
    bias = jnp.where(in_band, 0.0, NEG_INF).astype(F32)
    n_missing = jnp.where(pl.program_id(2) == 0, ATT_BLOCK, 0)
    bias_first = jnp.where(kj < n_missing, NEG_INF, bias)
    lane = lax.broadcasted_iota(jnp.int32, (ATT_BLOCK, LANES), 1)
    ones_blk = jnp.ones((2 * ATT_BLOCK, HEAD_DIM), BF16)

    for r in range(rb):
        for i in range(n_sub):
            rows = slice(i * ATT_BLOCK, (i + 1) * ATT_BLOCK)
            rows2 = slice((i - 1) * ATT_BLOCK, (i + 1) * ATT_BLOCK)
            st = jnp.zeros((ATT_BLOCK, LANES), F32)
            for h in range(ATT_HEADS):
                c0 = r * KV_W + h * HEAD_DIM
                sl = slice(c0, c0 + HEAD_DIM)
                if i == 0:
                    k_blk = jnp.concatenate([kh_ref[0, :, sl], k_ref[0, rows, sl]], axis=0)
                    v_blk = jnp.concatenate([vh_ref[0, :, sl], v_ref[0, rows, sl]], axis=0)
                    b = bias_first
                else:
                    k_blk, v_blk, b = k_ref[0, rows2, sl], v_ref[0, rows2, sl], bias
                s = lax.dot_general(q_ref[0, rows, sl], k_blk, (((1,), (1,)), ((), ())),
                                    preferred_element_type=F32) + b
                m = jnp.max(s, axis=-1, keepdims=True).astype(BF16).astype(F32)
                p = jnp.exp(s - m).astype(BF16)
                acc_l = jnp.dot(p, jnp.concatenate([v_blk, ones_blk], axis=1),
                                preferred_element_type=F32)
                o0 = r * ATT_OUT_W + h * HEAD_DIM
                o_ref[0, rows, o0:o0 + HEAD_DIM] = acc_l[:, :HEAD_DIM].astype(BF16)
                l_hi = acc_l[:, HEAD_DIM:].astype(BF16).astype(F32)
                st = jnp.where(lane == h, m, st)
                st = jnp.where(lane == ATT_HEADS + h, l_hi, st)
                st = jnp.where(lane == 2 * ATT_HEADS + h, acc_l[:, HEAD_DIM:] - l_hi, st)
            o_ref[0, rows, r * ATT_OUT_W + KV_W:(r + 1) * ATT_OUT_W] = st.astype(BF16)


def _dilated_attn(qkv, batch, seq, dilation):
    sub_len = seq // dilation
    tl = min(sub_len, ATT_STEP_QUERIES)
    rb = min(dilation, ATT_STEP_QUERIES // tl)
    n_sub = tl // ATT_BLOCK
    n_col = dilation // rb
    qkv3 = qkv.reshape(batch, sub_len, qkv.shape[1])
    cur = lambda part: pl.BlockSpec((1, tl, rb * KV_W), lambda b, r, i: (b, i, part * n_col + r))
    halo = lambda part: pl.BlockSpec(
        (1, ATT_BLOCK, rb * KV_W), lambda b, r, i: (b, jnp.maximum(i * n_sub - 1, 0), part * n_col + r))
    out = pl.pallas_call(
        functools.partial(_dilated_attn_kernel, tl=tl, rb=rb),
        grid=(batch, n_col, sub_len // tl),
        in_specs=[cur(0), cur(1), halo(1), cur(2), halo(2)],
        out_specs=pl.BlockSpec((1, tl, rb * ATT_OUT_W), lambda b, r, i: (b, i, r)),
        out_shape=jax.ShapeDtypeStruct((batch, sub_len, dilation * ATT_OUT_W), BF16),
        compiler_params=pltpu.CompilerParams(
            dimension_semantics=("parallel", "parallel", "parallel"), vmem_limit_bytes=VMEM_LIMIT),
        name=f"dilated_attn_d{dilation}",
    )(qkv3, qkv3, qkv3, qkv3, qkv3)
    return out.reshape(batch * sub_len, dilation * ATT_OUT_W)


def _mem_kv_kernel(mem_ref, g_ref, w_ref, kg_ref, k_ref, v_ref):
    mn = _rms(mem_ref[0], g_ref[...]).astype(BF16)
    kv = jnp.dot(mn, w_ref[...], preferred_element_type=F32)
    for h in range(X_HEADS):
        sl = slice(h * HEAD_DIM, (h + 1) * HEAD_DIM)
        k_ref[0, :, sl] = _rms(kv[:, sl], kg_ref[...]).astype(BF16)
    v_ref[0] = kv[:, X_W:].astype(BF16)


def _mem_kv(mem, g_mem, w_ckv, ckg):
    batch, n_mem, _ = mem.shape
    out = jax.ShapeDtypeStruct((batch, n_mem, X_W), BF16)
    blk = pl.BlockSpec((1, n_mem, X_W), lambda b: (b, 0, 0))
    return pl.pallas_call(
        _mem_kv_kernel,
        grid=(batch,),
        in_specs=[pl.BlockSpec((1, n_mem, D_MODEL), lambda b: (b, 0, 0)), _resident((1, D_MODEL)),
                  _resident((D_MODEL, 2 * X_W)), _resident((1, HEAD_DIM))],
        out_specs=[blk, blk],
        out_shape=[out, out],
        compiler_params=pltpu.CompilerParams(dimension_semantics=("parallel",),
                                             vmem_limit_bytes=VMEM_LIMIT),
        name="mem_kv",
    )(mem, g_mem, w_ckv, ckg)


def _mix_cross_kernel(x_ref, a1_ref, a4_ref, a16_ref, y_ref, wo_ref,
                      gc_ref, wq_ref, qg_ref, km_ref, vm_ref, wc_ref, wgu_ref, wd_ref,
                      h_ref, wgu_o, wd_o, acc_tok, st_tok, cat_a, cat_b, o_ref, *, tm):
    step = pl.program_id(0)
    wgu_o[...] = wgu_ref[...].astype(BF16)
    wd_o[...] = wd_ref[...].astype(BF16)

    def merge(cat_ref):
        for gi, (a_ref, d) in enumerate(((a4_ref, 4), (a16_ref, 16))):
            n = tm // d
            for r in range(d):
                dst = pl.ds(r, n, stride=d)
                st_tok[gi, dst, :] = a_ref[:, r * ATT_OUT_W + KV_W:(r + 1) * ATT_OUT_W].astype(F32)
                for h in range(ATT_HEADS):
                    c0 = r * ATT_OUT_W + h * HEAD_DIM
                    acc_tok[gi, h, dst, :] = a_ref[:, c0:c0 + HEAD_DIM].astype(F32)

        for i in range(tm // MERGE_CHUNK):
            rows = slice(i * MERGE_CHUNK, (i + 1) * MERGE_CHUNK)
            stats = [a1_ref[rows, KV_W:].astype(F32), st_tok[0, rows, :], st_tok[1, rows, :]]
            m_all = jnp.maximum(jnp.maximum(stats[0], stats[1]), stats[2])
            wts = [jnp.exp(s - m_all) for s in stats]
            dens = [s + pltpu.roll(s, LANES - ATT_HEADS, 1) for s in stats]
            for h in range(ATT_HEADS):
                sl = slice(h * HEAD_DIM, (h + 1) * HEAD_DIM)
                accs = (a1_ref[rows, sl].astype(F32), acc_tok[0, h, rows, :], acc_tok[1, h, rows, :])
                num = jnp.zeros((MERGE_CHUNK, HEAD_DIM), F32)
                den = jnp.zeros((MERGE_CHUNK, 1), F32)
                for g in range(len(DILATED_PAIRS)):
                    w = wts[g][:, h:h + 1]
                    num = num + w * accs[g]
                    den = den + w * dens[g][:, ATT_HEADS + h:ATT_HEADS + h + 1]
                cat_ref[rows, sl] = (num / den).astype(BF16)
        cat_ref[:, KV_W:] = y_ref[...]

    def mix(cat_ref):
        n_mem = km_ref.shape[1]
        h1 = x_ref[...] + jnp.dot(cat_ref[...], wo_ref[...], preferred_element_type=F32)
        hn = _rms(h1, gc_ref[...]).astype(BF16)
        qc = jnp.dot(hn, wq_ref[...], preferred_element_type=F32)
        for h in range(X_HEADS):
            sl = slice(h * HEAD_DIM, (h + 1) * HEAD_DIM)
            qh = _rms(qc[:, sl], qg_ref[...]).astype(BF16)
            s = lax.dot_general(qh, km_ref[0, :, sl], (((1,), (1,)), ((), ())),
                                preferred_element_type=F32)
            p = jnp.exp(s - jnp.max(s, axis=-1, keepdims=True)).astype(BF16)
            v_ones = jnp.concatenate([vm_ref[0, :, sl], jnp.ones((n_mem, HEAD_DIM), BF16)], axis=1)
            o_l = jnp.dot(p, v_ones, preferred_element_type=F32)
            o_ref[:, sl] = (o_l[:, :HEAD_DIM] / o_l[:, HEAD_DIM:]).astype(BF16)
        h_ref[...] = h1 + jnp.dot(o_ref[...], wc_ref[...], preferred_element_type=F32)

    @pl.when(step == 0)
    def _():
        cat_b[...] = jnp.zeros(cat_b.shape, BF16)

    @pl.when(step % 2 == 0)
    def _():
        merge(cat_a)
        mix(cat_b)

    @pl.when(step % 2 == 1)
    def _():
        merge(cat_b)
        mix(cat_a)


def _mix_cross(x2, branches, y, w_out, g_cross, w_cq, cqg, k_mem, v_mem, w_co, w_gate_up, w_down, seq):
    n_tok = x2.shape[0]
    tm = 512
    n_tiles = n_tok // tm
    tiles_per_seq = seq // tm
    n_mem = k_mem.shape[1]
    ahead = lambda i: (jnp.minimum(i, n_tiles - 1), 0)
    behind = lambda i: (jnp.maximum(i - 1, 0), 0)
    view = lambda d, w: pl.BlockSpec((tm // d, d * w), ahead)
    mem_blk = pl.BlockSpec((1, n_mem, X_W), lambda i: (jnp.maximum(i - 1, 0) // tiles_per_seq, 0, 0))

    def cast_blk(w, rows):
        n_blk = w.shape[0] // rows
        assert n_blk * rows == w.shape[0] and n_blk <= n_tiles + 1
        return pl.BlockSpec((rows, w.shape[1]), lambda i: (jnp.minimum(i, n_blk - 1), 0))

    cast_specs = [cast_blk(w_gate_up, 32), cast_blk(w_down, 128)]
    return pl.pallas_call(
        functools.partial(_mix_cross_kernel, tm=tm),
        grid=(n_tiles + 1,),
        in_specs=[pl.BlockSpec((tm, D_MODEL), behind)] + [view(d, ATT_OUT_W) for d in DILATIONS]
                 + [view(1, POOL_W), _resident((KV_W + POOL_W, D_MODEL)), _resident((1, D_MODEL)),
                    _resident((D_MODEL, X_W)), _resident((1, HEAD_DIM)), mem_blk, mem_blk,
                    _resident((X_W, D_MODEL))] + cast_specs,
        out_specs=[pl.BlockSpec((tm, D_MODEL), behind)] + cast_specs,
        out_shape=[jax.ShapeDtypeStruct((n_tok, D_MODEL), F32),
                   jax.ShapeDtypeStruct(w_gate_up.shape, BF16),
                   jax.ShapeDtypeStruct(w_down.shape, BF16)],
        scratch_shapes=[pltpu.VMEM((2, ATT_HEADS, tm, HEAD_DIM), F32),
                        pltpu.VMEM((2, tm, LANES), F32),
                        pltpu.VMEM((tm, KV_W + POOL_W), BF16),
                        pltpu.VMEM((tm, KV_W + POOL_W), BF16),
                        pltpu.VMEM((tm, X_W), BF16)],
        compiler_params=pltpu.CompilerParams(dimension_semantics=("arbitrary",),
                                             vmem_limit_bytes=VMEM_LIMIT),
        name="mix_cross",
    )(x2, *branches, y, w_out, g_cross, w_cq, cqg, k_mem, v_mem, w_co, w_gate_up, w_down)


def _ffn_kernel(h_ref, g_ref, wgu_ref, wd_ref, o_ref):
    h = h_ref[...]
    hn = _rms(h, g_ref[...]).astype(BF16)
    gate = jnp.dot(hn, wgu_ref[:, :D_FF], preferred_element_type=F32)
    up = jnp.dot(hn, wgu_ref[:, D_FF:], preferred_element_type=F32)
    act = (gate * jax.nn.sigmoid(gate) * up).astype(BF16)
    o_ref[...] = h + jnp.dot(act, wd_ref[...], preferred_element_type=F32)


def _ffn(h2, g_ffn, w_gate_up, w_down):
    n_tok = h2.shape[0]
    tm = 512
    tok = pl.BlockSpec((tm, D_MODEL), lambda i: (i, 0))
    return pl.pallas_call(
        _ffn_kernel,
        grid=(n_tok // tm,),
        in_specs=[tok, _resident((1, D_MODEL)), _resident((D_MODEL, 2 * D_FF)),
                  _resident((D_FF, D_MODEL))],
        out_specs=tok,
        out_shape=jax.ShapeDtypeStruct((n_tok, D_MODEL), F32),
        compiler_params=pltpu.CompilerParams(dimension_semantics=("parallel",),
                                             vmem_limit_bytes=VMEM_LIMIT),
        name="ffn",
    )(h2, g_ffn, w_gate_up, w_down)


def kernel(x, mem, positions, mix_norm_g, w_in, q_norm_g, k_norm_g, pool_w, pool_scale, w_out,
           cross_norm_g, mem_norm_g, w_cq, w_ckv, cq_norm_g, ck_norm_g, w_co,
           ffn_norm_g, w_gate_up, w_down):
    batch, seq, _ = x.shape
    depth = w_in.shape[0]
    scale = HEAD_DIM ** -0.5
    row = lambda a: a.reshape(1, -1)
    order = _rotary_lane_order()
    h = x.reshape(batch * seq, D_MODEL)
    for layer in range(depth):
        cos, sin, w_in_l, w_out_l, w_cq_l, w_ckv_l, w_co_l = _prep(
            positions, w_in[layer], w_out[layer], w_cq[layer], w_ckv[layer], w_co[layer])
        *qkv_views, y = _in_proj(
            h, row(mix_norm_g[layer]), w_in_l, row(q_norm_g[layer][order] * scale),
            row(k_norm_g[layer][order]), cos, sin, pool_w[layer].astype(BF16),
            row(pool_scale[layer]), seq)
        branches = [_dilated_attn(qkv, batch, seq, d) for qkv, d in zip(qkv_views, DILATIONS)]
        k_mem, v_mem = _mem_kv(mem, row(mem_norm_g[layer]), w_ckv_l, row(ck_norm_g[layer]))
        h, w_gu_l, w_down_l = _mix_cross(
            h, branches, y, w_out_l, row(cross_norm_g[layer]), w_cq_l,
            row(cq_norm_g[layer] * scale), k_mem, v_mem, w_co_l, w_gate_up[layer], w_down[layer], seq)
        h = _ffn(h, row(ffn_norm_g[layer]), w_gu_l, w_down_l)
    return h.reshape(batch, seq, D_MODEL)
```

```python
import functools

import jax
import jax.numpy as jnp
import numpy as np
from jax import lax
from jax.experimental import pallas as pl
from jax.experimental.pallas import tpu as pltpu

D_MODEL = 1024
HEAD_DIM = 128
ATT_HEADS = 4
DILATED_PAIRS = ((128, 1), (512, 4), (2048, 16))
DILATIONS = tuple(d for _, d in DILATED_PAIRS)
KV_W = ATT_HEADS * HEAD_DIM
POOL_WINDOWS = (2, 4, 8, 16)
POOL_W = len(POOL_WINDOWS) * HEAD_DIM
IN_W = (len(DILATED_PAIRS) + 2) * KV_W + POOL_W
ROT_DIM = HEAD_DIM // 4
ROT_HALF = ROT_DIM // 2
ROPE_THETA = 500000.0
X_HEADS = 4
X_W = X_HEADS * HEAD_DIM
D_FF = 2816
EPS = 1e-6
NEG_INF = -1e30
ATT_BLOCK = 128
ATT_STEP_QUERIES = 2048
LANES = 128
ATT_OUT_W = KV_W + LANES
SUBLANES = 8
BF16_TILE_ROWS = 2 * SUBLANES
ROT_PAIR_LANE = LANES // 2
ROW_CHUNK = 256
MERGE_CHUNK = 512
POOL_HALO = 32

F32 = jnp.float32
BF16 = jnp.bfloat16

VMEM_LIMIT = 56 * 1024 * 1024


def _rms(x, g):
    ms = jnp.mean(x * x, axis=-1, keepdims=True)
    return x * lax.rsqrt(ms + EPS) * g


def _resident(shape):
    nd = len(shape)
    return pl.BlockSpec(shape, lambda *_: (0,) * nd, pipeline_mode=pl.Buffered(1))


def _rotary_lane_order():
    rest = np.arange(ROT_DIM, HEAD_DIM)
    n_mid = ROT_PAIR_LANE - ROT_HALF
    return np.concatenate([np.arange(ROT_HALF), rest[:n_mid],
                           np.arange(ROT_HALF, ROT_DIM), rest[n_mid:]])


def _prep_kernel(pos_ref, invf_ref, win_ref, ctab_ref, stab_ref, win_o, *, n_qk_heads):
    ang = pos_ref[...] * invf_ref[...]
    cos = jnp.cos(ang)
    sin = jnp.sin(ang)
    n_rows = ang.shape[0]
    per_row = LANES // ROT_HALF
    lane = lax.broadcasted_iota(jnp.int32, (n_rows, LANES), 1)
    first = lane < ROT_HALF
    second = jnp.logical_and(lane >= ROT_PAIR_LANE, lane < ROT_PAIR_LANE + ROT_HALF)
    for j in range(per_row):
        shift = (LANES - ROT_HALF * j) % LANES
        cj = pltpu.roll(cos, shift, 1) if shift else cos
        sj = pltpu.roll(sin, shift, 1) if shift else sin
        dst = pl.ds(j, n_rows, stride=per_row)
        ctab_ref[dst, :] = jnp.where(first, cj, jnp.where(second, pltpu.roll(cj, ROT_PAIR_LANE, 1), 1.0))
        stab_ref[dst, :] = jnp.where(first, -sj, jnp.where(second, pltpu.roll(sj, ROT_PAIR_LANE, 1), 0.0))

    n_mid = ROT_PAIR_LANE - ROT_HALF
    lane_w = lax.broadcasted_iota(jnp.int32, (win_ref.shape[0], LANES), 1)
    keep = jnp.logical_or(lane_w < ROT_HALF, lane_w >= ROT_DIM + n_mid)
    from_low = jnp.logical_and(lane_w >= ROT_PAIR_LANE, lane_w < ROT_PAIR_LANE + ROT_HALF)
    for h in range(n_qk_heads):
        sl = slice(h * HEAD_DIM, (h + 1) * HEAD_DIM)
        w = win_ref[:, sl]
        moved = jnp.where(from_low, pltpu.roll(w, n_mid, 1), pltpu.roll(w, LANES - ROT_HALF, 1))
        win_o[:, sl] = jnp.where(keep, w, moved).astype(BF16)
    rest = slice(n_qk_heads * HEAD_DIM, IN_W)
    win_o[:, rest] = win_ref[:, rest].astype(BF16)


def _cast_specs(weights, n_steps):
    specs = []
    for w in weights:
        rows = -(-w.shape[0] // (n_steps * BF16_TILE_ROWS)) * BF16_TILE_ROWS
        n_blk = w.shape[0] // rows
        assert n_blk * rows == w.shape[0] and n_blk <= n_steps
        specs.append(pl.BlockSpec((rows, w.shape[1]), lambda i, n_blk=n_blk: (jnp.minimum(i, n_blk - 1), 0)))
    return specs


def _prep(positions, w_in):
    n_tok = positions.size
    per_row = LANES // ROT_HALF
    n_steps = 8
    rows = n_tok // per_row // n_steps
    pos = jnp.repeat(positions.reshape(-1).astype(F32), ROT_HALF).reshape(n_tok // per_row, LANES)
    inv_freq = ROPE_THETA ** (-jnp.arange(0, ROT_DIM, 2, dtype=F32) / ROT_DIM)
    invf = jnp.tile(inv_freq, per_row).reshape(1, LANES)
    weights = (w_in,)
    row_blk = lambda w: pl.BlockSpec((w.shape[0] // n_steps, w.shape[1]), lambda i: (i, 0))
    tab = pl.BlockSpec((rows * per_row, LANES), lambda i: (i, 0))
    return pl.pallas_call(
        functools.partial(_prep_kernel, n_qk_heads=(len(DILATED_PAIRS) + 1) * ATT_HEADS),
        grid=(n_steps,),
        in_specs=[pl.BlockSpec((rows, LANES), lambda i: (i, 0)), pl.BlockSpec((1, LANES), lambda i: (0, 0))]
                 + [row_blk(w) for w in weights],
        out_specs=[tab, tab] + [row_blk(w) for w in weights],
        out_shape=[jax.ShapeDtypeStruct((n_tok, LANES), F32)] * 2
                  + [jax.ShapeDtypeStruct(w.shape, BF16) for w in weights],
        compiler_params=pltpu.CompilerParams(dimension_semantics=("parallel",),
                                             vmem_limit_bytes=VMEM_LIMIT),
        name="prep",
    )(pos, invf, *weights)


def _in_proj_kernel(x_ref, g_ref, w_ref, qg_ref, kg_ref, cos_ref, sin_ref, pw_ref, ps_ref,
                    wout_ref, wcq_ref, wckv_ref, wco_ref,
                    tok_ref, view4_ref, view16_ref, y_ref, wout_o, wcq_o, wckv_o, wco_o,
                    xn_ref, tab_ref, pa_ref, pb_ref, hist_ref, lvl_ref, stage_ref, stage4_ref,
                    *, tm, tiles_per_seq):
    step = pl.program_id(0)
    for src, dst in ((wout_ref, wout_o), (wcq_ref, wcq_o), (wckv_ref, wckv_o), (wco_ref, wco_o)):
        dst[...] = src[...].astype(BF16)
    rc = ROW_CHUNK
    n_rc = tm // rc

    def project(p_ref):
        for i in range(n_rc):
            rows = slice(i * rc, (i + 1) * rc)
            xn_ref[rows, :] = _rms(x_ref[rows, :], g_ref[...]).astype(BF16)
        for c in range(IN_W // KV_W):
            cols = slice(c * KV_W, (c + 1) * KV_W)
            p_ref[:, cols] = jnp.dot(xn_ref[...], w_ref[:, cols], preferred_element_type=F32)

    def finish(p_ref):
        for gi, gain_ref in enumerate((qg_ref, kg_ref)):
            gain = gain_ref[...]
            tab_ref[2 * gi] = cos_ref[...] * gain
            tab_ref[2 * gi + 1] = sin_ref[...] * pltpu.roll(gain, ROT_PAIR_LANE, 1)

        def emit(c, gi, part, to_tok, to_v4, to_v16, a):
            for h in range(ATT_HEADS):
                src = slice(c * KV_W + h * HEAD_DIM, c * KV_W + (h + 1) * HEAD_DIM)
                for i in range(n_rc):
                    rows = slice(i * rc, (i + 1) * rc)
                    t = p_ref[rows, src]
                    if gi >= 0:
                        rs = lax.rsqrt(jnp.mean(t * t, axis=-1, keepdims=True) + EPS)
                        t = (t * tab_ref[2 * gi, rows, :]
                             + pltpu.roll(t, ROT_PAIR_LANE, 1) * tab_ref[2 * gi + 1, rows, :]) * rs
                    if to_tok:
                        c0 = part * KV_W + h * HEAD_DIM
                        tok_ref[rows, c0:c0 + HEAD_DIM] = t.astype(BF16)
                    if to_v4 or to_v16:
                        stage_ref[a, h, rows, :] = t
                if not (to_v4 or to_v16):
                    continue
                n4 = tm // 4
                for b in range(4):
                    t4 = stage_ref[a, h, pl.ds(b, n4, stride=4), :]
                    if to_v4:
                        c0 = (part * 4 + b) * KV_W + h * HEAD_DIM
                        view4_ref[:, c0:c0 + HEAD_DIM] = t4.astype(BF16)
                    if to_v16:
                        stage4_ref[a, h, b] = t4
                if to_v16:
                    for b in range(4):
                        for a4 in range(4):
                            c0 = (part * 16 + 4 * a4 + b) * KV_W + h * HEAD_DIM
                            view16_ref[:, c0:c0 + HEAD_DIM] = (
                                stage4_ref[a, h, b, pl.ds(a4, n4 // 4, stride=4), :].astype(BF16))

        emit(3, 1, 1, True, True, True, 0)
        emit(4, -1, 2, True, True, True, 1)
        emit(2, 0, 0, False, False, True, 2)
        emit(1, 0, 0, False, True, False, 3)
        emit(0, 0, 0, True, False, False, 0)

        seq_tile = jnp.maximum(step - 1, 0) % tiles_per_seq
        end = POOL_HALO + tm
        hist_ref[0:POOL_HALO, :] = jnp.where(seq_tile == 0, 0.0, 1.0) * hist_ref[tm:end, :]
        u_cols = slice(IN_W - POOL_W, IN_W)
        hist_ref[POOL_HALO:end, :] = p_ref[:, u_cols]
        row16 = lax.broadcasted_iota(jnp.int32, (2 * SUBLANES, HEAD_DIM), 0)
        for g, w in enumerate(POOL_WINDOWS):
            sl = slice(g * HEAD_DIM, (g + 1) * HEAD_DIM)
            src, start, shift = hist_ref, SUBLANES, 1
            while True:
                cols = sl if src is hist_ref else slice(None)
                tot = src[start:end, cols] + src[start - shift:end - shift, cols]
                shift *= 2
                if shift == w:
                    break
                lvl_ref[g, start:end, :] = tot
                src, start = lvl_ref.at[g], start + SUBLANES
            tot = tot[POOL_HALO - start:, :]
            n_first = jnp.maximum(jnp.minimum(row16 + 1, w), jnp.where(seq_tile == 0, 0, w))
            inv_first = 1.0 / n_first.astype(F32)
            ug = hist_ref[POOL_HALO:end, sl]
            d = jnp.concatenate([tot[:2 * SUBLANES] * inv_first, tot[2 * SUBLANES:] * (1.0 / w)],
                                axis=0) - ug
            yg = jnp.dot(d.astype(BF16), pw_ref[g], preferred_element_type=F32) * ps_ref[:, sl]
            y_ref[:, sl] = yg.astype(BF16)

    @pl.when(step == 0)
    def _():
        pb_ref[...] = jnp.zeros(pb_ref.shape, F32)
        hist_ref[...] = jnp.zeros(hist_ref.shape, F32)

    @pl.when(step % 2 == 0)
    def _():
        project(pa_ref)
        finish(pb_ref)

    @pl.when(step % 2 == 1)
    def _():
        project(pb_ref)
        finish(pa_ref)


def _in_proj(x2, g_mix, w_in, qg, kg, cos, sin, pool_w, pool_scale, later_weights, seq):
    n_tok = x2.shape[0]
    tm = 512
    n_tiles = n_tok // tm
    tiles_per_seq = seq // tm
    ahead = lambda i: (jnp.minimum(i, n_tiles - 1), 0)
    behind = lambda i: (jnp.maximum(i - 1, 0), 0)
    out_dw = [(d, 3 * KV_W) for d in DILATIONS] + [(1, POOL_W)]
    view = lambda d, w: pl.BlockSpec((tm // d, d * w), behind)
    view_sds = lambda d, w: jax.ShapeDtypeStruct((n_tok // d, d * w), BF16)
    cast_specs = _cast_specs(later_weights, n_tiles + 1)
    return pl.pallas_call(
        functools.partial(_in_proj_kernel, tm=tm, tiles_per_seq=tiles_per_seq),
        grid=(n_tiles + 1,),
        in_specs=[pl.BlockSpec((tm, D_MODEL), ahead), _resident((1, D_MODEL)),
                  _resident((D_MODEL, IN_W)), _resident((1, HEAD_DIM)), _resident((1, HEAD_DIM)),
                  pl.BlockSpec((tm, LANES), behind), pl.BlockSpec((tm, LANES), behind),
                  _resident((len(POOL_WINDOWS), HEAD_DIM, HEAD_DIM)), _resident((1, POOL_W))]
                 + cast_specs,
        out_specs=[view(d, w) for d, w in out_dw] + cast_specs,
        out_shape=[view_sds(d, w) for d, w in out_dw]
                  + [jax.ShapeDtypeStruct(w.shape, BF16) for w in later_weights],
        scratch_shapes=[pltpu.VMEM((tm, D_MODEL), BF16),
                        pltpu.VMEM((4, tm, LANES), F32),
                        pltpu.VMEM((tm, IN_W), F32),
                        pltpu.VMEM((tm, IN_W), F32),
                        pltpu.VMEM((POOL_HALO + tm, POOL_W), F32),
                        pltpu.VMEM((len(POOL_WINDOWS), POOL_HALO + tm, HEAD_DIM), F32),
                        pltpu.VMEM((4, ATT_HEADS, tm, HEAD_DIM), F32),
                        pltpu.VMEM((3, ATT_HEADS, 4, tm // 4, HEAD_DIM), F32)],
        compiler_params=pltpu.CompilerParams(dimension_semantics=("arbitrary",),
                                             vmem_limit_bytes=VMEM_LIMIT),
        name="in_proj",
    )(x2, g_mix, w_in, qg, kg, cos, sin, pool_w, pool_scale, *later_weights)


def _dilated_attn_kernel(q_ref, k_ref, kh_ref, v_ref, vh_ref, o_ref, *, tl, rb):
    n_sub = tl // ATT_BLOCK
    qi = lax.broadcasted_iota(jnp.int32, (ATT_BLOCK, 2 * ATT_BLOCK), 0)
    kj = lax.broadcasted_iota(jnp.int32, (ATT_BLOCK, 2 * ATT_BLOCK), 1)
    delta = qi + ATT_BLOCK - kj
    in_band = jnp.logical_and(delta >= 0, delta <= ATT_BLOCK)
    bias = jnp.where(in_band, 0.0, NEG_INF).astype(F32)
    n_missing = jnp.where(pl.program_id(2) == 0, ATT_BLOCK, 0)
    bias_first = jnp.where(kj < n_missing, NEG_INF, bias)
    lane = lax.broadcasted_iota(jnp.int32, (ATT_BLOCK, LANES), 1)
    ones_blk = jnp.ones((2 * ATT_BLOCK, HEAD_DIM), BF16)

    for r in range(rb):
        for i in range(n_sub):
            rows = slice(i * ATT_BLOCK, (i + 1) * ATT_BLOCK)
            rows2 = slice((i - 1) * ATT_BLOCK, (i + 1) * ATT_BLOCK)
            st = jnp.zeros((ATT_BLOCK, LANES), F32)
            for h in range(ATT_HEADS):
                c0 = r * KV_W + h * HEAD_DIM
                sl = slice(c0, c0 + HEAD_DIM)
                if i == 0:
                    k_blk = jnp.concatenate([kh_ref[0, :, sl], k_ref[0, rows, sl]], axis=0)
                    v_blk = jnp.concatenate([vh_ref[0, :, sl], v_ref[0, rows, sl]], axis=0)
                    b = bias_first
                else:
                    k_blk, v_blk, b = k_ref[0, rows2, sl], v_ref[0, rows2, sl], bias
                s = lax.dot_general(q_ref[0, rows, sl], k_blk, (((1,), (1,)), ((), ())),
                                    preferred_element_type=F32) + b
                m = jnp.max(s, axis=-1, keepdims=True).astype(BF16).astype(F32)
                p = jnp.exp(s - m).astype(BF16)
                acc_l = jnp.dot(p, jnp.concatenate([v_blk, ones_blk], axis=1),
                                preferred_element_type=F32)
                o0 = r * ATT_OUT_W + h * HEAD_DIM
                o_ref[0, rows, o0:o0 + HEAD_DIM] = acc_l[:, :HEAD_DIM].astype(BF16)
                l_hi = acc_l[:, HEAD_DIM:].astype(BF16).astype(F32)
                st = jnp.where(lane == h, m, st)
                st = jnp.where(lane == ATT_HEADS + h, l_hi, st)
                st = jnp.where(lane == 2 * ATT_HEADS + h, acc_l[:, HEAD_DIM:] - l_hi, st)
            o_ref[0, rows, r * ATT_OUT_W + KV_W:(r + 1) * ATT_OUT_W] = st.astype(BF16)


def _dilated_attn(qkv, batch, seq, dilation):
    sub_len = seq // dilation
    tl = min(sub_len, ATT_STEP_QUERIES)
    rb = min(dilation, ATT_STEP_QUERIES // tl)
    n_sub = tl // ATT_BLOCK
    n_col = dilation // rb
    qkv3 = qkv.reshape(batch, sub_len, qkv.shape[1])
    cur = lambda part: pl.BlockSpec((1, tl, rb * KV_W), lambda b, r, i: (b, i, part * n_col + r))
    halo = lambda part: pl.BlockSpec(
        (1, ATT_BLOCK, rb * KV_W), lambda b, r, i: (b, jnp.maximum(i * n_sub - 1, 0), part * n_col + r))
    out = pl.pallas_call(
        functools.partial(_dilated_attn_kernel, tl=tl, rb=rb),
        grid=(batch, n_col, sub_len // tl),
        in_specs=[cur(0), cur(1), halo(1), cur(2), halo(2)],
        out_specs=pl.BlockSpec((1, tl, rb * ATT_OUT_W), lambda b, r, i: (b, i, r)),
        out_shape=jax.ShapeDtypeStruct((batch, sub_len, dilation * ATT_OUT_W), BF16),
        compiler_params=pltpu.CompilerParams(
            dimension_semantics=("parallel", "parallel", "parallel"), vmem_limit_bytes=VMEM_LIMIT),
        name=f"dilated_attn_d{dilation}",
    )(qkv3, qkv3, qkv3, qkv3, qkv3)
    return out.reshape(batch * sub_len, dilation * ATT_OUT_W)


def _mem_kv_kernel(mem_ref, g_ref, w_ref, kg_ref, k_ref, v_ref):
    mn = _rms(mem_ref[0], g_ref[...]).astype(BF16)
    kv = jnp.dot(mn, w_ref[...], preferred_element_type=F32)
    for h in range(X_HEADS):
        sl = slice(h * HEAD_DIM, (h + 1) * HEAD_DIM)
        k_ref[0, :, sl] = _rms(kv[:, sl], kg_ref[...]).astype(BF16)
    v_ref[0] = kv[:, X_W:].astype(BF16)


def _mem_kv(mem, g_mem, w_ckv, ckg):
    batch, n_mem, _ = mem.shape
    out = jax.ShapeDtypeStruct((batch, n_mem, X_W), BF16)
    blk = pl.BlockSpec((1, n_mem, X_W), lambda b: (b, 0, 0))
    return pl.pallas_call(
        _mem_kv_kernel,
        grid=(batch,),
        in_specs=[pl.BlockSpec((1, n_mem, D_MODEL), lambda b: (b, 0, 0)), _resident((1, D_MODEL)),
                  _resident((D_MODEL, 2 * X_W)), _resident((1, HEAD_DIM))],
        out_specs=[blk, blk],
        out_shape=[out, out],
        compiler_params=pltpu.CompilerParams(dimension_semantics=("parallel",),
                                             vmem_limit_bytes=VMEM_LIMIT),
        name="mem_kv",
    )(mem, g_mem, w_ckv, ckg)


def _mix_cross_kernel(x_ref, a1_ref, a4_ref, a16_ref, y_ref, wo_ref,
                      gc_ref, wq_ref, qg_ref, km_ref, vm_ref, wc_ref, wgu_ref, wd_ref,
                      h_ref, wgu_o, wd_o, acc_tok, st_tok, cat_a, cat_b, o_ref, *, tm):
    step = pl.program_id(0)
    wgu_o[...] = wgu_ref[...].astype(BF16)
    wd_o[...] = wd_ref[...].astype(BF16)

    def merge(cat_ref):
        for gi, (a_ref, d) in enumerate(((a4_ref, 4), (a16_ref, 16))):
            n = tm // d
            for r in range(d):
                dst = pl.ds(r, n, stride=d)
                st_tok[gi, dst, :] = a_ref[:, r * ATT_OUT_W + KV_W:(r + 1) * ATT_OUT_W].astype(F32)
                for h in range(ATT_HEADS):
                    c0 = r * ATT_OUT_W + h * HEAD_DIM
                    acc_tok[gi, h, dst, :] = a_ref[:, c0:c0 + HEAD_DIM].astype(F32)

        for i in range(tm // MERGE_CHUNK):
            rows = slice(i * MERGE_CHUNK, (i + 1) * MERGE_CHUNK)
            stats = [a1_ref[rows, KV_W:].astype(F32), st_tok[0, rows, :], st_tok[1, rows, :]]
            m_all = jnp.maximum(jnp.maximum(stats[0], stats[1]), stats[2])
            wts = [jnp.exp(s - m_all) for s in stats]
            dens = [s + pltpu.roll(s, LANES - ATT_HEADS, 1) for s in stats]
            for h in range(ATT_HEADS):
                sl = slice(h * HEAD_DIM, (h + 1) * HEAD_DIM)
                accs = (a1_ref[rows, sl].astype(F32), acc_tok[0, h, rows, :], acc_tok[1, h, rows, :])
                num = jnp.zeros((MERGE_CHUNK, HEAD_DIM), F32)
                den = jnp.zeros((MERGE_CHUNK, 1), F32)
                for g in range(len(DILATED_PAIRS)):
                    w = wts[g][:, h:h + 1]
                    num = num + w * accs[g]
                    den = den + w * dens[g][:, ATT_HEADS + h:ATT_HEADS + h + 1]
                cat_ref[rows, sl] = (num / den).astype(BF16)
        cat_ref[:, KV_W:] = y_ref[...]

    def mix(cat_ref):
        n_mem = km_ref.shape[1]
        h1 = x_ref[...] + jnp.dot(cat_ref[...], wo_ref[...], preferred_element_type=F32)
        hn = _rms(h1, gc_ref[...]).astype(BF16)
        qc = jnp.dot(hn, wq_ref[...], preferred_element_type=F32)
        for h in range(X_HEADS):
            sl = slice(h * HEAD_DIM, (h + 1) * HEAD_DIM)
            qh = _rms(qc[:, sl], qg_ref[...]).astype(BF16)
            s = lax.dot_general(qh, km_ref[0, :, sl], (((1,), (1,)), ((), ())),
                                preferred_element_type=F32)
            p = jnp.exp(s - jnp.max(s, axis=-1, keepdims=True)).astype(BF16)
            v_ones = jnp.concatenate([vm_ref[0, :, sl], jnp.ones((n_mem, HEAD_DIM), BF16)], axis=1)
            o_l = jnp.dot(p, v_ones, preferred_element_type=F32)
            o_ref[:, sl] = (o_l[:, :HEAD_DIM] / o_l[:, HEAD_DIM:]).astype(BF16)
        h_ref[...] = h1 + jnp.dot(o_ref[...], wc_ref[...], preferred_element_type=F32)

    @pl.when(step == 0)
    def _():
        cat_b[...] = jnp.zeros(cat_b.shape, BF16)

    @pl.when(step % 2 == 0)
    def _():
        merge(cat_a)
        mix(cat_b)

    @pl.when(step % 2 == 1)
    def _():
        merge(cat_b)
        mix(cat_a)


def _mix_cross(x2, branches, y, w_out, g_cross, w_cq, cqg, k_mem, v_mem, w_co, w_gate_up, w_down, seq):
    n_tok = x2.shape[0]
    tm = 512
    n_tiles = n_tok // tm
    tiles_per_seq = seq // tm
    n_mem = k_mem.shape[1]
    ahead = lambda i: (jnp.minimum(i, n_tiles - 1), 0)
    behind = lambda i: (jnp.maximum(i - 1, 0), 0)
    view = lambda d, w: pl.BlockSpec((tm // d, d * w), ahead)
    mem_blk = pl.BlockSpec((1, n_mem, X_W), lambda i: (jnp.maximum(i - 1, 0) // tiles_per_seq, 0, 0))

    def cast_blk(w, rows):
        n_blk = w.shape[0] // rows
        assert n_blk * rows == w.shape[0] and n_blk <= n_tiles + 1
        return pl.BlockSpec((rows, w.shape[1]), lambda i: (jnp.minimum(i, n_blk - 1), 0))

    cast_specs = [cast_blk(w_gate_up, 32), cast_blk(w_down, 128)]
    return pl.pallas_call(
        functools.partial(_mix_cross_kernel, tm=tm),
        grid=(n_tiles + 1,),
        in_specs=[pl.BlockSpec((tm, D_MODEL), behind)] + [view(d, ATT_OUT_W) for d in DILATIONS]
                 + [view(1, POOL_W), _resident((KV_W + POOL_W, D_MODEL)), _resident((1, D_MODEL)),
                    _resident((D_MODEL, X_W)), _resident((1, HEAD_DIM)), mem_blk, mem_blk,
                    _resident((X_W, D_MODEL))] + cast_specs,
        out_specs=[pl.BlockSpec((tm, D_MODEL), behind)] + cast_specs,
        out_shape=[jax.ShapeDtypeStruct((n_tok, D_MODEL), F32),
                   jax.ShapeDtypeStruct(w_gate_up.shape, BF16),
                   jax.ShapeDtypeStruct(w_down.shape, BF16)],
        scratch_shapes=[pltpu.VMEM((2, ATT_HEADS, tm, HEAD_DIM), F32),
                        pltpu.VMEM((2, tm, LANES), F32),
                        pltpu.VMEM((tm, KV_W + POOL_W), BF16),
                        pltpu.VMEM((tm, KV_W + POOL_W), BF16),
                        pltpu.VMEM((tm, X_W), BF16)],
        compiler_params=pltpu.CompilerParams(dimension_semantics=("arbitrary",),
                                             vmem_limit_bytes=VMEM_LIMIT),
        name="mix_cross",
    )(x2, *branches, y, w_out, g_cross, w_cq, cqg, k_mem, v_mem, w_co, w_gate_up, w_down)


def _ffn_kernel(h_ref, g_ref, wgu_ref, wd_ref, o_ref):
    h = h_ref[...]
    hn = _rms(h, g_ref[...]).astype(BF16)
    gate = jnp.dot(hn, wgu_ref[:, :D_FF], preferred_element_type=F32)
    up = jnp.dot(hn, wgu_ref[:, D_FF:], preferred_element_type=F32)
    act = (gate * jax.nn.sigmoid(gate) * up).astype(BF16)
    o_ref[...] = h + jnp.dot(act, wd_ref[...], preferred_element_type=F32)


def _ffn(h2, g_ffn, w_gate_up, w_down):
    n_tok = h2.shape[0]
    tm = 512
    tok = pl.BlockSpec((tm, D_MODEL), lambda i: (i, 0))
    return pl.pallas_call(
        _ffn_kernel,
        grid=(n_tok // tm,),
        in_specs=[tok, _resident((1, D_MODEL)), _resident((D_MODEL, 2 * D_FF)),
                  _resident((D_FF, D_MODEL))],
        out_specs=tok,
        out_shape=jax.ShapeDtypeStruct((n_tok, D_MODEL), F32),
        compiler_params=pltpu.CompilerParams(dimension_semantics=("parallel",),
                                             vmem_limit_bytes=VMEM_LIMIT),
        name="ffn",
    )(h2, g_ffn, w_gate_up, w_down)


def kernel(x, mem, positions, mix_norm_g, w_in, q_norm_g, k_norm_g, pool_w, pool_scale, w_out,
           cross_norm_g, mem_norm_g, w_cq, w_ckv, cq_norm_g, ck_norm_g, w_co,
           ffn_norm_g, w_gate_up, w_down):
    batch, seq, _ = x.shape
    depth = w_in.shape[0]
    scale = HEAD_DIM ** -0.5
    row = lambda a: a.reshape(1, -1)
    order = _rotary_lane_order()
    h = x.reshape(batch * seq, D_MODEL)
    for layer in range(depth):
        cos, sin, w_in_l = _prep(positions, w_in[layer])
        *qkv_views, y, w_out_l, w_cq_l, w_ckv_l, w_co_l = _in_proj(
            h, row(mix_norm_g[layer]), w_in_l, row(q_norm_g[layer][order] * scale),
            row(k_norm_g[layer][order]), cos, sin, pool_w[layer].astype(BF16),
            row(pool_scale[layer]), (w_out[layer], w_cq[layer], w_ckv[layer], w_co[layer]), seq)
        branches = [_dilated_attn(qkv, batch, seq, d) for qkv, d in zip(qkv_views, DILATIONS)]
        k_mem, v_mem = _mem_kv(mem, row(mem_norm_g[layer]), w_ckv_l, row(ck_norm_g[layer]))
        h, w_gu_l, w_down_l = _mix_cross(
            h, branches, y, w_out_l, row(cross_norm_g[layer]), w_cq_l,
            row(cq_norm_g[layer] * scale), k_mem, v_mem, w_co_l, w_gate_up[layer], w_down[layer], seq)
        h = _ffn(h, row(ffn_norm_g[layer]), w_gu_l, w_down_l)
    return h.reshape(batch, seq, D_MODEL)
```

```python
import functools

import jax
import jax.numpy as jnp
from jax import lax
from jax.experimental import pallas as pl
from jax.experimental.pallas import tpu as pltpu

D_MODEL = 1024
HEAD_DIM = 128
ATT_HEADS = 4
DILATED_PAIRS = ((128, 1), (512, 4), (2048, 16))
DILATIONS = tuple(d for _, d in DILATED_PAIRS)
KV_W = ATT_HEADS * HEAD_DIM
POOL_WINDOWS = (2, 4, 8, 16)
POOL_W = len(POOL_WINDOWS) * HEAD_DIM
IN_W = (len(DILATED_PAIRS) + 2) * KV_W + POOL_W
ROT_DIM = HEAD_DIM // 4
ROT_HALF = ROT_DIM // 2
ROPE_THETA = 500000.0
X_HEADS = 4
X_W = X_HEADS * HEAD_DIM
D_FF = 2816
EPS = 1e-6
NEG_INF = -1e30
ATT_BLOCK = 128
ATT_STEP_QUERIES = 2048
LANES = 128
ATT_OUT_W = KV_W + LANES
SUBLANES = 8
BF16_TILE_ROWS = 2 * SUBLANES
ROT_PAIR_LANE = LANES // 2
ROW_CHUNK = 256
MERGE_CHUNK = 512
POOL_HALO = 32

F32 = jnp.float32
BF16 = jnp.bfloat16

VMEM_LIMIT = 56 * 1024 * 1024


def _rms(x, g):
    ms = jnp.mean(x * x, axis=-1, keepdims=True)
    return x * lax.rsqrt(ms + EPS) * g


def _resident(shape):
    nd = len(shape)
    return pl.BlockSpec(shape, lambda *_: (0,) * nd, pipeline_mode=pl.Buffered(1))


def _to_rotary_lanes(w):
    n_mid = ROT_PAIR_LANE - ROT_HALF
    lane = lax.broadcasted_iota(jnp.int32, w.shape, w.ndim - 1)
    keep = jnp.logical_or(lane < ROT_HALF, lane >= ROT_DIM + n_mid)
    from_low = jnp.logical_and(lane >= ROT_PAIR_LANE, lane < ROT_PAIR_LANE + ROT_HALF)
    moved = jnp.where(from_low, pltpu.roll(w, n_mid, w.ndim - 1),
                      pltpu.roll(w, LANES - ROT_HALF, w.ndim - 1))
    return jnp.where(keep, w, moved)


def _prep_kernel(pos_ref, invf_ref, win_ref, ctab_ref, stab_ref, win_o, *, n_qk_heads):
    ang = pos_ref[...] * invf_ref[...]
    cos = jnp.cos(ang)
    sin = jnp.sin(ang)
    n_rows = ang.shape[0]
    per_row = LANES // ROT_HALF
    lane = lax.broadcasted_iota(jnp.int32, (n_rows, LANES), 1)
    first = lane < ROT_HALF
    second = jnp.logical_and(lane >= ROT_PAIR_LANE, lane < ROT_PAIR_LANE + ROT_HALF)
    for j in range(per_row):
        shift = (LANES - ROT_HALF * j) % LANES
        cj = pltpu.roll(cos, shift, 1) if shift else cos
        sj = pltpu.roll(sin, shift, 1) if shift else sin
        dst = pl.ds(j, n_rows, stride=per_row)
        ctab_ref[dst, :] = jnp.where(first, cj, jnp.where(second, pltpu.roll(cj, ROT_PAIR_LANE, 1), 1.0))
        stab_ref[dst, :] = jnp.where(first, -sj, jnp.where(second, pltpu.roll(sj, ROT_PAIR_LANE, 1), 0.0))

    for h in range(n_qk_heads):
        sl = slice(h * HEAD_DIM, (h + 1) * HEAD_DIM)
        win_o[:, sl] = _to_rotary_lanes(win_ref[:, sl]).astype(BF16)
    rest = slice(n_qk_heads * HEAD_DIM, IN_W)
    win_o[:, rest] = win_ref[:, rest].astype(BF16)


def _cast_specs(weights, n_steps):
    specs = []
    for w in weights:
        rows = -(-w.shape[0] // (n_steps * BF16_TILE_ROWS)) * BF16_TILE_ROWS
        n_blk = w.shape[0] // rows
        assert n_blk * rows == w.shape[0] and n_blk <= n_steps
        specs.append(pl.BlockSpec((rows, w.shape[1]), lambda i, n_blk=n_blk: (jnp.minimum(i, n_blk - 1), 0)))
    return specs


def _prep(positions, w_in):
    n_tok = positions.size
    per_row = LANES // ROT_HALF
    n_steps = 16
    rows = n_tok // per_row // n_steps
    pos = jnp.broadcast_to(positions.reshape(-1, per_row, 1).astype(F32),
                           (n_tok // per_row, per_row, ROT_HALF)).reshape(n_tok // per_row, LANES)
    inv_freq = ROPE_THETA ** (-jnp.arange(0, ROT_DIM, 2, dtype=F32) / ROT_DIM)
    invf = jnp.tile(inv_freq, per_row).reshape(1, LANES)
    weights = (w_in,)
    row_blk = lambda w: pl.BlockSpec((w.shape[0] // n_steps, w.shape[1]), lambda i: (i, 0))
    tab = pl.BlockSpec((rows * per_row, LANES), lambda i: (i, 0))
    return pl.pallas_call(
        functools.partial(_prep_kernel, n_qk_heads=(len(DILATED_PAIRS) + 1) * ATT_HEADS),
        grid=(n_steps,),
        in_specs=[pl.BlockSpec((rows, LANES), lambda i: (i, 0)), pl.BlockSpec((1, LANES), lambda i: (0, 0))]
                 + [row_blk(w) for w in weights],
        out_specs=[tab, tab] + [row_blk(w) for w in weights],
        out_shape=[jax.ShapeDtypeStruct((n_tok, LANES), F32)] * 2
                  + [jax.ShapeDtypeStruct(w.shape, BF16) for w in weights],
        compiler_params=pltpu.CompilerParams(dimension_semantics=("parallel",),
                                             vmem_limit_bytes=VMEM_LIMIT),
        name="prep",
    )(pos, invf, *weights)


def _in_proj_kernel(x_ref, g_ref, w_ref, qg_ref, kg_ref, cos_ref, sin_ref, pw_ref, ps_ref,
                    wout_ref, wcq_ref, wckv_ref, wco_ref,
                    tok_ref, view4_ref, view16_ref, y_ref, wout_o, wcq_o, wckv_o, wco_o,
                    xn_ref, tab_ref, pa_ref, pb_ref, hist_ref, lvl_ref, stage_ref, stage4_ref,
                    *, tm, tiles_per_seq):
    step = pl.program_id(0)
    for src, dst in ((wout_ref, wout_o), (wcq_ref, wcq_o), (wckv_ref, wckv_o), (wco_ref, wco_o)):
        dst[...] = src[...].astype(BF16)
    rc = ROW_CHUNK
    n_rc = tm // rc

    def project(p_ref):
        for i in range(n_rc):
            rows = slice(i * rc, (i + 1) * rc)
            xn_ref[rows, :] = _rms(x_ref[rows, :], g_ref[...]).astype(BF16)
        for c in range(IN_W // KV_W):
            cols = slice(c * KV_W, (c + 1) * KV_W)
            p_ref[:, cols] = jnp.dot(xn_ref[...], w_ref[:, cols], preferred_element_type=F32)

    def finish(p_ref):
        for gi, (gain_ref, scale) in enumerate(((qg_ref, HEAD_DIM ** -0.5), (kg_ref, 1.0))):
            gain = _to_rotary_lanes(gain_ref[...] * scale)
            tab_ref[2 * gi] = cos_ref[...] * gain
            tab_ref[2 * gi + 1] = sin_ref[...] * pltpu.roll(gain, ROT_PAIR_LANE, 1)

        def emit(c, gi, part, to_tok, to_v4, to_v16, a):
            for h in range(ATT_HEADS):
                src = slice(c * KV_W + h * HEAD_DIM, c * KV_W + (h + 1) * HEAD_DIM)
                for i in range(n_rc):
                    rows = slice(i * rc, (i + 1) * rc)
                    t = p_ref[rows, src]
                    if gi >= 0:
                        rs = lax.rsqrt(jnp.mean(t * t, axis=-1, keepdims=True) + EPS)
                        t = (t * tab_ref[2 * gi, rows, :]
                             + pltpu.roll(t, ROT_PAIR_LANE, 1) * tab_ref[2 * gi + 1, rows, :]) * rs
                    if to_tok:
                        c0 = part * KV_W + h * HEAD_DIM
                        tok_ref[rows, c0:c0 + HEAD_DIM] = t.astype(BF16)
                    if to_v4 or to_v16:
                        stage_ref[a, h, rows, :] = t
                if not (to_v4 or to_v16):
                    continue
                n4 = tm // 4
                for b in range(4):
                    t4 = stage_ref[a, h, pl.ds(b, n4, stride=4), :]
                    if to_v4:
                        c0 = (part * 4 + b) * KV_W + h * HEAD_DIM
                        view4_ref[:, c0:c0 + HEAD_DIM] = t4.astype(BF16)
                    if to_v16:
                        stage4_ref[a, h, b] = t4
                if to_v16:
                    for b in range(4):
                        for a4 in range(4):
                            c0 = (part * 16 + 4 * a4 + b) * KV_W + h * HEAD_DIM
                            view16_ref[:, c0:c0 + HEAD_DIM] = (
                                stage4_ref[a, h, b, pl.ds(a4, n4 // 4, stride=4), :].astype(BF16))

        emit(3, 1, 1, True, True, True, 0)
        emit(4, -1, 2, True, True, True, 1)
        emit(2, 0, 0, False, False, True, 2)
        emit(1, 0, 0, False, True, False, 3)
        emit(0, 0, 0, True, False, False, 0)

        seq_tile = jnp.maximum(step - 1, 0) % tiles_per_seq
        end = POOL_HALO + tm
        hist_ref[0:POOL_HALO, :] = jnp.where(seq_tile == 0, 0.0, 1.0) * hist_ref[tm:end, :]
        u_cols = slice(IN_W - POOL_W, IN_W)
        hist_ref[POOL_HALO:end, :] = p_ref[:, u_cols]
        row16 = lax.broadcasted_iota(jnp.int32, (2 * SUBLANES, HEAD_DIM), 0)
        for g, w in enumerate(POOL_WINDOWS):
            sl = slice(g * HEAD_DIM, (g + 1) * HEAD_DIM)
            src, start, shift = hist_ref, SUBLANES, 1
            while True:
                cols = sl if src is hist_ref else slice(None)
                tot = src[start:end, cols] + src[start - shift:end - shift, cols]
                shift *= 2
                if shift == w:
                    break
                lvl_ref[g, start:end, :] = tot
                src, start = lvl_ref.at[g], start + SUBLANES
            tot = tot[POOL_HALO - start:, :]
            n_first = jnp.maximum(jnp.minimum(row16 + 1, w), jnp.where(seq_tile == 0, 0, w))
            inv_first = 1.0 / n_first.astype(F32)
            ug = hist_ref[POOL_HALO:end, sl]
            d = jnp.concatenate([tot[:2 * SUBLANES] * inv_first, tot[2 * SUBLANES:] * (1.0 / w)],
                                axis=0) - ug
            yg = jnp.dot(d.astype(BF16), pw_ref[g].astype(BF16),
                         preferred_element_type=F32) * ps_ref[:, sl]
            y_ref[:, sl] = yg.astype(BF16)

    @pl.when(step == 0)
    def _():
        pb_ref[...] = jnp.zeros(pb_ref.shape, F32)
        hist_ref[...] = jnp.zeros(hist_ref.shape, F32)

    @pl.when(step % 2 == 0)
    def _():
        project(pa_ref)
        finish(pb_ref)

    @pl.when(step % 2 == 1)
    def _():
        project(pb_ref)
        finish(pa_ref)


def _in_proj(x2, g_mix, w_in, qg, kg, cos, sin, pool_w, pool_scale, later_weights, seq):
    n_tok = x2.shape[0]
    tm = 512
    n_tiles = n_tok // tm
    tiles_per_seq = seq // tm
    ahead = lambda i: (jnp.minimum(i, n_tiles - 1), 0)
    behind = lambda i: (jnp.maximum(i - 1, 0), 0)
    out_dw = [(d, 3 * KV_W) for d in DILATIONS] + [(1, POOL_W)]
    view = lambda d, w: pl.BlockSpec((tm // d, d * w), behind)
    view_sds = lambda d, w: jax.ShapeDtypeStruct((n_tok // d, d * w), BF16)
    cast_specs = _cast_specs(later_weights, n_tiles + 1)
    return pl.pallas_call(
        functools.partial(_in_proj_kernel, tm=tm, tiles_per_seq=tiles_per_seq),
        grid=(n_tiles + 1,),
        in_specs=[pl.BlockSpec((tm, D_MODEL), ahead), _resident((1, D_MODEL)),
                  _resident((D_MODEL, IN_W)), _resident((1, HEAD_DIM)), _resident((1, HEAD_DIM)),
                  pl.BlockSpec((tm, LANES), behind), pl.BlockSpec((tm, LANES), behind),
                  _resident((len(POOL_WINDOWS), HEAD_DIM, HEAD_DIM)), _resident((1, POOL_W))]
                 + cast_specs,
        out_specs=[view(d, w) for d, w in out_dw] + cast_specs,
        out_shape=[view_sds(d, w) for d, w in out_dw]
                  + [jax.ShapeDtypeStruct(w.shape, BF16) for w in later_weights],
        scratch_shapes=[pltpu.VMEM((tm, D_MODEL), BF16),
                        pltpu.VMEM((4, tm, LANES), F32),
                        pltpu.VMEM((tm, IN_W), F32),
                        pltpu.VMEM((tm, IN_W), F32),
                        pltpu.VMEM((POOL_HALO + tm, POOL_W), F32),
                        pltpu.VMEM((len(POOL_WINDOWS), POOL_HALO + tm, HEAD_DIM), F32),
                        pltpu.VMEM((4, ATT_HEADS, tm, HEAD_DIM), F32),
                        pltpu.VMEM((3, ATT_HEADS, 4, tm // 4, HEAD_DIM), F32)],
        compiler_params=pltpu.CompilerParams(dimension_semantics=("arbitrary",),
                                             vmem_limit_bytes=VMEM_LIMIT),
        name="in_proj",
    )(x2, g_mix, w_in, qg, kg, cos, sin, pool_w, pool_scale, *later_weights)


def _dilated_attn_kernel(q_ref, k_ref, kh_ref, v_ref, vh_ref, o_ref, *, tl, rb):
    n_sub = tl // ATT_BLOCK
    qi = lax.broadcasted_iota(jnp.int32, (ATT_BLOCK, 2 * ATT_BLOCK), 0)
    kj = lax.broadcasted_iota(jnp.int32, (ATT_BLOCK, 2 * ATT_BLOCK), 1)
    delta = qi + ATT_BLOCK - kj
    in_band = jnp.logical_and(delta >= 0, delta <= ATT_BLOCK)
    bias = jnp.where(in_band, 0.0, NEG_INF).astype(F32)
    n_missing = jnp.where(pl.program_id(2) == 0, ATT_BLOCK, 0)
    bias_first = jnp.where(kj < n_missing, NEG_INF, bias)
    lane = lax.broadcasted_iota(jnp.int32, (ATT_BLOCK, LANES), 1)
    ones_blk = jnp.ones((2 * ATT_BLOCK, HEAD_DIM), BF16)

    for r in range(rb):
        for i in range(n_sub):
            rows = slice(i * ATT_BLOCK, (i + 1) * ATT_BLOCK)
            rows2 = slice((i - 1) * ATT_BLOCK, (i + 1) * ATT_BLOCK)
            st = jnp.zeros((ATT_BLOCK, LANES), F32)
            for h in range(ATT_HEADS):
                c0 = r * KV_W + h * HEAD_DIM
                sl = slice(c0, c0 + HEAD_DIM)
                if i == 0:
                    k_blk = jnp.concatenate([kh_ref[0, :, sl], k_ref[0, rows, sl]], axis=0)
                    v_blk = jnp.concatenate([vh_ref[0, :, sl], v_ref[0, rows, sl]], axis=0)
                    b = bias_first
                else:
                    k_blk, v_blk, b = k_ref[0, rows2, sl], v_ref[0, rows2, sl], bias
                s = lax.dot_general(q_ref[0, rows, sl], k_blk, (((1,), (1,)), ((), ())),
                                    preferred_element_type=F32) + b
                m = jnp.max(s, axis=-1, keepdims=True).astype(BF16).astype(F32)
                p = jnp.exp(s - m).astype(BF16)
                acc_l = jnp.dot(p, jnp.concatenate([v_blk, ones_blk], axis=1),
                                preferred_element_type=F32)
                o0 = r * ATT_OUT_W + h * HEAD_DIM
                o_ref[0, rows, o0:o0 + HEAD_DIM] = acc_l[:, :HEAD_DIM].astype(BF16)
                l_hi = acc_l[:, HEAD_DIM:].astype(BF16).astype(F32)
                st = jnp.where(lane == h, m, st)
                st = jnp.where(lane == ATT_HEADS + h, l_hi, st)
                st = jnp.where(lane == 2 * ATT_HEADS + h, acc_l[:, HEAD_DIM:] - l_hi, st)
            o_ref[0, rows, r * ATT_OUT_W + KV_W:(r + 1) * ATT_OUT_W] = st.astype(BF16)


def _dilated_attn(qkv, batch, seq, dilation):
    sub_len = seq // dilation
    tl = min(sub_len, ATT_STEP_QUERIES)
    rb = min(dilation, ATT_STEP_QUERIES // tl)
    n_sub = tl // ATT_BLOCK
    n_col = dilation // rb
    qkv3 = qkv.reshape(batch, sub_len, qkv.shape[1])
    cur = lambda part: pl.BlockSpec((1, tl, rb * KV_W), lambda b, r, i: (b, i, part * n_col + r))
    halo = lambda part: pl.BlockSpec(
        (1, ATT_BLOCK, rb * KV_W), lambda b, r, i: (b, jnp.maximum(i * n_sub - 1, 0), part * n_col + r))
    out = pl.pallas_call(
        functools.partial(_dilated_attn_kernel, tl=tl, rb=rb),
        grid=(batch, n_col, sub_len // tl),
        in_specs=[cur(0), cur(1), halo(1), cur(2), halo(2)],
        out_specs=pl.BlockSpec((1, tl, rb * ATT_OUT_W), lambda b, r, i: (b, i, r)),
        out_shape=jax.ShapeDtypeStruct((batch, sub_len, dilation * ATT_OUT_W), BF16),
        compiler_params=pltpu.CompilerParams(
            dimension_semantics=("parallel", "parallel", "parallel"), vmem_limit_bytes=VMEM_LIMIT),
        name=f"dilated_attn_d{dilation}",
    )(qkv3, qkv3, qkv3, qkv3, qkv3)
    return out.reshape(batch * sub_len, dilation * ATT_OUT_W)


def _mem_kv_kernel(mem_ref, g_ref, w_ref, kg_ref, k_ref, v_ref):
    mn = _rms(mem_ref[0], g_ref[...]).astype(BF16)
    kv = jnp.dot(mn, w_ref[...], preferred_element_type=F32)
    for h in range(X_HEADS):
        sl = slice(h * HEAD_DIM, (h + 1) * HEAD_DIM)
        k_ref[0, :, sl] = _rms(kv[:, sl], kg_ref[...]).astype(BF16)
    v_ref[0] = kv[:, X_W:].astype(BF16)


def _mem_kv(mem, g_mem, w_ckv, ckg):
    batch, n_mem, _ = mem.shape
    out = jax.ShapeDtypeStruct((batch, n_mem, X_W), BF16)
    blk = pl.BlockSpec((1, n_mem, X_W), lambda b: (b, 0, 0))
    return pl.pallas_call(
        _mem_kv_kernel,
        grid=(batch,),
        in_specs=[pl.BlockSpec((1, n_mem, D_MODEL), lambda b: (b, 0, 0)), _resident((1, D_MODEL)),
                  _resident((D_MODEL, 2 * X_W)), _resident((1, HEAD_DIM))],
        out_specs=[blk, blk],
        out_shape=[out, out],
        compiler_params=pltpu.CompilerParams(dimension_semantics=("parallel",),
                                             vmem_limit_bytes=VMEM_LIMIT),
        name="mem_kv",
    )(mem, g_mem, w_ckv, ckg)


def _mix_cross_kernel(x_ref, a1_ref, a4_ref, a16_ref, y_ref, wo_ref,
                      gc_ref, wq_ref, qg_ref, km_ref, vm_ref, wc_ref, wgu_ref, wd_ref,
                      h_ref, wgu_o, wd_o, acc_tok, st_tok, cat_a, cat_b, o_ref, *, tm):
    step = pl.program_id(0)
    wgu_o[...] = wgu_ref[...].astype(BF16)
    wd_o[...] = wd_ref[...].astype(BF16)

    def merge(cat_ref):
        for gi, (a_ref, d) in enumerate(((a4_ref, 4), (a16_ref, 16))):
            n = tm // d
            for r in range(d):
                dst = pl.ds(r, n, stride=d)
                st_tok[gi, dst, :] = a_ref[:, r * ATT_OUT_W + KV_W:(r + 1) * ATT_OUT_W].astype(F32)
                for h in range(ATT_HEADS):
                    c0 = r * ATT_OUT_W + h * HEAD_DIM
                    acc_tok[gi, h, dst, :] = a_ref[:, c0:c0 + HEAD_DIM].astype(F32)

        for i in range(tm // MERGE_CHUNK):
            rows = slice(i * MERGE_CHUNK, (i + 1) * MERGE_CHUNK)
            stats = [a1_ref[rows, KV_W:].astype(F32), st_tok[0, rows, :], st_tok[1, rows, :]]
            m_all = jnp.maximum(jnp.maximum(stats[0], stats[1]), stats[2])
            wts = [jnp.exp(s - m_all) for s in stats]
            dens = [s + pltpu.roll(s, LANES - ATT_HEADS, 1) for s in stats]
            for h in range(ATT_HEADS):
                sl = slice(h * HEAD_DIM, (h + 1) * HEAD_DIM)
                accs = (a1_ref[rows, sl].astype(F32), acc_tok[0, h, rows, :], acc_tok[1, h, rows, :])
                num = jnp.zeros((MERGE_CHUNK, HEAD_DIM), F32)
                den = jnp.zeros((MERGE_CHUNK, 1), F32)
                for g in range(len(DILATED_PAIRS)):
                    w = wts[g][:, h:h + 1]
                    num = num + w * accs[g]
                    den = den + w * dens[g][:, ATT_HEADS + h:ATT_HEADS + h + 1]
                cat_ref[rows, sl] = (num / den).astype(BF16)
        cat_ref[:, KV_W:] = y_ref[...]

    def mix(cat_ref):
        n_mem = km_ref.shape[1]
        h1 = x_ref[...] + jnp.dot(cat_ref[...], wo_ref[...], preferred_element_type=F32)
        hn = _rms(h1, gc_ref[...]).astype(BF16)
        qc = jnp.dot(hn, wq_ref[...], preferred_element_type=F32)
        for h in range(X_HEADS):
            sl = slice(h * HEAD_DIM, (h + 1) * HEAD_DIM)
            qh = _rms(qc[:, sl], qg_ref[...] * HEAD_DIM ** -0.5).astype(BF16)
            s = lax.dot_general(qh, km_ref[0, :, sl], (((1,), (1,)), ((), ())),
                                preferred_element_type=F32)
            p = jnp.exp(s - jnp.max(s, axis=-1, keepdims=True)).astype(BF16)
            v_ones = jnp.concatenate([vm_ref[0, :, sl], jnp.ones((n_mem, HEAD_DIM), BF16)], axis=1)
            o_l = jnp.dot(p, v_ones, preferred_element_type=F32)
            o_ref[:, sl] = (o_l[:, :HEAD_DIM] / o_l[:, HEAD_DIM:]).astype(BF16)
        h_ref[...] = h1 + jnp.dot(o_ref[...], wc_ref[...], preferred_element_type=F32)

    @pl.when(step == 0)
    def _():
        cat_b[...] = jnp.zeros(cat_b.shape, BF16)

    @pl.when(step % 2 == 0)
    def _():
        merge(cat_a)
        mix(cat_b)

    @pl.when(step % 2 == 1)
    def _():
        merge(cat_b)
        mix(cat_a)


def _mix_cross(x2, branches, y, w_out, g_cross, w_cq, cqg, k_mem, v_mem, w_co, w_gate_up, w_down, seq):
    n_tok = x2.shape[0]
    tm = 512
    n_tiles = n_tok // tm
    tiles_per_seq = seq // tm
    n_mem = k_mem.shape[1]
    ahead = lambda i: (jnp.minimum(i, n_tiles - 1), 0)
    behind = lambda i: (jnp.maximum(i - 1, 0), 0)
    view = lambda d, w: pl.BlockSpec((tm // d, d * w), ahead)
    mem_blk = pl.BlockSpec((1, n_mem, X_W), lambda i: (jnp.maximum(i - 1, 0) // tiles_per_seq, 0, 0))

    def cast_blk(w, rows):
        n_blk = w.shape[0] // rows
        assert n_blk * rows == w.shape[0] and n_blk <= n_tiles + 1
        return pl.BlockSpec((rows, w.shape[1]), lambda i: (jnp.minimum(i, n_blk - 1), 0))

    cast_specs = [cast_blk(w_gate_up, 32), cast_blk(w_down, 128)]
    return pl.pallas_call(
        functools.partial(_mix_cross_kernel, tm=tm),
        grid=(n_tiles + 1,),
        in_specs=[pl.BlockSpec((tm, D_MODEL), behind)] + [view(d, ATT_OUT_W) for d in DILATIONS]
                 + [view(1, POOL_W), _resident((KV_W + POOL_W, D_MODEL)), _resident((1, D_MODEL)),
                    _resident((D_MODEL, X_W)), _resident((1, HEAD_DIM)), mem_blk, mem_blk,
                    _resident((X_W, D_MODEL))] + cast_specs,
        out_specs=[pl.BlockSpec((tm, D_MODEL), behind)] + cast_specs,
        out_shape=[jax.ShapeDtypeStruct((n_tok, D_MODEL), F32),
                   jax.ShapeDtypeStruct(w_gate_up.shape, BF16),
                   jax.ShapeDtypeStruct(w_down.shape, BF16)],
        scratch_shapes=[pltpu.VMEM((2, ATT_HEADS, tm, HEAD_DIM), F32),
                        pltpu.VMEM((2, tm, LANES), F32),
                        pltpu.VMEM((tm, KV_W + POOL_W), BF16),
                        pltpu.VMEM((tm, KV_W + POOL_W), BF16),
                        pltpu.VMEM((tm, X_W), BF16)],
        compiler_params=pltpu.CompilerParams(dimension_semantics=("arbitrary",),
                                             vmem_limit_bytes=VMEM_LIMIT),
        name="mix_cross",
    )(x2, *branches, y, w_out, g_cross, w_cq, cqg, k_mem, v_mem, w_co, w_gate_up, w_down)


def _ffn_kernel(h_ref, g_ref, wgu_ref, wd_ref, o_ref):
    h = h_ref[...]
    hn = _rms(h, g_ref[...]).astype(BF16)
    gate = jnp.dot(hn, wgu_ref[:, :D_FF], preferred_element_type=F32)
    up = jnp.dot(hn, wgu_ref[:, D_FF:], preferred_element_type=F32)
    act = (gate * jax.nn.sigmoid(gate) * up).astype(BF16)
    o_ref[...] = h + jnp.dot(act, wd_ref[...], preferred_element_type=F32)


def _ffn(h2, g_ffn, w_gate_up, w_down):
    n_tok = h2.shape[0]
    tm = 512
    tok = pl.BlockSpec((tm, D_MODEL), lambda i: (i, 0))
    return pl.pallas_call(
        _ffn_kernel,
        grid=(n_tok // tm,),
        in_specs=[tok, _resident((1, D_MODEL)), _resident((D_MODEL, 2 * D_FF)),
                  _resident((D_FF, D_MODEL))],
        out_specs=tok,
        out_shape=jax.ShapeDtypeStruct((n_tok, D_MODEL), F32),
        compiler_params=pltpu.CompilerParams(dimension_semantics=("parallel",),
                                             vmem_limit_bytes=VMEM_LIMIT),
        name="ffn",
    )(h2, g_ffn, w_gate_up, w_down)


def kernel(x, mem, positions, mix_norm_g, w_in, q_norm_g, k_norm_g, pool_w, pool_scale, w_out,
           cross_norm_g, mem_norm_g, w_cq, w_ckv, cq_norm_g, ck_norm_g, w_co,
           ffn_norm_g, w_gate_up, w_down):
    batch, seq, _ = x.shape
    depth = w_in.shape[0]
    row = lambda a: a.reshape(1, -1)
    h = x.reshape(batch * seq, D_MODEL)
    for layer in range(depth):
        cos, sin, w_in_l = _prep(positions, w_in[layer])
        *qkv_views, y, w_out_l, w_cq_l, w_ckv_l, w_co_l = _in_proj(
            h, row(mix_norm_g[layer]), w_in_l, row(q_norm_g[layer]), row(k_norm_g[layer]), cos, sin,
            pool_w[layer], row(pool_scale[layer]),
            (w_out[layer], w_cq[layer], w_ckv[layer], w_co[layer]), seq)
        branches = [_dilated_attn(qkv, batch, seq, d) for qkv, d in zip(qkv_views, DILATIONS)]
        k_mem, v_mem = _mem_kv(mem, row(mem_norm_g[layer]), w_ckv_l, row(ck_norm_g[layer]))
        h, w_gu_l, w_down_l = _mix_cross(
            h, branches, y, w_out_l, row(cross_norm_g[layer]), w_cq_l,
            row(cq_norm_g[layer]), k_mem, v_mem, w_co_l, w_gate_up[layer], w_down[layer], seq)
        h = _ffn(h, row(ffn_norm_g[layer]), w_gu_l, w_down_l)
    return h.reshape(batch, seq, D_MODEL)
```

```python
import functools

import jax
import jax.numpy as jnp
from jax import lax
from jax.experimental import pallas as pl
from jax.experimental.pallas import tpu as pltpu

D_MODEL = 1024
HEAD_DIM = 128
ATT_HEADS = 4
DILATED_PAIRS = ((128, 1), (512, 4), (2048, 16))
DILATIONS = tuple(d for _, d in DILATED_PAIRS)
KV_W = ATT_HEADS * HEAD_DIM
POOL_WINDOWS = (2, 4, 8, 16)
POOL_W = len(POOL_WINDOWS) * HEAD_DIM
IN_W = (len(DILATED_PAIRS) + 2) * KV_W + POOL_W
ROT_DIM = HEAD_DIM // 4
ROT_HALF = ROT_DIM // 2
ROPE_THETA = 500000.0
X_HEADS = 4
X_W = X_HEADS * HEAD_DIM
D_FF = 2816
EPS = 1e-6
NEG_INF = -1e30
ATT_BLOCK = 128
ATT_STEP_QUERIES = 2048
LANES = 128
ATT_OUT_W = KV_W + LANES
SUBLANES = 8
BF16_TILE_ROWS = 2 * SUBLANES
ROT_PAIR_LANE = LANES // 2
ROW_CHUNK = 256
MERGE_CHUNK = 512
POOL_HALO = 32

F32 = jnp.float32
BF16 = jnp.bfloat16

VMEM_LIMIT = 56 * 1024 * 1024


def _rms(x, g):
    ms = jnp.mean(x * x, axis=-1, keepdims=True)
    return x * lax.rsqrt(ms + EPS) * g


def _resident(shape):
    nd = len(shape)
    return pl.BlockSpec(shape, lambda *_: (0,) * nd, pipeline_mode=pl.Buffered(1))


def _to_rotary_lanes(w):
    n_mid = ROT_PAIR_LANE - ROT_HALF
    lane = lax.broadcasted_iota(jnp.int32, w.shape, w.ndim - 1)
    keep = jnp.logical_or(lane < ROT_HALF, lane >= ROT_DIM + n_mid)
    from_low = jnp.logical_and(lane >= ROT_PAIR_LANE, lane < ROT_PAIR_LANE + ROT_HALF)
    moved = jnp.where(from_low, pltpu.roll(w, n_mid, w.ndim - 1),
                      pltpu.roll(w, LANES - ROT_HALF, w.ndim - 1))
    return jnp.where(keep, w, moved)


def _prep_kernel(pos_ref, invf_ref, win_ref, ctab_ref, stab_ref, win_o, *, n_qk_heads):
    ang = pos_ref[...] * invf_ref[...]
    cos = jnp.cos(ang)
    sin = jnp.sin(ang)
    n_rows = ang.shape[0]
    per_row = LANES // ROT_HALF
    lane = lax.broadcasted_iota(jnp.int32, (n_rows, LANES), 1)
    first = lane < ROT_HALF
    second = jnp.logical_and(lane >= ROT_PAIR_LANE, lane < ROT_PAIR_LANE + ROT_HALF)
    half = per_row // 2
    for j in range(half):
        shift = (LANES - ROT_HALF * j) % LANES
        c_lo = pltpu.roll(cos, shift, 1) if shift else cos
        s_lo = pltpu.roll(sin, shift, 1) if shift else sin
        c_hi = pltpu.roll(c_lo, ROT_PAIR_LANE, 1)
        s_hi = pltpu.roll(s_lo, ROT_PAIR_LANE, 1)
        for tok, (ca, cb, sa, sb) in ((j, (c_lo, c_hi, s_lo, s_hi)), (j + half, (c_hi, c_lo, s_hi, s_lo))):
            dst = pl.ds(tok, n_rows, stride=per_row)
            ctab_ref[dst, :] = jnp.where(first, ca, jnp.where(second, cb, 1.0))
            stab_ref[dst, :] = jnp.where(first, -sa, jnp.where(second, sb, 0.0))

    for h in range(n_qk_heads):
        sl = slice(h * HEAD_DIM, (h + 1) * HEAD_DIM)
        win_o[:, sl] = _to_rotary_lanes(win_ref[:, sl]).astype(BF16)
    rest = slice(n_qk_heads * HEAD_DIM, IN_W)
    win_o[:, rest] = win_ref[:, rest].astype(BF16)


def _cast_specs(weights, n_steps):
    specs = []
    for w in weights:
        rows = -(-w.shape[0] // (n_steps * BF16_TILE_ROWS)) * BF16_TILE_ROWS
        n_blk = w.shape[0] // rows
        assert n_blk * rows == w.shape[0] and n_blk <= n_steps
        specs.append(pl.BlockSpec((rows, w.shape[1]), lambda i, n_blk=n_blk: (jnp.minimum(i, n_blk - 1), 0)))
    return specs


def _prep(positions, w_in):
    n_tok = positions.size
    per_row = LANES // ROT_HALF
    n_steps = 4
    rows = n_tok // per_row // n_steps
    pos = jnp.broadcast_to(positions.reshape(-1, per_row, 1).astype(F32),
                           (n_tok // per_row, per_row, ROT_HALF)).reshape(n_tok // per_row, LANES)
    inv_freq = ROPE_THETA ** (-jnp.arange(0, ROT_DIM, 2, dtype=F32) / ROT_DIM)
    invf = jnp.tile(inv_freq, per_row).reshape(1, LANES)
    weights = (w_in,)
    row_blk = lambda w: pl.BlockSpec((w.shape[0] // n_steps, w.shape[1]), lambda i: (i, 0))
    tab = pl.BlockSpec((rows * per_row, LANES), lambda i: (i, 0))
    return pl.pallas_call(
        functools.partial(_prep_kernel, n_qk_heads=(len(DILATED_PAIRS) + 1) * ATT_HEADS),
        grid=(n_steps,),
        in_specs=[pl.BlockSpec((rows, LANES), lambda i: (i, 0)), pl.BlockSpec((1, LANES), lambda i: (0, 0))]
                 + [row_blk(w) for w in weights],
        out_specs=[tab, tab] + [row_blk(w) for w in weights],
        out_shape=[jax.ShapeDtypeStruct((n_tok, LANES), F32)] * 2
                  + [jax.ShapeDtypeStruct(w.shape, BF16) for w in weights],
        compiler_params=pltpu.CompilerParams(dimension_semantics=("parallel",),
                                             vmem_limit_bytes=VMEM_LIMIT),
        name="prep",
    )(pos, invf, *weights)


def _in_proj_kernel(x_ref, g_ref, w_ref, qg_ref, kg_ref, cos_ref, sin_ref, pw_ref, ps_ref,
                    wout_ref, wcq_ref, wckv_ref, wco_ref,
                    tok_ref, view4_ref, view16_ref, y_ref, wout_o, wcq_o, wckv_o, wco_o,
                    xn_ref, tab_ref, pa_ref, pb_ref, hist_ref, lvl_ref, stage_ref, stage4_ref,
                    *, tm, tiles_per_seq):
    step = pl.program_id(0)
    for src, dst in ((wout_ref, wout_o), (wcq_ref, wcq_o), (wckv_ref, wckv_o), (wco_ref, wco_o)):
        dst[...] = src[...].astype(BF16)
    rc = ROW_CHUNK
    n_rc = tm // rc

    def project(p_ref):
        for i in range(n_rc):
            rows = slice(i * rc, (i + 1) * rc)
            xn_ref[rows, :] = _rms(x_ref[rows, :], g_ref[...]).astype(BF16)
        for c in range(IN_W // KV_W):
            cols = slice(c * KV_W, (c + 1) * KV_W)
            p_ref[:, cols] = jnp.dot(xn_ref[...], w_ref[:, cols], preferred_element_type=F32)

    def finish(p_ref):
        for gi, (gain_ref, scale) in enumerate(((qg_ref, HEAD_DIM ** -0.5), (kg_ref, 1.0))):
            gain = _to_rotary_lanes(gain_ref[...] * scale)
            tab_ref[2 * gi] = cos_ref[...] * gain
            tab_ref[2 * gi + 1] = sin_ref[...] * pltpu.roll(gain, ROT_PAIR_LANE, 1)

        def emit(c, gi, part, to_tok, to_v4, to_v16, a):
            for h in range(ATT_HEADS):
                src = slice(c * KV_W + h * HEAD_DIM, c * KV_W + (h + 1) * HEAD_DIM)
                for i in range(n_rc):
                    rows = slice(i * rc, (i + 1) * rc)
                    t = p_ref[rows, src]
                    if gi >= 0:
                        rs = lax.rsqrt(jnp.mean(t * t, axis=-1, keepdims=True) + EPS)
                        t = (t * tab_ref[2 * gi, rows, :]
                             + pltpu.roll(t, ROT_PAIR_LANE, 1) * tab_ref[2 * gi + 1, rows, :]) * rs
                    if to_tok:
                        c0 = part * KV_W + h * HEAD_DIM
                        tok_ref[rows, c0:c0 + HEAD_DIM] = t.astype(BF16)
                    if to_v4 or to_v16:
                        stage_ref[a, h, rows, :] = t
                if not (to_v4 or to_v16):
                    continue
                n4 = tm // 4
                for b in range(4):
                    t4 = stage_ref[a, h, pl.ds(b, n4, stride=4), :]
                    if to_v4:
                        c0 = (part * 4 + b) * KV_W + h * HEAD_DIM
                        view4_ref[:, c0:c0 + HEAD_DIM] = t4.astype(BF16)
                    if to_v16:
                        stage4_ref[a, h, b] = t4
                if to_v16:
                    for b in range(4):
                        for a4 in range(4):
                            c0 = (part * 16 + 4 * a4 + b) * KV_W + h * HEAD_DIM
                            view16_ref[:, c0:c0 + HEAD_DIM] = (
                                stage4_ref[a, h, b, pl.ds(a4, n4 // 4, stride=4), :].astype(BF16))

        emit(3, 1, 1, True, True, True, 0)
        emit(4, -1, 2, True, True, True, 1)
        emit(2, 0, 0, False, False, True, 2)
        emit(1, 0, 0, False, True, False, 3)
        emit(0, 0, 0, True, False, False, 0)

        seq_tile = jnp.maximum(step - 1, 0) % tiles_per_seq
        end = POOL_HALO + tm
        hist_ref[0:POOL_HALO, :] = jnp.where(seq_tile == 0, 0.0, 1.0) * hist_ref[tm:end, :]
        u_cols = slice(IN_W - POOL_W, IN_W)
        hist_ref[POOL_HALO:end, :] = p_ref[:, u_cols]
        row16 = lax.broadcasted_iota(jnp.int32, (2 * SUBLANES, HEAD_DIM), 0)
        for g, w in enumerate(POOL_WINDOWS):
            sl = slice(g * HEAD_DIM, (g + 1) * HEAD_DIM)
            src, start, shift = hist_ref, SUBLANES, 1
            while True:
                cols = sl if src is hist_ref else slice(None)
                tot = src[start:end, cols] + src[start - shift:end - shift, cols]
                shift *= 2
                if shift == w:
                    break
                lvl_ref[g, start:end, :] = tot
                src, start = lvl_ref.at[g], start + SUBLANES
            tot = tot[POOL_HALO - start:, :]
            n_first = jnp.maximum(jnp.minimum(row16 + 1, w), jnp.where(seq_tile == 0, 0, w))
            inv_first = 1.0 / n_first.astype(F32)
            ug = hist_ref[POOL_HALO:end, sl]
            d = jnp.concatenate([tot[:2 * SUBLANES] * inv_first, tot[2 * SUBLANES:] * (1.0 / w)],
                                axis=0) - ug
            yg = jnp.dot(d.astype(BF16), pw_ref[g].astype(BF16),
                         preferred_element_type=F32) * ps_ref[:, sl]
            y_ref[:, sl] = yg.astype(BF16)

    @pl.when(step == 0)
    def _():
        pb_ref[...] = jnp.zeros(pb_ref.shape, F32)
        hist_ref[...] = jnp.zeros(hist_ref.shape, F32)

    @pl.when(step % 2 == 0)
    def _():
        project(pa_ref)
        finish(pb_ref)

    @pl.when(step % 2 == 1)
    def _():
        project(pb_ref)
        finish(pa_ref)


def _in_proj(x2, g_mix, w_in, qg, kg, cos, sin, pool_w, pool_scale, later_weights, seq):
    n_tok = x2.shape[0]
    tm = 512
    n_tiles = n_tok // tm
    tiles_per_seq = seq // tm
    ahead = lambda i: (jnp.minimum(i, n_tiles - 1), 0)
    behind = lambda i: (jnp.maximum(i - 1, 0), 0)
    out_dw = [(d, 3 * KV_W) for d in DILATIONS] + [(1, POOL_W)]
    view = lambda d, w: pl.BlockSpec((tm // d, d * w), behind)
    view_sds = lambda d, w: jax.ShapeDtypeStruct((n_tok // d, d * w), BF16)
    cast_specs = _cast_specs(later_weights, n_tiles + 1)
    return pl.pallas_call(
        functools.partial(_in_proj_kernel, tm=tm, tiles_per_seq=tiles_per_seq),
        grid=(n_tiles + 1,),
        in_specs=[pl.BlockSpec((tm, D_MODEL), ahead), _resident((1, D_MODEL)),
                  _resident((D_MODEL, IN_W)), _resident((1, HEAD_DIM)), _resident((1, HEAD_DIM)),
                  pl.BlockSpec((tm, LANES), behind), pl.BlockSpec((tm, LANES), behind),
                  _resident((len(POOL_WINDOWS), HEAD_DIM, HEAD_DIM)), _resident((1, POOL_W))]
                 + cast_specs,
        out_specs=[view(d, w) for d, w in out_dw] + cast_specs,
        out_shape=[view_sds(d, w) for d, w in out_dw]
                  + [jax.ShapeDtypeStruct(w.shape, BF16) for w in later_weights],
        scratch_shapes=[pltpu.VMEM((tm, D_MODEL), BF16),
                        pltpu.VMEM((4, tm, LANES), F32),
                        pltpu.VMEM((tm, IN_W), F32),
                        pltpu.VMEM((tm, IN_W), F32),
                        pltpu.VMEM((POOL_HALO + tm, POOL_W), F32),
                        pltpu.VMEM((len(POOL_WINDOWS), POOL_HALO + tm, HEAD_DIM), F32),
                        pltpu.VMEM((4, ATT_HEADS, tm, HEAD_DIM), F32),
                        pltpu.VMEM((3, ATT_HEADS, 4, tm // 4, HEAD_DIM), F32)],
        compiler_params=pltpu.CompilerParams(dimension_semantics=("arbitrary",),
                                             vmem_limit_bytes=VMEM_LIMIT),
        name="in_proj",
    )(x2, g_mix, w_in, qg, kg, cos, sin, pool_w, pool_scale, *later_weights)


def _dilated_attn_kernel(q_ref, k_ref, kh_ref, v_ref, vh_ref, o_ref, *, tl, rb):
    n_sub = tl // ATT_BLOCK
    qi = lax.broadcasted_iota(jnp.int32, (ATT_BLOCK, 2 * ATT_BLOCK), 0)
    kj = lax.broadcasted_iota(jnp.int32, (ATT_BLOCK, 2 * ATT_BLOCK), 1)
    delta = qi + ATT_BLOCK - kj
    in_band = jnp.logical_and(delta >= 0, delta <= ATT_BLOCK)
    bias = jnp.where(in_band, 0.0, NEG_INF).astype(F32)
    n_missing = jnp.where(pl.program_id(2) == 0, ATT_BLOCK, 0)
    bias_first = jnp.where(kj < n_missing, NEG_INF, bias)
    lane = lax.broadcasted_iota(jnp.int32, (ATT_BLOCK, LANES), 1)
    ones_blk = jnp.ones((2 * ATT_BLOCK, HEAD_DIM), BF16)

    for r in range(rb):
        for i in range(n_sub):
            rows = slice(i * ATT_BLOCK, (i + 1) * ATT_BLOCK)
            rows2 = slice((i - 1) * ATT_BLOCK, (i + 1) * ATT_BLOCK)
            st = jnp.zeros((ATT_BLOCK, LANES), F32)
            for h in range(ATT_HEADS):
                c0 = r * KV_W + h * HEAD_DIM
                sl = slice(c0, c0 + HEAD_DIM)
                if i == 0:
                    k_blk = jnp.concatenate([kh_ref[0, :, sl], k_ref[0, rows, sl]], axis=0)
                    v_blk = jnp.concatenate([vh_ref[0, :, sl], v_ref[0, rows, sl]], axis=0)
                    b = bias_first
                else:
                    k_blk, v_blk, b = k_ref[0, rows2, sl], v_ref[0, rows2, sl], bias
                s = lax.dot_general(q_ref[0, rows, sl], k_blk, (((1,), (1,)), ((), ())),
                                    preferred_element_type=F32) + b
                m = jnp.max(s, axis=-1, keepdims=True).astype(BF16).astype(F32)
                p = jnp.exp(s - m).astype(BF16)
                acc_l = jnp.dot(p, jnp.concatenate([v_blk, ones_blk], axis=1),
                                preferred_element_type=F32)
                o0 = r * ATT_OUT_W + h * HEAD_DIM
                o_ref[0, rows, o0:o0 + HEAD_DIM] = acc_l[:, :HEAD_DIM].astype(BF16)
                l_hi = acc_l[:, HEAD_DIM:].astype(BF16).astype(F32)
                st = jnp.where(lane == h, m, st)
                st = jnp.where(lane == ATT_HEADS + h, l_hi, st)
                st = jnp.where(lane == 2 * ATT_HEADS + h, acc_l[:, HEAD_DIM:] - l_hi, st)
            o_ref[0, rows, r * ATT_OUT_W + KV_W:(r + 1) * ATT_OUT_W] = st.astype(BF16)


def _dilated_attn(qkv, batch, seq, dilation):
    sub_len = seq // dilation
    tl = min(sub_len, ATT_STEP_QUERIES)
    rb = min(dilation, ATT_STEP_QUERIES // tl)
    n_sub = tl // ATT_BLOCK
    n_col = dilation // rb
    qkv3 = qkv.reshape(batch, sub_len, qkv.shape[1])
    cur = lambda part: pl.BlockSpec((1, tl, rb * KV_W), lambda b, r, i: (b, i, part * n_col + r))
    halo = lambda part: pl.BlockSpec(
        (1, ATT_BLOCK, rb * KV_W), lambda b, r, i: (b, jnp.maximum(i * n_sub - 1, 0), part * n_col + r))
    out = pl.pallas_call(
        functools.partial(_dilated_attn_kernel, tl=tl, rb=rb),
        grid=(batch, n_col, sub_len // tl),
        in_specs=[cur(0), cur(1), halo(1), cur(2), halo(2)],
        out_specs=pl.BlockSpec((1, tl, rb * ATT_OUT_W), lambda b, r, i: (b, i, r)),
        out_shape=jax.ShapeDtypeStruct((batch, sub_len, dilation * ATT_OUT_W), BF16),
        compiler_params=pltpu.CompilerParams(
            dimension_semantics=("parallel", "parallel", "parallel"), vmem_limit_bytes=VMEM_LIMIT),
        name=f"dilated_attn_d{dilation}",
    )(qkv3, qkv3, qkv3, qkv3, qkv3)
    return out.reshape(batch * sub_len, dilation * ATT_OUT_W)


def _mem_kv_kernel(mem_ref, g_ref, w_ref, kg_ref, k_ref, v_ref):
    mn = _rms(mem_ref[0], g_ref[...]).astype(BF16)
    kv = jnp.dot(mn, w_ref[...], preferred_element_type=F32)
    for h in range(X_HEADS):
        sl = slice(h * HEAD_DIM, (h + 1) * HEAD_DIM)
        k_ref[0, :, sl] = _rms(kv[:, sl], kg_ref[...]).astype(BF16)
    v_ref[0] = kv[:, X_W:].astype(BF16)


def _mem_kv(mem, g_mem, w_ckv, ckg):
    batch, n_mem, _ = mem.shape
    out = jax.ShapeDtypeStruct((batch, n_mem, X_W), BF16)
    blk = pl.BlockSpec((1, n_mem, X_W), lambda b: (b, 0, 0))
    return pl.pallas_call(
        _mem_kv_kernel,
        grid=(batch,),
        in_specs=[pl.BlockSpec((1, n_mem, D_MODEL), lambda b: (b, 0, 0)), _resident((1, D_MODEL)),
                  _resident((D_MODEL, 2 * X_W)), _resident((1, HEAD_DIM))],
        out_specs=[blk, blk],
        out_shape=[out, out],
        compiler_params=pltpu.CompilerParams(dimension_semantics=("parallel",),
                                             vmem_limit_bytes=VMEM_LIMIT),
        name="mem_kv",
    )(mem, g_mem, w_ckv, ckg)


def _mix_cross_kernel(x_ref, a1_ref, a4_ref, a16_ref, y_ref, wo_ref,
                      gc_ref, wq_ref, qg_ref, km_ref, vm_ref, wc_ref, wgu_ref, wd_ref,
                      h_ref, wgu_o, wd_o, acc_tok, st_tok, cat_a, cat_b, o_ref, *, tm):
    step = pl.program_id(0)
    wgu_o[...] = wgu_ref[...].astype(BF16)
    wd_o[...] = wd_ref[...].astype(BF16)

    def merge(cat_ref):
        for gi, (a_ref, d) in enumerate(((a4_ref, 4), (a16_ref, 16))):
            n = tm // d
            for r in range(d):
                dst = pl.ds(r, n, stride=d)
                st_tok[gi, dst, :] = a_ref[:, r * ATT_OUT_W + KV_W:(r + 1) * ATT_OUT_W].astype(F32)
                for h in range(ATT_HEADS):
                    c0 = r * ATT_OUT_W + h * HEAD_DIM
                    acc_tok[gi, h, dst, :] = a_ref[:, c0:c0 + HEAD_DIM].astype(F32)

        for i in range(tm // MERGE_CHUNK):
            rows = slice(i * MERGE_CHUNK, (i + 1) * MERGE_CHUNK)
            stats = [a1_ref[rows, KV_W:].astype(F32), st_tok[0, rows, :], st_tok[1, rows, :]]
            m_all = jnp.maximum(jnp.maximum(stats[0], stats[1]), stats[2])
            wts = [jnp.exp(s - m_all) for s in stats]
            dens = [s + pltpu.roll(s, LANES - ATT_HEADS, 1) for s in stats]
            for h in range(ATT_HEADS):
                sl = slice(h * HEAD_DIM, (h + 1) * HEAD_DIM)
                accs = (a1_ref[rows, sl].astype(F32), acc_tok[0, h, rows, :], acc_tok[1, h, rows, :])
                num = jnp.zeros((MERGE_CHUNK, HEAD_DIM), F32)
                den = jnp.zeros((MERGE_CHUNK, 1), F32)
                for g in range(len(DILATED_PAIRS)):
                    w = wts[g][:, h:h + 1]
                    num = num + w * accs[g]
                    den = den + w * dens[g][:, ATT_HEADS + h:ATT_HEADS + h + 1]
                cat_ref[rows, sl] = (num / den).astype(BF16)
        cat_ref[:, KV_W:] = y_ref[...]

    def mix(cat_ref):
        n_mem = km_ref.shape[1]
        h1 = x_ref[...] + jnp.dot(cat_ref[...], wo_ref[...], preferred_element_type=F32)
        hn = _rms(h1, gc_ref[...]).astype(BF16)
        qc = jnp.dot(hn, wq_ref[...], preferred_element_type=F32)
        for h in range(X_HEADS):
            sl = slice(h * HEAD_DIM, (h + 1) * HEAD_DIM)
            qh = _rms(qc[:, sl], qg_ref[...] * HEAD_DIM ** -0.5).astype(BF16)
            s = lax.dot_general(qh, km_ref[0, :, sl], (((1,), (1,)), ((), ())),
                                preferred_element_type=F32)
            p = jnp.exp(s - jnp.max(s, axis=-1, keepdims=True)).astype(BF16)
            v_ones = jnp.concatenate([vm_ref[0, :, sl], jnp.ones((n_mem, HEAD_DIM), BF16)], axis=1)
            o_l = jnp.dot(p, v_ones, preferred_element_type=F32)
            o_ref[:, sl] = (o_l[:, :HEAD_DIM] / o_l[:, HEAD_DIM:]).astype(BF16)
        h_ref[...] = h1 + jnp.dot(o_ref[...], wc_ref[...], preferred_element_type=F32)

    @pl.when(step == 0)
    def _():
        cat_b[...] = jnp.zeros(cat_b.shape, BF16)

    @pl.when(step % 2 == 0)
    def _():
        merge(cat_a)
        mix(cat_b)

    @pl.when(step % 2 == 1)
    def _():
        merge(cat_b)
        mix(cat_a)


def _mix_cross(x2, branches, y, w_out, g_cross, w_cq, cqg, k_mem, v_mem, w_co, w_gate_up, w_down, seq):
    n_tok = x2.shape[0]
    tm = 512
    n_tiles = n_tok // tm
    tiles_per_seq = seq // tm
    n_mem = k_mem.shape[1]
    ahead = lambda i: (jnp.minimum(i, n_tiles - 1), 0)
    behind = lambda i: (jnp.maximum(i - 1, 0), 0)
    view = lambda d, w: pl.BlockSpec((tm // d, d * w), ahead)
    mem_blk = pl.BlockSpec((1, n_mem, X_W), lambda i: (jnp.maximum(i - 1, 0) // tiles_per_seq, 0, 0))

    def cast_blk(w, rows):
        n_blk = w.shape[0] // rows
        assert n_blk * rows == w.shape[0] and n_blk <= n_tiles + 1
        return pl.BlockSpec((rows, w.shape[1]), lambda i: (jnp.minimum(i, n_blk - 1), 0))

    cast_specs = [cast_blk(w_gate_up, 32), cast_blk(w_down, 128)]
    return pl.pallas_call(
        functools.partial(_mix_cross_kernel, tm=tm),
        grid=(n_tiles + 1,),
        in_specs=[pl.BlockSpec((tm, D_MODEL), behind)] + [view(d, ATT_OUT_W) for d in DILATIONS]
                 + [view(1, POOL_W), _resident((KV_W + POOL_W, D_MODEL)), _resident((1, D_MODEL)),
                    _resident((D_MODEL, X_W)), _resident((1, HEAD_DIM)), mem_blk, mem_blk,
                    _resident((X_W, D_MODEL))] + cast_specs,
        out_specs=[pl.BlockSpec((tm, D_MODEL), behind)] + cast_specs,
        out_shape=[jax.ShapeDtypeStruct((n_tok, D_MODEL), F32),
                   jax.ShapeDtypeStruct(w_gate_up.shape, BF16),
                   jax.ShapeDtypeStruct(w_down.shape, BF16)],
        scratch_shapes=[pltpu.VMEM((2, ATT_HEADS, tm, HEAD_DIM), F32),
                        pltpu.VMEM((2, tm, LANES), F32),
                        pltpu.VMEM((tm, KV_W + POOL_W), BF16),
                        pltpu.VMEM((tm, KV_W + POOL_W), BF16),
                        pltpu.VMEM((tm, X_W), BF16)],
        compiler_params=pltpu.CompilerParams(dimension_semantics=("arbitrary",),
                                             vmem_limit_bytes=VMEM_LIMIT),
        name="mix_cross",
    )(x2, *branches, y, w_out, g_cross, w_cq, cqg, k_mem, v_mem, w_co, w_gate_up, w_down)


def _ffn_kernel(h_ref, g_ref, wgu_ref, wd_ref, o_ref):
    h = h_ref[...]
    hn = _rms(h, g_ref[...]).astype(BF16)
    gate = jnp.dot(hn, wgu_ref[:, :D_FF], preferred_element_type=F32)
    up = jnp.dot(hn, wgu_ref[:, D_FF:], preferred_element_type=F32)
    act = (gate * jax.nn.sigmoid(gate) * up).astype(BF16)
    o_ref[...] = h + jnp.dot(act, wd_ref[...], preferred_element_type=F32)


def _ffn(h2, g_ffn, w_gate_up, w_down):
    n_tok = h2.shape[0]
    tm = 512
    tok = pl.BlockSpec((tm, D_MODEL), lambda i: (i, 0))
    return pl.pallas_call(
        _ffn_kernel,
        grid=(n_tok // tm,),
        in_specs=[tok, _resident((1, D_MODEL)), _resident((D_MODEL, 2 * D_FF)),
                  _resident((D_FF, D_MODEL))],
        out_specs=tok,
        out_shape=jax.ShapeDtypeStruct((n_tok, D_MODEL), F32),
        compiler_params=pltpu.CompilerParams(dimension_semantics=("parallel",),
                                             vmem_limit_bytes=VMEM_LIMIT),
        name="ffn",
    )(h2, g_ffn, w_gate_up, w_down)


def kernel(x, mem, positions, mix_norm_g, w_in, q_norm_g, k_norm_g, pool_w, pool_scale, w_out,
           cross_norm_g, mem_norm_g, w_cq, w_ckv, cq_norm_g, ck_norm_g, w_co,
           ffn_norm_g, w_gate_up, w_down):
    batch, seq, _ = x.shape
    depth = w_in.shape[0]
    row = lambda a: a.reshape(1, -1)
    h = x.reshape(batch * seq, D_MODEL)
    for layer in range(depth):
        cos, sin, w_in_l = _prep(positions, w_in[layer])
        *qkv_views, y, w_out_l, w_cq_l, w_ckv_l, w_co_l = _in_proj(
            h, row(mix_norm_g[layer]), w_in_l, row(q_norm_g[layer]), row(k_norm_g[layer]), cos, sin,
            pool_w[layer], row(pool_scale[layer]),
            (w_out[layer], w_cq[layer], w_ckv[layer], w_co[layer]), seq)
        branches = [_dilated_attn(qkv, batch, seq, d) for qkv, d in zip(qkv_views, DILATIONS)]
        k_mem, v_mem = _mem_kv(mem, row(mem_norm_g[layer]), w_ckv_l, row(ck_norm_g[layer]))
        h, w_gu_l, w_down_l = _mix_cross(
            h, branches, y, w_out_l, row(cross_norm_g[layer]), w_cq_l,
            row(cq_norm_g[layer]), k_mem, v_mem, w_co_l, w_gate_up[layer], w_down[layer], seq)
        h = _ffn(h, row(ffn_norm_g[layer]), w_gu_l, w_down_l)
    return h.reshape(batch, seq, D_MODEL)
```

```python
import functools
import math

import jax
import jax.numpy as jnp
from jax import lax
from jax.experimental import pallas as pl
from jax.experimental.pallas import tpu as pltpu

D_MODEL = 1024
HEAD_DIM = 128
ATT_HEADS = 4
DILATED_PAIRS = ((128, 1), (512, 4), (2048, 16))
DILATIONS = tuple(d for _, d in DILATED_PAIRS)
KV_W = ATT_HEADS * HEAD_DIM
POOL_WINDOWS = (2, 4, 8, 16)
POOL_W = len(POOL_WINDOWS) * HEAD_DIM
IN_W = (len(DILATED_PAIRS) + 2) * KV_W + POOL_W
ROT_DIM = HEAD_DIM // 4
ROT_HALF = ROT_DIM // 2
ROPE_THETA = 500000.0
X_HEADS = 4
X_W = X_HEADS * HEAD_DIM
D_FF = 2816
EPS = 1e-6
NEG_INF = -1e30
SCORE_SCALE = HEAD_DIM ** -0.5 * math.log2(math.e)
ATT_BLOCK = 128
ATT_STEP_QUERIES = 2048
LANES = 128
ATT_OUT_W = KV_W + LANES
SUBLANES = 8
BF16_TILE_ROWS = 2 * SUBLANES
ROT_PAIR_LANE = LANES // 2
ROW_CHUNK = 256
MERGE_CHUNK = 512
POOL_HALO = 32

F32 = jnp.float32
BF16 = jnp.bfloat16

VMEM_LIMIT = 56 * 1024 * 1024


def _rms(x, g):
    ms = jnp.mean(x * x, axis=-1, keepdims=True)
    return x * lax.rsqrt(ms + EPS) * g


def _resident(shape):
    nd = len(shape)
    return pl.BlockSpec(shape, lambda *_: (0,) * nd, pipeline_mode=pl.Buffered(1))


def _to_rotary_lanes(w):
    n_mid = ROT_PAIR_LANE - ROT_HALF
    lane = lax.broadcasted_iota(jnp.int32, w.shape, w.ndim - 1)
    keep = jnp.logical_or(lane < ROT_HALF, lane >= ROT_DIM + n_mid)
    from_low = jnp.logical_and(lane >= ROT_PAIR_LANE, lane < ROT_PAIR_LANE + ROT_HALF)
    moved = jnp.where(from_low, pltpu.roll(w, n_mid, w.ndim - 1),
                      pltpu.roll(w, LANES - ROT_HALF, w.ndim - 1))
    return jnp.where(keep, w, moved)


def _prep_kernel(pos_ref, invf_ref, win_ref, ctab_ref, stab_ref, win_o, *, n_qk_heads):
    ang = pos_ref[...] * invf_ref[...]
    cos = jnp.cos(ang)
    sin = jnp.sin(ang)
    n_rows = ang.shape[0]
    per_row = LANES // ROT_HALF
    lane = lax.broadcasted_iota(jnp.int32, (n_rows, LANES), 1)
    first = lane < ROT_HALF
    second = jnp.logical_and(lane >= ROT_PAIR_LANE, lane < ROT_PAIR_LANE + ROT_HALF)
    half = per_row // 2
    for j in range(half):
        shift = (LANES - ROT_HALF * j) % LANES
        c_lo = pltpu.roll(cos, shift, 1) if shift else cos
        s_lo = pltpu.roll(sin, shift, 1) if shift else sin
        c_hi = pltpu.roll(c_lo, ROT_PAIR_LANE, 1)
        s_hi = pltpu.roll(s_lo, ROT_PAIR_LANE, 1)
        for tok, (ca, cb, sa, sb) in ((j, (c_lo, c_hi, s_lo, s_hi)), (j + half, (c_hi, c_lo, s_hi, s_lo))):
            dst = pl.ds(tok, n_rows, stride=per_row)
            ctab_ref[dst, :] = jnp.where(first, ca, jnp.where(second, cb, 1.0))
            stab_ref[dst, :] = jnp.where(first, -sa, jnp.where(second, sb, 0.0))

    for h in range(n_qk_heads):
        sl = slice(h * HEAD_DIM, (h + 1) * HEAD_DIM)
        win_o[:, sl] = _to_rotary_lanes(win_ref[:, sl]).astype(BF16)
    rest = slice(n_qk_heads * HEAD_DIM, IN_W)
    win_o[:, rest] = win_ref[:, rest].astype(BF16)


def _cast_specs(weights, n_steps):
    specs = []
    for w in weights:
        rows = -(-w.shape[0] // (n_steps * BF16_TILE_ROWS)) * BF16_TILE_ROWS
        n_blk = w.shape[0] // rows
        assert n_blk * rows == w.shape[0] and n_blk <= n_steps
        specs.append(pl.BlockSpec((rows, w.shape[1]), lambda i, n_blk=n_blk: (jnp.minimum(i, n_blk - 1), 0)))
    return specs


def _prep(positions, w_in):
    n_tok = positions.size
    per_row = LANES // ROT_HALF
    n_steps = 4
    rows = n_tok // per_row // n_steps
    pos = jnp.broadcast_to(positions.reshape(-1, per_row, 1).astype(F32),
                           (n_tok // per_row, per_row, ROT_HALF)).reshape(n_tok // per_row, LANES)
    inv_freq = ROPE_THETA ** (-jnp.arange(0, ROT_DIM, 2, dtype=F32) / ROT_DIM)
    invf = jnp.tile(inv_freq, per_row).reshape(1, LANES)
    weights = (w_in,)
    row_blk = lambda w: pl.BlockSpec((w.shape[0] // n_steps, w.shape[1]), lambda i: (i, 0))
    tab = pl.BlockSpec((rows * per_row, LANES), lambda i: (i, 0))
    return pl.pallas_call(
        functools.partial(_prep_kernel, n_qk_heads=(len(DILATED_PAIRS) + 1) * ATT_HEADS),
        grid=(n_steps,),
        in_specs=[pl.BlockSpec((rows, LANES), lambda i: (i, 0)), pl.BlockSpec((1, LANES), lambda i: (0, 0))]
                 + [row_blk(w) for w in weights],
        out_specs=[tab, tab] + [row_blk(w) for w in weights],
        out_shape=[jax.ShapeDtypeStruct((n_tok, LANES), F32)] * 2
                  + [jax.ShapeDtypeStruct(w.shape, BF16) for w in weights],
        compiler_params=pltpu.CompilerParams(dimension_semantics=("parallel",),
                                             vmem_limit_bytes=VMEM_LIMIT),
        name="prep",
    )(pos, invf, *weights)


def _in_proj_kernel(x_ref, g_ref, w_ref, qg_ref, kg_ref, cos_ref, sin_ref, pw_ref, ps_ref,
                    wout_ref, wcq_ref, wckv_ref, wco_ref,
                    tok_ref, view4_ref, view16_ref, y_ref, wout_o, wcq_o, wckv_o, wco_o,
                    xn_ref, tab_ref, pa_ref, pb_ref, hist_ref, lvl_ref, stage_ref, stage4_ref,
                    *, tm, tiles_per_seq):
    step = pl.program_id(0)
    for src, dst in ((wout_ref, wout_o), (wcq_ref, wcq_o), (wckv_ref, wckv_o), (wco_ref, wco_o)):
        dst[...] = src[...].astype(BF16)
    rc = ROW_CHUNK
    n_rc = tm // rc

    def project(p_ref):
        for i in range(n_rc):
            rows = slice(i * rc, (i + 1) * rc)
            xn_ref[rows, :] = _rms(x_ref[rows, :], g_ref[...]).astype(BF16)
        for c in range(IN_W // KV_W):
            cols = slice(c * KV_W, (c + 1) * KV_W)
            p_ref[:, cols] = jnp.dot(xn_ref[...], w_ref[:, cols], preferred_element_type=F32)

    def finish(p_ref):
        for gi, (gain_ref, scale) in enumerate(((qg_ref, SCORE_SCALE), (kg_ref, 1.0))):
            gain = _to_rotary_lanes(gain_ref[...] * scale)
            tab_ref[2 * gi] = cos_ref[...] * gain
            tab_ref[2 * gi + 1] = sin_ref[...] * pltpu.roll(gain, ROT_PAIR_LANE, 1)

        def emit(c, gi, part, to_tok, to_v4, to_v16, a):
            for h in range(ATT_HEADS):
                src = slice(c * KV_W + h * HEAD_DIM, c * KV_W + (h + 1) * HEAD_DIM)
                for i in range(n_rc):
                    rows = slice(i * rc, (i + 1) * rc)
                    t = p_ref[rows, src]
                    if gi >= 0:
                        rs = lax.rsqrt(jnp.mean(t * t, axis=-1, keepdims=True) + EPS)
                        t = (t * tab_ref[2 * gi, rows, :]
                             + pltpu.roll(t, ROT_PAIR_LANE, 1) * tab_ref[2 * gi + 1, rows, :]) * rs
                    if to_tok:
                        c0 = part * KV_W + h * HEAD_DIM
                        tok_ref[rows, c0:c0 + HEAD_DIM] = t.astype(BF16)
                    if to_v4 or to_v16:
                        stage_ref[a, h, rows, :] = t
                if not (to_v4 or to_v16):
                    continue
                n4 = tm // 4
                for b in range(4):
                    t4 = stage_ref[a, h, pl.ds(b, n4, stride=4), :]
                    if to_v4:
                        c0 = (part * 4 + b) * KV_W + h * HEAD_DIM
                        view4_ref[:, c0:c0 + HEAD_DIM] = t4.astype(BF16)
                    if to_v16:
                        stage4_ref[a, h, b] = t4
                if to_v16:
                    for b in range(4):
                        for a4 in range(4):
                            c0 = (part * 16 + 4 * a4 + b) * KV_W + h * HEAD_DIM
                            view16_ref[:, c0:c0 + HEAD_DIM] = (
                                stage4_ref[a, h, b, pl.ds(a4, n4 // 4, stride=4), :].astype(BF16))

        emit(3, 1, 1, True, True, True, 0)
        emit(4, -1, 2, True, True, True, 1)
        emit(2, 0, 0, False, False, True, 2)
        emit(1, 0, 0, False, True, False, 3)
        emit(0, 0, 0, True, False, False, 0)

        seq_tile = jnp.maximum(step - 1, 0) % tiles_per_seq
        end = POOL_HALO + tm
        hist_ref[0:POOL_HALO, :] = jnp.where(seq_tile == 0, 0.0, 1.0) * hist_ref[tm:end, :]
        u_cols = slice(IN_W - POOL_W, IN_W)
        hist_ref[POOL_HALO:end, :] = p_ref[:, u_cols]
        row16 = lax.broadcasted_iota(jnp.int32, (2 * SUBLANES, HEAD_DIM), 0)
        for g, w in enumerate(POOL_WINDOWS):
            sl = slice(g * HEAD_DIM, (g + 1) * HEAD_DIM)
            src, start, shift = hist_ref, SUBLANES, 1
            while True:
                cols = sl if src is hist_ref else slice(None)
                tot = src[start:end, cols] + src[start - shift:end - shift, cols]
                shift *= 2
                if shift == w:
                    break
                lvl_ref[g, start:end, :] = tot
                src, start = lvl_ref.at[g], start + SUBLANES
            tot = tot[POOL_HALO - start:, :]
            n_first = jnp.maximum(jnp.minimum(row16 + 1, w), jnp.where(seq_tile == 0, 0, w))
            inv_first = 1.0 / n_first.astype(F32)
            ug = hist_ref[POOL_HALO:end, sl]
            d = jnp.concatenate([tot[:2 * SUBLANES] * inv_first, tot[2 * SUBLANES:] * (1.0 / w)],
                                axis=0) - ug
            yg = jnp.dot(d.astype(BF16), pw_ref[g].astype(BF16),
                         preferred_element_type=F32) * ps_ref[:, sl]
            y_ref[:, sl] = yg.astype(BF16)

    @pl.when(step == 0)
    def _():
        pb_ref[...] = jnp.zeros(pb_ref.shape, F32)
        hist_ref[...] = jnp.zeros(hist_ref.shape, F32)

    @pl.when(step % 2 == 0)
    def _():
        project(pa_ref)
        finish(pb_ref)

    @pl.when(step % 2 == 1)
    def _():
        project(pb_ref)
        finish(pa_ref)


def _in_proj(x2, g_mix, w_in, qg, kg, cos, sin, pool_w, pool_scale, later_weights, seq):
    n_tok = x2.shape[0]
    tm = 512
    n_tiles = n_tok // tm
    tiles_per_seq = seq // tm
    ahead = lambda i: (jnp.minimum(i, n_tiles - 1), 0)
    behind = lambda i: (jnp.maximum(i - 1, 0), 0)
    out_dw = [(d, 3 * KV_W) for d in DILATIONS] + [(1, POOL_W)]
    view = lambda d, w: pl.BlockSpec((tm // d, d * w), behind)
    view_sds = lambda d, w: jax.ShapeDtypeStruct((n_tok // d, d * w), BF16)
    cast_specs = _cast_specs(later_weights, n_tiles + 1)
    return pl.pallas_call(
        functools.partial(_in_proj_kernel, tm=tm, tiles_per_seq=tiles_per_seq),
        grid=(n_tiles + 1,),
        in_specs=[pl.BlockSpec((tm, D_MODEL), ahead), _resident((1, D_MODEL)),
                  _resident((D_MODEL, IN_W)), _resident((1, HEAD_DIM)), _resident((1, HEAD_DIM)),
                  pl.BlockSpec((tm, LANES), behind), pl.BlockSpec((tm, LANES), behind),
                  _resident((len(POOL_WINDOWS), HEAD_DIM, HEAD_DIM)), _resident((1, POOL_W))]
                 + cast_specs,
        out_specs=[view(d, w) for d, w in out_dw] + cast_specs,
        out_shape=[view_sds(d, w) for d, w in out_dw]
                  + [jax.ShapeDtypeStruct(w.shape, BF16) for w in later_weights],
        scratch_shapes=[pltpu.VMEM((tm, D_MODEL), BF16),
                        pltpu.VMEM((4, tm, LANES), F32),
                        pltpu.VMEM((tm, IN_W), F32),
                        pltpu.VMEM((tm, IN_W), F32),
                        pltpu.VMEM((POOL_HALO + tm, POOL_W), F32),
                        pltpu.VMEM((len(POOL_WINDOWS), POOL_HALO + tm, HEAD_DIM), F32),
                        pltpu.VMEM((4, ATT_HEADS, tm, HEAD_DIM), F32),
                        pltpu.VMEM((3, ATT_HEADS, 4, tm // 4, HEAD_DIM), F32)],
        compiler_params=pltpu.CompilerParams(dimension_semantics=("arbitrary",),
                                             vmem_limit_bytes=VMEM_LIMIT),
        name="in_proj",
    )(x2, g_mix, w_in, qg, kg, cos, sin, pool_w, pool_scale, *later_weights)


def _dilated_attn_kernel(q_ref, k_ref, kh_ref, v_ref, vh_ref, o_ref, *, tl, rb):
    n_sub = tl // ATT_BLOCK
    qi = lax.broadcasted_iota(jnp.int32, (ATT_BLOCK, 2 * ATT_BLOCK), 0)
    kj = lax.broadcasted_iota(jnp.int32, (ATT_BLOCK, 2 * ATT_BLOCK), 1)
    delta = qi + ATT_BLOCK - kj
    in_band = jnp.logical_and(delta >= 0, delta <= ATT_BLOCK)
    bias = jnp.where(in_band, 0.0, NEG_INF).astype(F32)
    n_missing = jnp.where(pl.program_id(2) == 0, ATT_BLOCK, 0)
    bias_first = jnp.where(kj < n_missing, NEG_INF, bias)
    lane = lax.broadcasted_iota(jnp.int32, (ATT_BLOCK, LANES), 1)
    key_lane = lax.broadcasted_iota(jnp.int32, (2 * ATT_BLOCK, HEAD_DIM), 1)
    sum_blk = [jnp.where(jnp.logical_or(key_lane == ATT_HEADS + h, key_lane == 2 * ATT_HEADS + h),
                         1.0, 0.0).astype(BF16) for h in range(ATT_HEADS)]

    for r in range(rb):
        for i in range(n_sub):
            rows = slice(i * ATT_BLOCK, (i + 1) * ATT_BLOCK)
            rows2 = slice((i - 1) * ATT_BLOCK, (i + 1) * ATT_BLOCK)
            shift = jnp.zeros((ATT_BLOCK, LANES), F32)
            den = jnp.zeros((ATT_BLOCK, LANES), F32)
            for h in range(ATT_HEADS):
                c0 = r * KV_W + h * HEAD_DIM
                sl = slice(c0, c0 + HEAD_DIM)
                if i == 0:
                    k_blk = jnp.concatenate([kh_ref[0, :, sl], k_ref[0, rows, sl]], axis=0)
                    v_blk = jnp.concatenate([vh_ref[0, :, sl], v_ref[0, rows, sl]], axis=0)
                    b = bias_first
                else:
                    k_blk, v_blk, b = k_ref[0, rows2, sl], v_ref[0, rows2, sl], bias
                s = lax.dot_general(q_ref[0, rows, sl], k_blk, (((1,), (1,)), ((), ())),
                                    preferred_element_type=F32) + b
                m = jnp.max(s, axis=-1, keepdims=True).astype(BF16).astype(F32)
                p = jnp.exp2(s - m).astype(BF16)
                acc_l = jnp.dot(p, jnp.concatenate([v_blk, sum_blk[h]], axis=1),
                                preferred_element_type=F32)
                o0 = r * ATT_OUT_W + h * HEAD_DIM
                o_ref[0, rows, o0:o0 + HEAD_DIM] = acc_l[:, :HEAD_DIM].astype(BF16)
                den = den + acc_l[:, HEAD_DIM:]
                shift = jnp.where(lane == h, m, shift)
            den_hi = den.astype(BF16).astype(F32)
            st = shift + jnp.where(lane < 2 * ATT_HEADS, den_hi, den - den_hi)
            o_ref[0, rows, r * ATT_OUT_W + KV_W:(r + 1) * ATT_OUT_W] = st.astype(BF16)


def _dilated_attn(qkv, batch, seq, dilation):
    sub_len = seq // dilation
    tl = min(sub_len, ATT_STEP_QUERIES)
    rb = min(dilation, ATT_STEP_QUERIES // tl)
    n_sub = tl // ATT_BLOCK
    n_col = dilation // rb
    qkv3 = qkv.reshape(batch, sub_len, qkv.shape[1])
    cur = lambda part: pl.BlockSpec((1, tl, rb * KV_W), lambda b, r, i: (b, i, part * n_col + r))
    halo = lambda part: pl.BlockSpec(
        (1, ATT_BLOCK, rb * KV_W), lambda b, r, i: (b, jnp.maximum(i * n_sub - 1, 0), part * n_col + r))
    out = pl.pallas_call(
        functools.partial(_dilated_attn_kernel, tl=tl, rb=rb),
        grid=(batch, n_col, sub_len // tl),
        in_specs=[cur(0), cur(1), halo(1), cur(2), halo(2)],
        out_specs=pl.BlockSpec((1, tl, rb * ATT_OUT_W), lambda b, r, i: (b, i, r)),
        out_shape=jax.ShapeDtypeStruct((batch, sub_len, dilation * ATT_OUT_W), BF16),
        compiler_params=pltpu.CompilerParams(
            dimension_semantics=("parallel", "parallel", "parallel"), vmem_limit_bytes=VMEM_LIMIT),
        name=f"dilated_attn_d{dilation}",
    )(qkv3, qkv3, qkv3, qkv3, qkv3)
    return out.reshape(batch * sub_len, dilation * ATT_OUT_W)


def _mem_kv_kernel(mem_ref, g_ref, w_ref, kg_ref, k_ref, v_ref):
    mn = _rms(mem_ref[0], g_ref[...]).astype(BF16)
    kv = jnp.dot(mn, w_ref[...], preferred_element_type=F32)
    for h in range(X_HEADS):
        sl = slice(h * HEAD_DIM, (h + 1) * HEAD_DIM)
        k_ref[0, :, sl] = _rms(kv[:, sl], kg_ref[...]).astype(BF16)
    v_ref[0] = kv[:, X_W:].astype(BF16)


def _mem_kv(mem, g_mem, w_ckv, ckg):
    batch, n_mem, _ = mem.shape
    out = jax.ShapeDtypeStruct((batch, n_mem, X_W), BF16)
    blk = pl.BlockSpec((1, n_mem, X_W), lambda b: (b, 0, 0))
    return pl.pallas_call(
        _mem_kv_kernel,
        grid=(batch,),
        in_specs=[pl.BlockSpec((1, n_mem, D_MODEL), lambda b: (b, 0, 0)), _resident((1, D_MODEL)),
                  _resident((D_MODEL, 2 * X_W)), _resident((1, HEAD_DIM))],
        out_specs=[blk, blk],
        out_shape=[out, out],
        compiler_params=pltpu.CompilerParams(dimension_semantics=("parallel",),
                                             vmem_limit_bytes=VMEM_LIMIT),
        name="mem_kv",
    )(mem, g_mem, w_ckv, ckg)


def _mix_cross_kernel(x_ref, a1_ref, a4_ref, a16_ref, y_ref, wo_ref,
                      gc_ref, wq_ref, qg_ref, km_ref, vm_ref, wc_ref, wgu_ref, wd_ref,
                      h_ref, wgu_o, wd_o, acc_tok, st_tok, cat_a, cat_b, o_ref, *, tm):
    step = pl.program_id(0)
    wgu_o[...] = wgu_ref[...].astype(BF16)
    wd_o[...] = wd_ref[...].astype(BF16)

    def merge(cat_ref):
        for gi, (a_ref, d) in enumerate(((a4_ref, 4), (a16_ref, 16))):
            n = tm // d
            for r in range(d):
                dst = pl.ds(r, n, stride=d)
                st_tok[gi, dst, :] = a_ref[:, r * ATT_OUT_W + KV_W:(r + 1) * ATT_OUT_W].astype(F32)
                for h in range(ATT_HEADS):
                    c0 = r * ATT_OUT_W + h * HEAD_DIM
                    acc_tok[gi, h, dst, :] = a_ref[:, c0:c0 + HEAD_DIM].astype(F32)

        for i in range(tm // MERGE_CHUNK):
            rows = slice(i * MERGE_CHUNK, (i + 1) * MERGE_CHUNK)
            stats = [a1_ref[rows, KV_W:].astype(F32), st_tok[0, rows, :], st_tok[1, rows, :]]
            m_all = jnp.maximum(jnp.maximum(stats[0], stats[1]), stats[2])
            wts = [jnp.exp2(s - m_all) for s in stats]
            dens = [s + pltpu.roll(s, LANES - ATT_HEADS, 1) for s in stats]
            for h in range(ATT_HEADS):
                sl = slice(h * HEAD_DIM, (h + 1) * HEAD_DIM)
                accs = (a1_ref[rows, sl].astype(F32), acc_tok[0, h, rows, :], acc_tok[1, h, rows, :])
                num = jnp.zeros((MERGE_CHUNK, HEAD_DIM), F32)
                den = jnp.zeros((MERGE_CHUNK, 1), F32)
                for g in range(len(DILATED_PAIRS)):
                    w = wts[g][:, h:h + 1]
                    num = num + w * accs[g]
                    den = den + w * dens[g][:, ATT_HEADS + h:ATT_HEADS + h + 1]
                cat_ref[rows, sl] = (num / den).astype(BF16)
        cat_ref[:, KV_W:] = y_ref[...]

    def mix(cat_ref):
        n_mem = km_ref.shape[1]
        h1 = x_ref[...] + jnp.dot(cat_ref[...], wo_ref[...], preferred_element_type=F32)
        hn = _rms(h1, gc_ref[...]).astype(BF16)
        qc = jnp.dot(hn, wq_ref[...], preferred_element_type=F32)
        for h in range(X_HEADS):
            sl = slice(h * HEAD_DIM, (h + 1) * HEAD_DIM)
            qh = _rms(qc[:, sl], qg_ref[...] * SCORE_SCALE).astype(BF16)
            s = lax.dot_general(qh, km_ref[0, :, sl], (((1,), (1,)), ((), ())),
                                preferred_element_type=F32)
            p = jnp.exp2(s - jnp.max(s, axis=-1, keepdims=True)).astype(BF16)
            v_ones = jnp.concatenate([vm_ref[0, :, sl], jnp.ones((n_mem, HEAD_DIM), BF16)], axis=1)
            o_l = jnp.dot(p, v_ones, preferred_element_type=F32)
            o_ref[:, sl] = (o_l[:, :HEAD_DIM] / o_l[:, HEAD_DIM:]).astype(BF16)
        h_ref[...] = h1 + jnp.dot(o_ref[...], wc_ref[...], preferred_element_type=F32)

    @pl.when(step == 0)
    def _():
        cat_b[...] = jnp.zeros(cat_b.shape, BF16)

    @pl.when(step % 2 == 0)
    def _():
        merge(cat_a)
        mix(cat_b)

    @pl.when(step % 2 == 1)
    def _():
        merge(cat_b)
        mix(cat_a)


def _mix_cross(x2, branches, y, w_out, g_cross, w_cq, cqg, k_mem, v_mem, w_co, w_gate_up, w_down, seq):
    n_tok = x2.shape[0]
    tm = 512
    n_tiles = n_tok // tm
    tiles_per_seq = seq // tm
    n_mem = k_mem.shape[1]
    ahead = lambda i: (jnp.minimum(i, n_tiles - 1), 0)
    behind = lambda i: (jnp.maximum(i - 1, 0), 0)
    view = lambda d, w: pl.BlockSpec((tm // d, d * w), ahead)
    mem_blk = pl.BlockSpec((1, n_mem, X_W), lambda i: (jnp.maximum(i - 1, 0) // tiles_per_seq, 0, 0))

    def cast_blk(w, rows):
        n_blk = w.shape[0] // rows
        assert n_blk * rows == w.shape[0] and n_blk <= n_tiles + 1
        return pl.BlockSpec((rows, w.shape[1]), lambda i: (jnp.minimum(i, n_blk - 1), 0))

    cast_specs = [cast_blk(w_gate_up, 32), cast_blk(w_down, 128)]
    return pl.pallas_call(
        functools.partial(_mix_cross_kernel, tm=tm),
        grid=(n_tiles + 1,),
        in_specs=[pl.BlockSpec((tm, D_MODEL), behind)] + [view(d, ATT_OUT_W) for d in DILATIONS]
                 + [view(1, POOL_W), _resident((KV_W + POOL_W, D_MODEL)), _resident((1, D_MODEL)),
                    _resident((D_MODEL, X_W)), _resident((1, HEAD_DIM)), mem_blk, mem_blk,
                    _resident((X_W, D_MODEL))] + cast_specs,
        out_specs=[pl.BlockSpec((tm, D_MODEL), behind)] + cast_specs,
        out_shape=[jax.ShapeDtypeStruct((n_tok, D_MODEL), F32),
                   jax.ShapeDtypeStruct(w_gate_up.shape, BF16),
                   jax.ShapeDtypeStruct(w_down.shape, BF16)],
        scratch_shapes=[pltpu.VMEM((2, ATT_HEADS, tm, HEAD_DIM), F32),
                        pltpu.VMEM((2, tm, LANES), F32),
                        pltpu.VMEM((tm, KV_W + POOL_W), BF16),
                        pltpu.VMEM((tm, KV_W + POOL_W), BF16),
                        pltpu.VMEM((tm, X_W), BF16)],
        compiler_params=pltpu.CompilerParams(dimension_semantics=("arbitrary",),
                                             vmem_limit_bytes=VMEM_LIMIT),
        name="mix_cross",
    )(x2, *branches, y, w_out, g_cross, w_cq, cqg, k_mem, v_mem, w_co, w_gate_up, w_down)


def _ffn_kernel(h_ref, g_ref, wgu_ref, wd_ref, o_ref):
    h = h_ref[...]
    hn = _rms(h, g_ref[...]).astype(BF16)
    gate = jnp.dot(hn, wgu_ref[:, :D_FF], preferred_element_type=F32)
    up = jnp.dot(hn, wgu_ref[:, D_FF:], preferred_element_type=F32)
    act = (gate * jax.nn.sigmoid(gate) * up).astype(BF16)
    o_ref[...] = h + jnp.dot(act, wd_ref[...], preferred_element_type=F32)


def _ffn(h2, g_ffn, w_gate_up, w_down):
    n_tok = h2.shape[0]
    tm = 512
    tok = pl.BlockSpec((tm, D_MODEL), lambda i: (i, 0))
    return pl.pallas_call(
        _ffn_kernel,
        grid=(n_tok // tm,),
        in_specs=[tok, _resident((1, D_MODEL)), _resident((D_MODEL, 2 * D_FF)),
                  _resident((D_FF, D_MODEL))],
        out_specs=tok,
        out_shape=jax.ShapeDtypeStruct((n_tok, D_MODEL), F32),
        compiler_params=pltpu.CompilerParams(dimension_semantics=("parallel",),
                                             vmem_limit_bytes=VMEM_LIMIT),
        name="ffn",
    )(h2, g_ffn, w_gate_up, w_down)


def kernel(x, mem, positions, mix_norm_g, w_in, q_norm_g, k_norm_g, pool_w, pool_scale, w_out,
           cross_norm_g, mem_norm_g, w_cq, w_ckv, cq_norm_g, ck_norm_g, w_co,
           ffn_norm_g, w_gate_up, w_down):
    batch, seq, _ = x.shape
    depth = w_in.shape[0]
    row = lambda a: a.reshape(1, -1)
    h = x.reshape(batch * seq, D_MODEL)
    for layer in range(depth):
        cos, sin, w_in_l = _prep(positions, w_in[layer])
        *qkv_views, y, w_out_l, w_cq_l, w_ckv_l, w_co_l = _in_proj(
            h, row(mix_norm_g[layer]), w_in_l, row(q_norm_g[layer]), row(k_norm_g[layer]), cos, sin,
            pool_w[layer], row(pool_scale[layer]),
            (w_out[layer], w_cq[layer], w_ckv[layer], w_co[layer]), seq)
        branches = [_dilated_attn(qkv, batch, seq, d) for qkv, d in zip(qkv_views, DILATIONS)]
        k_mem, v_mem = _mem_kv(mem, row(mem_norm_g[layer]), w_ckv_l, row(ck_norm_g[layer]))
        h, w_gu_l, w_down_l = _mix_cross(
            h, branches, y, w_out_l, row(cross_norm_g[layer]), w_cq_l,
            row(cq_norm_g[layer]), k_mem, v_mem, w_co_l, w_gate_up[layer], w_down[layer], seq)
        h = _ffn(h, row(ffn_norm_g[layer]), w_gu_l, w_down_l)
    return h.reshape(batch, seq, D_MODEL)
```

```python
import functools
import math

import jax
import jax.numpy as jnp
from jax import lax
from jax.experimental import pallas as pl
from jax.experimental.pallas import tpu as pltpu

D_MODEL = 1024
HEAD_DIM = 128
ATT_HEADS = 4
DILATED_PAIRS = ((128, 1), (512, 4), (2048, 16))
DILATIONS = tuple(d for _, d in DILATED_PAIRS)
KV_W = ATT_HEADS * HEAD_DIM
POOL_WINDOWS = (2, 4, 8, 16)
POOL_W = len(POOL_WINDOWS) * HEAD_DIM
IN_W = (len(DILATED_PAIRS) + 2) * KV_W + POOL_W
ROT_DIM = HEAD_DIM // 4
ROT_HALF = ROT_DIM // 2
ROPE_THETA = 500000.0
X_HEADS = 4
X_W = X_HEADS * HEAD_DIM
D_FF = 2816
EPS = 1e-6
NEG_INF = -1e30
SCORE_SCALE = HEAD_DIM ** -0.5 * math.log2(math.e)
ATT_BLOCK = 128
ATT_STEP_QUERIES = 2048
LANES = 128
ATT_OUT_W = KV_W + LANES
SUBLANES = 8
BF16_TILE_ROWS = 2 * SUBLANES
ROT_PAIR_LANE = LANES // 2
ROW_CHUNK = 256
MERGE_CHUNK = 512
POOL_HALO = 32

F32 = jnp.float32
BF16 = jnp.bfloat16

VMEM_LIMIT = 56 * 1024 * 1024


def _rms(x, g):
    ms = jnp.mean(x * x, axis=-1, keepdims=True)
    return x * lax.rsqrt(ms + EPS) * g


def _resident(shape):
    nd = len(shape)
    return pl.BlockSpec(shape, lambda *_: (0,) * nd, pipeline_mode=pl.Buffered(1))


def _to_rotary_lanes(w):
    n_mid = ROT_PAIR_LANE - ROT_HALF
    lane = lax.broadcasted_iota(jnp.int32, w.shape, w.ndim - 1)
    keep = jnp.logical_or(lane < ROT_HALF, lane >= ROT_DIM + n_mid)
    from_low = jnp.logical_and(lane >= ROT_PAIR_LANE, lane < ROT_PAIR_LANE + ROT_HALF)
    moved = jnp.where(from_low, pltpu.roll(w, n_mid, w.ndim - 1),
                      pltpu.roll(w, LANES - ROT_HALF, w.ndim - 1))
    return jnp.where(keep, w, moved)


def _prep_kernel(pos_ref, invf_ref, win_ref, ctab_ref, stab_ref, win_o, *, n_qk_heads):
    ang = pos_ref[...] * invf_ref[...]
    cos = jnp.cos(ang)
    sin = jnp.sin(ang)
    n_rows = ang.shape[0]
    per_row = LANES // ROT_HALF
    lane = lax.broadcasted_iota(jnp.int32, (n_rows, LANES), 1)
    first = lane < ROT_HALF
    second = jnp.logical_and(lane >= ROT_PAIR_LANE, lane < ROT_PAIR_LANE + ROT_HALF)
    half = per_row // 2
    for j in range(half):
        shift = (LANES - ROT_HALF * j) % LANES
        c_lo = pltpu.roll(cos, shift, 1) if shift else cos
        s_lo = pltpu.roll(sin, shift, 1) if shift else sin
        c_hi = pltpu.roll(c_lo, ROT_PAIR_LANE, 1)
        s_hi = pltpu.roll(s_lo, ROT_PAIR_LANE, 1)
        for tok, (ca, cb, sa, sb) in ((j, (c_lo, c_hi, s_lo, s_hi)), (j + half, (c_hi, c_lo, s_hi, s_lo))):
            dst = pl.ds(tok, n_rows, stride=per_row)
            ctab_ref[dst, :] = jnp.where(first, ca, jnp.where(second, cb, 1.0))
            stab_ref[dst, :] = jnp.where(first, -sa, jnp.where(second, sb, 0.0))

    for h in range(n_qk_heads):
        sl = slice(h * HEAD_DIM, (h + 1) * HEAD_DIM)
        win_o[:, sl] = _to_rotary_lanes(win_ref[:, sl]).astype(BF16)
    rest = slice(n_qk_heads * HEAD_DIM, IN_W)
    win_o[:, rest] = win_ref[:, rest].astype(BF16)


def _cast_specs(weights, n_steps):
    specs = []
    for w in weights:
        rows = -(-w.shape[0] // (n_steps * BF16_TILE_ROWS)) * BF16_TILE_ROWS
        n_blk = w.shape[0] // rows
        assert n_blk * rows == w.shape[0] and n_blk <= n_steps
        specs.append(pl.BlockSpec((rows, w.shape[1]), lambda i, n_blk=n_blk: (jnp.minimum(i, n_blk - 1), 0)))
    return specs


def _prep(positions, w_in):
    n_tok = positions.size
    per_row = LANES // ROT_HALF
    n_steps = 4
    rows = n_tok // per_row // n_steps
    pos = jnp.broadcast_to(positions.reshape(-1, per_row, 1).astype(F32),
                           (n_tok // per_row, per_row, ROT_HALF)).reshape(n_tok // per_row, LANES)
    inv_freq = ROPE_THETA ** (-jnp.arange(0, ROT_DIM, 2, dtype=F32) / ROT_DIM)
    invf = jnp.tile(inv_freq, per_row).reshape(1, LANES)
    weights = (w_in,)
    row_blk = lambda w: pl.BlockSpec((w.shape[0] // n_steps, w.shape[1]), lambda i: (i, 0))
    tab = pl.BlockSpec((rows * per_row, LANES), lambda i: (i, 0))
    return pl.pallas_call(
        functools.partial(_prep_kernel, n_qk_heads=(len(DILATED_PAIRS) + 1) * ATT_HEADS),
        grid=(n_steps,),
        in_specs=[pl.BlockSpec((rows, LANES), lambda i: (i, 0)), pl.BlockSpec((1, LANES), lambda i: (0, 0))]
                 + [row_blk(w) for w in weights],
        out_specs=[tab, tab] + [row_blk(w) for w in weights],
        out_shape=[jax.ShapeDtypeStruct((n_tok, LANES), F32)] * 2
                  + [jax.ShapeDtypeStruct(w.shape, BF16) for w in weights],
        compiler_params=pltpu.CompilerParams(dimension_semantics=("parallel",),
                                             vmem_limit_bytes=VMEM_LIMIT),
        name="prep",
    )(pos, invf, *weights)


def _in_proj_kernel(x_ref, g_ref, w_ref, qg_ref, kg_ref, cos_ref, sin_ref, pw_ref, ps_ref,
                    wout_ref, wcq_ref, wckv_ref, wco_ref,
                    tok_ref, view4_ref, view16_ref, y_ref, wout_o, wcq_o, wckv_o, wco_o,
                    xn_ref, tab_ref, pa_ref, pb_ref, hist_ref, lvl_ref, stage_ref, stage4_ref,
                    *, tm, tiles_per_seq):
    step = pl.program_id(0)
    for src, dst in ((wout_ref, wout_o), (wcq_ref, wcq_o), (wckv_ref, wckv_o), (wco_ref, wco_o)):
        dst[...] = src[...].astype(BF16)
    rc = ROW_CHUNK
    n_rc = tm // rc

    def project(p_ref):
        for i in range(n_rc):
            rows = slice(i * rc, (i + 1) * rc)
            xn_ref[rows, :] = _rms(x_ref[rows, :], g_ref[...]).astype(BF16)
        for c in range(IN_W // KV_W):
            cols = slice(c * KV_W, (c + 1) * KV_W)
            p_ref[:, cols] = jnp.dot(xn_ref[...], w_ref[:, cols], preferred_element_type=F32)

    def finish(p_ref):
        for gi, (gain_ref, scale) in enumerate(((qg_ref, SCORE_SCALE), (kg_ref, 1.0))):
            gain = _to_rotary_lanes(gain_ref[...] * scale)
            tab_ref[2 * gi] = cos_ref[...] * gain
            tab_ref[2 * gi + 1] = sin_ref[...] * pltpu.roll(gain, ROT_PAIR_LANE, 1)

        def emit(c, gi, part, to_tok, to_v4, to_v16, a):
            for h in range(ATT_HEADS):
                src = slice(c * KV_W + h * HEAD_DIM, c * KV_W + (h + 1) * HEAD_DIM)
                for i in range(n_rc):
                    rows = slice(i * rc, (i + 1) * rc)
                    t = p_ref[rows, src]
                    if gi >= 0:
                        rs = lax.rsqrt(jnp.mean(t * t, axis=-1, keepdims=True) + EPS)
                        t = (t * tab_ref[2 * gi, rows, :]
                             + pltpu.roll(t, ROT_PAIR_LANE, 1) * tab_ref[2 * gi + 1, rows, :]) * rs
                    if to_tok:
                        c0 = part * KV_W + h * HEAD_DIM
                        tok_ref[rows, c0:c0 + HEAD_DIM] = t.astype(BF16)
                    if to_v4 or to_v16:
                        stage_ref[a, h, rows, :] = t
                if not (to_v4 or to_v16):
                    continue
                n4 = tm // 4
                for b in range(4):
                    t4 = stage_ref[a, h, pl.ds(b, n4, stride=4), :]
                    if to_v4:
                        c0 = (part * 4 + b) * KV_W + h * HEAD_DIM
                        view4_ref[:, c0:c0 + HEAD_DIM] = t4.astype(BF16)
                    if to_v16:
                        stage4_ref[a, h, b] = t4
                if to_v16:
                    for b in range(4):
                        for a4 in range(4):
                            c0 = (part * 16 + 4 * a4 + b) * KV_W + h * HEAD_DIM
                            view16_ref[:, c0:c0 + HEAD_DIM] = (
                                stage4_ref[a, h, b, pl.ds(a4, n4 // 4, stride=4), :].astype(BF16))

        emit(3, 1, 1, True, True, True, 0)
        emit(4, -1, 2, True, True, True, 1)
        emit(2, 0, 0, False, False, True, 2)
        emit(1, 0, 0, False, True, False, 3)
        emit(0, 0, 0, True, False, False, 0)

        seq_tile = jnp.maximum(step - 1, 0) % tiles_per_seq
        end = POOL_HALO + tm
        hist_ref[0:POOL_HALO, :] = jnp.where(seq_tile == 0, 0.0, 1.0) * hist_ref[tm:end, :]
        u_cols = slice(IN_W - POOL_W, IN_W)
        hist_ref[POOL_HALO:end, :] = p_ref[:, u_cols]
        row16 = lax.broadcasted_iota(jnp.int32, (2 * SUBLANES, HEAD_DIM), 0)
        for g, w in enumerate(POOL_WINDOWS):
            sl = slice(g * HEAD_DIM, (g + 1) * HEAD_DIM)
            src, start, shift = hist_ref, SUBLANES, 1
            while True:
                cols = sl if src is hist_ref else slice(None)
                tot = src[start:end, cols] + src[start - shift:end - shift, cols]
                shift *= 2
                if shift == w:
                    break
                lvl_ref[g, start:end, :] = tot
                src, start = lvl_ref.at[g], start + SUBLANES
            tot = tot[POOL_HALO - start:, :]
            n_first = jnp.maximum(jnp.minimum(row16 + 1, w), jnp.where(seq_tile == 0, 0, w))
            inv_first = 1.0 / n_first.astype(F32)
            ug = hist_ref[POOL_HALO:end, sl]
            d = jnp.concatenate([tot[:2 * SUBLANES] * inv_first, tot[2 * SUBLANES:] * (1.0 / w)],
                                axis=0) - ug
            yg = jnp.dot(d.astype(BF16), pw_ref[g].astype(BF16),
                         preferred_element_type=F32) * ps_ref[:, sl]
            y_ref[:, sl] = yg.astype(BF16)

    last = pl.num_programs(0) - 1

    @pl.when(step == 0)
    def _():
        hist_ref[...] = jnp.zeros(hist_ref.shape, F32)
        project(pa_ref)

    @pl.when(jnp.logical_and(step % 2 == 0, jnp.logical_and(step > 0, step < last)))
    def _():
        project(pa_ref)
        finish(pb_ref)

    @pl.when(step % 2 == 1)
    def _():
        project(pb_ref)
        finish(pa_ref)

    @pl.when(step == last)
    def _():
        finish(pb_ref)


def _in_proj(x2, g_mix, w_in, qg, kg, cos, sin, pool_w, pool_scale, later_weights, seq):
    n_tok = x2.shape[0]
    tm = 512
    n_tiles = n_tok // tm
    assert n_tiles % 2 == 0
    tiles_per_seq = seq // tm
    ahead = lambda i: (jnp.minimum(i, n_tiles - 1), 0)
    behind = lambda i: (jnp.maximum(i - 1, 0), 0)
    out_dw = [(d, 3 * KV_W) for d in DILATIONS] + [(1, POOL_W)]
    view = lambda d, w: pl.BlockSpec((tm // d, d * w), behind)
    view_sds = lambda d, w: jax.ShapeDtypeStruct((n_tok // d, d * w), BF16)
    cast_specs = _cast_specs(later_weights, n_tiles + 1)
    return pl.pallas_call(
        functools.partial(_in_proj_kernel, tm=tm, tiles_per_seq=tiles_per_seq),
        grid=(n_tiles + 1,),
        in_specs=[pl.BlockSpec((tm, D_MODEL), ahead), _resident((1, D_MODEL)),
                  _resident((D_MODEL, IN_W)), _resident((1, HEAD_DIM)), _resident((1, HEAD_DIM)),
                  pl.BlockSpec((tm, LANES), behind), pl.BlockSpec((tm, LANES), behind),
                  _resident((len(POOL_WINDOWS), HEAD_DIM, HEAD_DIM)), _resident((1, POOL_W))]
                 + cast_specs,
        out_specs=[view(d, w) for d, w in out_dw] + cast_specs,
        out_shape=[view_sds(d, w) for d, w in out_dw]
                  + [jax.ShapeDtypeStruct(w.shape, BF16) for w in later_weights],
        scratch_shapes=[pltpu.VMEM((tm, D_MODEL), BF16),
                        pltpu.VMEM((4, tm, LANES), F32),
                        pltpu.VMEM((tm, IN_W), F32),
                        pltpu.VMEM((tm, IN_W), F32),
                        pltpu.VMEM((POOL_HALO + tm, POOL_W), F32),
                        pltpu.VMEM((len(POOL_WINDOWS), POOL_HALO + tm, HEAD_DIM), F32),
                        pltpu.VMEM((4, ATT_HEADS, tm, HEAD_DIM), F32),
                        pltpu.VMEM((3, ATT_HEADS, 4, tm // 4, HEAD_DIM), F32)],
        compiler_params=pltpu.CompilerParams(dimension_semantics=("arbitrary",),
                                             vmem_limit_bytes=VMEM_LIMIT),
        name="in_proj",
    )(x2, g_mix, w_in, qg, kg, cos, sin, pool_w, pool_scale, *later_weights)


def _dilated_attn_kernel(q_ref, k_ref, kh_ref, v_ref, vh_ref, o_ref, *, tl, rb):
    n_sub = tl // ATT_BLOCK
    qi = lax.broadcasted_iota(jnp.int32, (ATT_BLOCK, 2 * ATT_BLOCK), 0)
    kj = lax.broadcasted_iota(jnp.int32, (ATT_BLOCK, 2 * ATT_BLOCK), 1)
    delta = qi + ATT_BLOCK - kj
    in_band = jnp.logical_and(delta >= 0, delta <= ATT_BLOCK)
    bias = jnp.where(in_band, 0.0, NEG_INF).astype(F32)
    n_missing = jnp.where(pl.program_id(2) == 0, ATT_BLOCK, 0)
    bias_first = jnp.where(kj < n_missing, NEG_INF, bias)
    lane = lax.broadcasted_iota(jnp.int32, (ATT_BLOCK, LANES), 1)
    key_lane = lax.broadcasted_iota(jnp.int32, (2 * ATT_BLOCK, HEAD_DIM), 1)
    sum_blk = [jnp.where(jnp.logical_or(key_lane == ATT_HEADS + h, key_lane == 2 * ATT_HEADS + h),
                         1.0, 0.0).astype(BF16) for h in range(ATT_HEADS)]

    for r in range(rb):
        for i in range(n_sub):
            rows = slice(i * ATT_BLOCK, (i + 1) * ATT_BLOCK)
            rows2 = slice((i - 1) * ATT_BLOCK, (i + 1) * ATT_BLOCK)
            shift = jnp.zeros((ATT_BLOCK, LANES), F32)
            den = jnp.zeros((ATT_BLOCK, LANES), F32)
            for h in range(ATT_HEADS):
                c0 = r * KV_W + h * HEAD_DIM
                sl = slice(c0, c0 + HEAD_DIM)
                if i == 0:
                    k_blk = jnp.concatenate([kh_ref[0, :, sl], k_ref[0, rows, sl]], axis=0)
                    v_blk = jnp.concatenate([vh_ref[0, :, sl], v_ref[0, rows, sl]], axis=0)
                    b = bias_first
                else:
                    k_blk, v_blk, b = k_ref[0, rows2, sl], v_ref[0, rows2, sl], bias
                s = lax.dot_general(q_ref[0, rows, sl], k_blk, (((1,), (1,)), ((), ())),
                                    preferred_element_type=F32) + b
                m = jnp.max(s, axis=-1, keepdims=True).astype(BF16).astype(F32)
                p = jnp.exp2(s - m).astype(BF16)
                acc_l = jnp.dot(p, jnp.concatenate([v_blk, sum_blk[h]], axis=1),
                                preferred_element_type=F32)
                o0 = r * ATT_OUT_W + h * HEAD_DIM
                o_ref[0, rows, o0:o0 + HEAD_DIM] = acc_l[:, :HEAD_DIM].astype(BF16)
                den = den + acc_l[:, HEAD_DIM:]
                shift = jnp.where(lane == h, m, shift)
            den_hi = den.astype(BF16).astype(F32)
            st = shift + jnp.where(lane < 2 * ATT_HEADS, den_hi, den - den_hi)
            o_ref[0, rows, r * ATT_OUT_W + KV_W:(r + 1) * ATT_OUT_W] = st.astype(BF16)


def _dilated_attn(qkv, batch, seq, dilation):
    sub_len = seq // dilation
    tl = min(sub_len, ATT_STEP_QUERIES)
    rb = min(dilation, ATT_STEP_QUERIES // tl)
    n_sub = tl // ATT_BLOCK
    n_col = dilation // rb
    qkv3 = qkv.reshape(batch, sub_len, qkv.shape[1])
    cur = lambda part: pl.BlockSpec((1, tl, rb * KV_W), lambda b, r, i: (b, i, part * n_col + r))
    halo = lambda part: pl.BlockSpec(
        (1, ATT_BLOCK, rb * KV_W), lambda b, r, i: (b, jnp.maximum(i * n_sub - 1, 0), part * n_col + r))
    out = pl.pallas_call(
        functools.partial(_dilated_attn_kernel, tl=tl, rb=rb),
        grid=(batch, n_col, sub_len // tl),
        in_specs=[cur(0), cur(1), halo(1), cur(2), halo(2)],
        out_specs=pl.BlockSpec((1, tl, rb * ATT_OUT_W), lambda b, r, i: (b, i, r)),
        out_shape=jax.ShapeDtypeStruct((batch, sub_len, dilation * ATT_OUT_W), BF16),
        compiler_params=pltpu.CompilerParams(
            dimension_semantics=("parallel", "parallel", "parallel"), vmem_limit_bytes=VMEM_LIMIT),
        name=f"dilated_attn_d{dilation}",
    )(qkv3, qkv3, qkv3, qkv3, qkv3)
    return out.reshape(batch * sub_len, dilation * ATT_OUT_W)


def _mem_kv_kernel(mem_ref, g_ref, w_ref, kg_ref, k_ref, v_ref):
    mn = _rms(mem_ref[0], g_ref[...]).astype(BF16)
    kv = jnp.dot(mn, w_ref[...], preferred_element_type=F32)
    for h in range(X_HEADS):
        sl = slice(h * HEAD_DIM, (h + 1) * HEAD_DIM)
        k_ref[0, :, sl] = _rms(kv[:, sl], kg_ref[...]).astype(BF16)
    v_ref[0] = kv[:, X_W:].astype(BF16)


def _mem_kv(mem, g_mem, w_ckv, ckg):
    batch, n_mem, _ = mem.shape
    out = jax.ShapeDtypeStruct((batch, n_mem, X_W), BF16)
    blk = pl.BlockSpec((1, n_mem, X_W), lambda b: (b, 0, 0))
    return pl.pallas_call(
        _mem_kv_kernel,
        grid=(batch,),
        in_specs=[pl.BlockSpec((1, n_mem, D_MODEL), lambda b: (b, 0, 0)), _resident((1, D_MODEL)),
                  _resident((D_MODEL, 2 * X_W)), _resident((1, HEAD_DIM))],
        out_specs=[blk, blk],
        out_shape=[out, out],
        compiler_params=pltpu.CompilerParams(dimension_semantics=("parallel",),
                                             vmem_limit_bytes=VMEM_LIMIT),
        name="mem_kv",
    )(mem, g_mem, w_ckv, ckg)


def _mix_cross_kernel(x_ref, a1_ref, a4_ref, a16_ref, y_ref, wo_ref,
                      gc_ref, wq_ref, qg_ref, km_ref, vm_ref, wc_ref, wgu_ref, wd_ref,
                      h_ref, wgu_o, wd_o, acc_tok, st_tok, cat_a, cat_b, o_ref, *, tm):
    step = pl.program_id(0)
    wgu_o[...] = wgu_ref[...].astype(BF16)
    wd_o[...] = wd_ref[...].astype(BF16)

    def merge(cat_ref):
        for gi, (a_ref, d) in enumerate(((a4_ref, 4), (a16_ref, 16))):
            n = tm // d
            for r in range(d):
                dst = pl.ds(r, n, stride=d)
                st_tok[gi, dst, :] = a_ref[:, r * ATT_OUT_W + KV_W:(r + 1) * ATT_OUT_W].astype(F32)
                for h in range(ATT_HEADS):
                    c0 = r * ATT_OUT_W + h * HEAD_DIM
                    acc_tok[gi, h, dst, :] = a_ref[:, c0:c0 + HEAD_DIM].astype(F32)

        for i in range(tm // MERGE_CHUNK):
            rows = slice(i * MERGE_CHUNK, (i + 1) * MERGE_CHUNK)
            stats = [a1_ref[rows, KV_W:].astype(F32), st_tok[0, rows, :], st_tok[1, rows, :]]
            m_all = jnp.maximum(jnp.maximum(stats[0], stats[1]), stats[2])
            wts = [jnp.exp2(s - m_all) for s in stats]
            dens = [pltpu.roll(s, LANES - ATT_HEADS, 1) + pltpu.roll(s, LANES - 2 * ATT_HEADS, 1)
                    for s in stats]
            total = wts[0] * dens[0] + wts[1] * dens[1] + wts[2] * dens[2]
            coef = [w / total for w in wts]
            for h in range(ATT_HEADS):
                sl = slice(h * HEAD_DIM, (h + 1) * HEAD_DIM)
                accs = (a1_ref[rows, sl].astype(F32), acc_tok[0, h, rows, :], acc_tok[1, h, rows, :])
                num = jnp.zeros((MERGE_CHUNK, HEAD_DIM), F32)
                for g in range(len(DILATED_PAIRS)):
                    num = num + coef[g][:, h:h + 1] * accs[g]
                cat_ref[rows, sl] = num.astype(BF16)
        cat_ref[:, KV_W:] = y_ref[...]

    def mix(cat_ref):
        n_mem = km_ref.shape[1]
        h1 = x_ref[...] + jnp.dot(cat_ref[...], wo_ref[...], preferred_element_type=F32)
        hn = _rms(h1, gc_ref[...]).astype(BF16)
        qc = jnp.dot(hn, wq_ref[...], preferred_element_type=F32)
        for h in range(X_HEADS):
            sl = slice(h * HEAD_DIM, (h + 1) * HEAD_DIM)
            qh = _rms(qc[:, sl], qg_ref[...] * SCORE_SCALE).astype(BF16)
            s = lax.dot_general(qh, km_ref[0, :, sl], (((1,), (1,)), ((), ())),
                                preferred_element_type=F32)
            p = jnp.exp2(s - jnp.max(s, axis=-1, keepdims=True)).astype(BF16)
            v_ones = jnp.concatenate([vm_ref[0, :, sl], jnp.ones((n_mem, HEAD_DIM), BF16)], axis=1)
            o_l = jnp.dot(p, v_ones, preferred_element_type=F32)
            o_ref[:, sl] = (o_l[:, :HEAD_DIM] / o_l[:, HEAD_DIM:]).astype(BF16)
        h_ref[...] = h1 + jnp.dot(o_ref[...], wc_ref[...], preferred_element_type=F32)

    last = pl.num_programs(0) - 1

    @pl.when(step == 0)
    def _():
        merge(cat_a)

    @pl.when(jnp.logical_and(step % 2 == 0, jnp.logical_and(step > 0, step < last)))
    def _():
        merge(cat_a)
        mix(cat_b)

    @pl.when(step % 2 == 1)
    def _():
        merge(cat_b)
        mix(cat_a)

    @pl.when(step == last)
    def _():
        mix(cat_b)


def _mix_cross(x2, branches, y, w_out, g_cross, w_cq, cqg, k_mem, v_mem, w_co, w_gate_up, w_down, seq):
    n_tok = x2.shape[0]
    tm = 512
    n_tiles = n_tok // tm
    assert n_tiles % 2 == 0
    tiles_per_seq = seq // tm
    n_mem = k_mem.shape[1]
    ahead = lambda i: (jnp.minimum(i, n_tiles - 1), 0)
    behind = lambda i: (jnp.maximum(i - 1, 0), 0)
    view = lambda d, w: pl.BlockSpec((tm // d, d * w), ahead)
    mem_blk = pl.BlockSpec((1, n_mem, X_W), lambda i: (jnp.maximum(i - 1, 0) // tiles_per_seq, 0, 0))

    def cast_blk(w, rows):
        n_blk = w.shape[0] // rows
        assert n_blk * rows == w.shape[0] and n_blk <= n_tiles + 1
        return pl.BlockSpec((rows, w.shape[1]), lambda i: (jnp.minimum(i, n_blk - 1), 0))

    cast_specs = [cast_blk(w_gate_up, 32), cast_blk(w_down, 128)]
    return pl.pallas_call(
        functools.partial(_mix_cross_kernel, tm=tm),
        grid=(n_tiles + 1,),
        in_specs=[pl.BlockSpec((tm, D_MODEL), behind)] + [view(d, ATT_OUT_W) for d in DILATIONS]
                 + [view(1, POOL_W), _resident((KV_W + POOL_W, D_MODEL)), _resident((1, D_MODEL)),
                    _resident((D_MODEL, X_W)), _resident((1, HEAD_DIM)), mem_blk, mem_blk,
                    _resident((X_W, D_MODEL))] + cast_specs,
        out_specs=[pl.BlockSpec((tm, D_MODEL), behind)] + cast_specs,
        out_shape=[jax.ShapeDtypeStruct((n_tok, D_MODEL), F32),
                   jax.ShapeDtypeStruct(w_gate_up.shape, BF16),
                   jax.ShapeDtypeStruct(w_down.shape, BF16)],
        scratch_shapes=[pltpu.VMEM((2, ATT_HEADS, tm, HEAD_DIM), F32),
                        pltpu.VMEM((2, tm, LANES), F32),
                        pltpu.VMEM((tm, KV_W + POOL_W), BF16),
                        pltpu.VMEM((tm, KV_W + POOL_W), BF16),
                        pltpu.VMEM((tm, X_W), BF16)],
        compiler_params=pltpu.CompilerParams(dimension_semantics=("arbitrary",),
                                             vmem_limit_bytes=VMEM_LIMIT),
        name="mix_cross",
    )(x2, *branches, y, w_out, g_cross, w_cq, cqg, k_mem, v_mem, w_co, w_gate_up, w_down)


def _ffn_kernel(h_ref, g_ref, wgu_ref, wd_ref, o_ref):
    h = h_ref[...]
    hn = _rms(h, g_ref[...]).astype(BF16)
    gate = jnp.dot(hn, wgu_ref[:, :D_FF], preferred_element_type=F32)
    up = jnp.dot(hn, wgu_ref[:, D_FF:], preferred_element_type=F32)
    act = (gate * jax.nn.sigmoid(gate) * up).astype(BF16)
    o_ref[...] = h + jnp.dot(act, wd_ref[...], preferred_element_type=F32)


def _ffn(h2, g_ffn, w_gate_up, w_down):
    n_tok = h2.shape[0]
    tm = 512
    tok = pl.BlockSpec((tm, D_MODEL), lambda i: (i, 0))
    return pl.pallas_call(
        _ffn_kernel,
        grid=(n_tok // tm,),
        in_specs=[tok, _resident((1, D_MODEL)), _resident((D_MODEL, 2 * D_FF)),
                  _resident((D_FF, D_MODEL))],
        out_specs=tok,
        out_shape=jax.ShapeDtypeStruct((n_tok, D_MODEL), F32),
        compiler_params=pltpu.CompilerParams(dimension_semantics=("parallel",),
                                             vmem_limit_bytes=VMEM_LIMIT),
        name="ffn",
    )(h2, g_ffn, w_gate_up, w_down)


def kernel(x, mem, positions, mix_norm_g, w_in, q_norm_g, k_norm_g, pool_w, pool_scale, w_out,
           cross_norm_g, mem_norm_g, w_cq, w_ckv, cq_norm_g, ck_norm_g, w_co,
           ffn_norm_g, w_gate_up, w_down):
    batch, seq, _ = x.shape
    depth = w_in.shape[0]
    row = lambda a: a.reshape(1, -1)
    h = x.reshape(batch * seq, D_MODEL)
    for layer in range(depth):
        cos, sin, w_in_l = _prep(positions, w_in[layer])
        *qkv_views, y, w_out_l, w_cq_l, w_ckv_l, w_co_l = _in_proj(
            h, row(mix_norm_g[layer]), w_in_l, row(q_norm_g[layer]), row(k_norm_g[layer]), cos, sin,
            pool_w[layer], row(pool_scale[layer]),
            (w_out[layer], w_cq[layer], w_ckv[layer], w_co[layer]), seq)
        branches = [_dilated_attn(qkv, batch, seq, d) for qkv, d in zip(qkv_views, DILATIONS)]
        k_mem, v_mem = _mem_kv(mem, row(mem_norm_g[layer]), w_ckv_l, row(ck_norm_g[layer]))
        h, w_gu_l, w_down_l = _mix_cross(
            h, branches, y, w_out_l, row(cross_norm_g[layer]), w_cq_l,
            row(cq_norm_g[layer]), k_mem, v_mem, w_co_l, w_gate_up[layer], w_down[layer], seq)
        h = _ffn(h, row(ffn_norm_g[layer]), w_gu_l, w_down_l)
    return h.reshape(batch, seq, D_MODEL)
```

```python
import functools
import math

import jax
import jax.numpy as jnp
from jax import lax
from jax.experimental import pallas as pl
from jax.experimental.pallas import tpu as pltpu

D_MODEL = 1024
HEAD_DIM = 128
ATT_HEADS = 4
DILATED_PAIRS = ((128, 1), (512, 4), (2048, 16))
DILATIONS = tuple(d for _, d in DILATED_PAIRS)
KV_W = ATT_HEADS * HEAD_DIM
POOL_WINDOWS = (2, 4, 8, 16)
POOL_W = len(POOL_WINDOWS) * HEAD_DIM
IN_W = (len(DILATED_PAIRS) + 2) * KV_W + POOL_W
ROT_DIM = HEAD_DIM // 4
ROT_HALF = ROT_DIM // 2
ROPE_THETA = 500000.0
X_HEADS = 4
X_W = X_HEADS * HEAD_DIM
D_FF = 2816
EPS = 1e-6
NEG_INF = -1e30
SCORE_SCALE = HEAD_DIM ** -0.5 * math.log2(math.e)
ATT_BLOCK = 128
ATT_STEP_QUERIES = 2048
LANES = 128
ATT_OUT_W = KV_W + LANES
SUBLANES = 8
BF16_TILE_ROWS = 2 * SUBLANES
ROT_PAIR_LANE = LANES // 2
TOKEN_TILE = 512
ROW_CHUNK = 256
MERGE_CHUNK = 512
POOL_HALO = 32

F32 = jnp.float32
BF16 = jnp.bfloat16

VMEM_LIMIT = 56 * 1024 * 1024


def _rms(x, g):
    ms = jnp.mean(x * x, axis=-1, keepdims=True)
    return x * lax.rsqrt(ms + EPS) * g


def _resident(shape):
    nd = len(shape)
    return pl.BlockSpec(shape, lambda *_: (0,) * nd, pipeline_mode=pl.Buffered(1))


def _to_rotary_lanes(w):
    n_mid = ROT_PAIR_LANE - ROT_HALF
    lane = lax.broadcasted_iota(jnp.int32, w.shape, w.ndim - 1)
    keep = jnp.logical_or(lane < ROT_HALF, lane >= ROT_DIM + n_mid)
    from_low = jnp.logical_and(lane >= ROT_PAIR_LANE, lane < ROT_PAIR_LANE + ROT_HALF)
    moved = jnp.where(from_low, pltpu.roll(w, n_mid, w.ndim - 1),
                      pltpu.roll(w, LANES - ROT_HALF, w.ndim - 1))
    return jnp.where(keep, w, moved)


def _prep_kernel(pos_ref, invf_ref, win_ref, ctab_ref, stab_ref, win_o, *, n_qk_heads):
    ang = pos_ref[...] * invf_ref[...]
    cos = jnp.cos(ang)
    sin = jnp.sin(ang)
    n_rows = ang.shape[0]
    per_row = LANES // ROT_HALF
    lane = lax.broadcasted_iota(jnp.int32, (n_rows, LANES), 1)
    first = lane < ROT_HALF
    second = jnp.logical_and(lane >= ROT_PAIR_LANE, lane < ROT_PAIR_LANE + ROT_HALF)
    half = per_row // 2
    for j in range(half):
        shift = (LANES - ROT_HALF * j) % LANES
        c_lo = pltpu.roll(cos, shift, 1) if shift else cos
        s_lo = pltpu.roll(sin, shift, 1) if shift else sin
        c_hi = pltpu.roll(c_lo, ROT_PAIR_LANE, 1)
        s_hi = pltpu.roll(s_lo, ROT_PAIR_LANE, 1)
        for tok, (ca, cb, sa, sb) in ((j, (c_lo, c_hi, s_lo, s_hi)), (j + half, (c_hi, c_lo, s_hi, s_lo))):
            dst = pl.ds(tok, n_rows, stride=per_row)
            ctab_ref[dst, :] = jnp.where(first, ca, jnp.where(second, cb, 1.0))
            stab_ref[dst, :] = jnp.where(first, -sa, jnp.where(second, sb, 0.0))

    for h in range(n_qk_heads):
        sl = slice(h * HEAD_DIM, (h + 1) * HEAD_DIM)
        win_o[:, sl] = _to_rotary_lanes(win_ref[:, sl]).astype(BF16)
    rest = slice(n_qk_heads * HEAD_DIM, IN_W)
    win_o[:, rest] = win_ref[:, rest].astype(BF16)


def _cast_specs(weights, n_steps):
    specs = []
    for w in weights:
        rows = -(-w.shape[0] // (n_steps * BF16_TILE_ROWS)) * BF16_TILE_ROWS
        n_blk = w.shape[0] // rows
        assert n_blk * rows == w.shape[0] and n_blk <= n_steps
        specs.append(pl.BlockSpec((rows, w.shape[1]), lambda i, n_blk=n_blk: (jnp.minimum(i, n_blk - 1), 0)))
    return specs


def _prep(positions, w_in):
    n_tok = positions.size
    per_row = LANES // ROT_HALF
    n_steps = 4
    rows = n_tok // per_row // n_steps
    pos = jnp.broadcast_to(positions.reshape(-1, per_row, 1).astype(F32),
                           (n_tok // per_row, per_row, ROT_HALF)).reshape(n_tok // per_row, LANES)
    inv_freq = ROPE_THETA ** (-jnp.arange(0, ROT_DIM, 2, dtype=F32) / ROT_DIM)
    invf = jnp.tile(inv_freq, per_row).reshape(1, LANES)
    weights = (w_in,)
    row_blk = lambda w: pl.BlockSpec((w.shape[0] // n_steps, w.shape[1]), lambda i: (i, 0))
    tab = pl.BlockSpec((rows * per_row, LANES), lambda i: (i, 0))
    return pl.pallas_call(
        functools.partial(_prep_kernel, n_qk_heads=(len(DILATED_PAIRS) + 1) * ATT_HEADS),
        grid=(n_steps,),
        in_specs=[pl.BlockSpec((rows, LANES), lambda i: (i, 0)), pl.BlockSpec((1, LANES), lambda i: (0, 0))]
                 + [row_blk(w) for w in weights],
        out_specs=[tab, tab] + [row_blk(w) for w in weights],
        out_shape=[jax.ShapeDtypeStruct((n_tok, LANES), F32)] * 2
                  + [jax.ShapeDtypeStruct(w.shape, BF16) for w in weights],
        compiler_params=pltpu.CompilerParams(dimension_semantics=("parallel",),
                                             vmem_limit_bytes=VMEM_LIMIT),
        name="prep",
    )(pos, invf, *weights)


def _in_proj_kernel(x_ref, g_ref, w_ref, qg_ref, kg_ref, cos_ref, sin_ref, pw_ref, ps_ref,
                    wout_ref, wcq_ref, wckv_ref, wco_ref,
                    tok_ref, view4_ref, view16_ref, y_ref, wout_o, wcq_o, wckv_o, wco_o,
                    xn_ref, tab_ref, pa_ref, pb_ref, hist_ref, lvl_ref, stage_ref, stage4_ref,
                    *, tm, tiles_per_seq):
    step = pl.program_id(0)
    for src, dst in ((wout_ref, wout_o), (wcq_ref, wcq_o), (wckv_ref, wckv_o), (wco_ref, wco_o)):
        dst[...] = src[...].astype(BF16)
    rc = ROW_CHUNK
    n_rc = tm // rc

    def project(p_ref):
        for i in range(n_rc):
            rows = slice(i * rc, (i + 1) * rc)
            xn_ref[rows, :] = _rms(x_ref[rows, :], g_ref[...]).astype(BF16)
        for c in range(IN_W // KV_W):
            cols = slice(c * KV_W, (c + 1) * KV_W)
            p_ref[:, cols] = jnp.dot(xn_ref[...], w_ref[:, cols], preferred_element_type=F32)

    def finish(p_ref):
        for gi, (gain_ref, scale) in enumerate(((qg_ref, SCORE_SCALE), (kg_ref, 1.0))):
            gain = _to_rotary_lanes(gain_ref[...] * scale)
            tab_ref[2 * gi] = cos_ref[...] * gain
            tab_ref[2 * gi + 1] = sin_ref[...] * pltpu.roll(gain, ROT_PAIR_LANE, 1)

        def emit(c, gi, part, to_tok, to_v4, to_v16, a):
            for h in range(ATT_HEADS):
                src = slice(c * KV_W + h * HEAD_DIM, c * KV_W + (h + 1) * HEAD_DIM)
                for i in range(n_rc):
                    rows = slice(i * rc, (i + 1) * rc)
                    t = p_ref[rows, src]
                    if gi >= 0:
                        rs = lax.rsqrt(jnp.mean(t * t, axis=-1, keepdims=True) + EPS)
                        t = (t * tab_ref[2 * gi, rows, :]
                             + pltpu.roll(t, ROT_PAIR_LANE, 1) * tab_ref[2 * gi + 1, rows, :]) * rs
                    if to_tok:
                        c0 = part * KV_W + h * HEAD_DIM
                        tok_ref[rows, c0:c0 + HEAD_DIM] = t.astype(BF16)
                    if to_v4 or to_v16:
                        stage_ref[a, h, rows, :] = t
                if not (to_v4 or to_v16):
                    continue
                n4 = tm // 4
                for b in range(4):
                    t4 = stage_ref[a, h, pl.ds(b, n4, stride=4), :]
                    if to_v4:
                        c0 = (part * 4 + b) * KV_W + h * HEAD_DIM
                        view4_ref[:, c0:c0 + HEAD_DIM] = t4.astype(BF16)
                    if to_v16:
                        stage4_ref[a, h, b] = t4
                if to_v16:
                    for b in range(4):
                        for a4 in range(4):
                            c0 = (part * 16 + 4 * a4 + b) * KV_W + h * HEAD_DIM
                            view16_ref[:, c0:c0 + HEAD_DIM] = (
                                stage4_ref[a, h, b, pl.ds(a4, n4 // 4, stride=4), :].astype(BF16))

        emit(3, 1, 1, True, True, True, 0)
        emit(4, -1, 2, True, True, True, 1)
        emit(2, 0, 0, False, False, True, 2)
        emit(1, 0, 0, False, True, False, 3)
        emit(0, 0, 0, True, False, False, 0)

        seq_tile = jnp.maximum(step - 1, 0) % tiles_per_seq
        end = POOL_HALO + tm
        hist_ref[0:POOL_HALO, :] = jnp.where(seq_tile == 0, 0.0, 1.0) * hist_ref[tm:end, :]
        u_cols = slice(IN_W - POOL_W, IN_W)
        hist_ref[POOL_HALO:end, :] = p_ref[:, u_cols]
        row16 = lax.broadcasted_iota(jnp.int32, (2 * SUBLANES, HEAD_DIM), 0)
        for g, w in enumerate(POOL_WINDOWS):
            sl = slice(g * HEAD_DIM, (g + 1) * HEAD_DIM)
            src, start, shift = hist_ref, SUBLANES, 1
            while True:
                cols = sl if src is hist_ref else slice(None)
                tot = src[start:end, cols] + src[start - shift:end - shift, cols]
                shift *= 2
                if shift == w:
                    break
                lvl_ref[g, start:end, :] = tot
                src, start = lvl_ref.at[g], start + SUBLANES
            tot = tot[POOL_HALO - start:, :]
            n_first = jnp.maximum(jnp.minimum(row16 + 1, w), jnp.where(seq_tile == 0, 0, w))
            inv_first = 1.0 / n_first.astype(F32)
            ug = hist_ref[POOL_HALO:end, sl]
            d = jnp.concatenate([tot[:2 * SUBLANES] * inv_first, tot[2 * SUBLANES:] * (1.0 / w)],
                                axis=0) - ug
            yg = jnp.dot(d.astype(BF16), pw_ref[g].astype(BF16),
                         preferred_element_type=F32) * ps_ref[:, sl]
            y_ref[:, sl] = yg.astype(BF16)

    last = pl.num_programs(0) - 1

    @pl.when(step == 0)
    def _():
        hist_ref[...] = jnp.zeros(hist_ref.shape, F32)
        project(pa_ref)

    @pl.when(jnp.logical_and(step % 2 == 0, jnp.logical_and(step > 0, step < last)))
    def _():
        project(pa_ref)
        finish(pb_ref)

    @pl.when(step % 2 == 1)
    def _():
        project(pb_ref)
        finish(pa_ref)

    @pl.when(step == last)
    def _():
        finish(pb_ref)


def _in_proj(x2, g_mix, w_in, qg, kg, cos, sin, pool_w, pool_scale, later_weights, seq):
    n_tok = x2.shape[0]
    tm = TOKEN_TILE
    n_tiles = n_tok // tm
    assert n_tiles % 2 == 0
    tiles_per_seq = seq // tm
    ahead = lambda i: (jnp.minimum(i, n_tiles - 1), 0)
    behind = lambda i: (jnp.maximum(i - 1, 0), 0)
    out_dw = [(d, 3 * KV_W) for d in DILATIONS] + [(1, POOL_W)]
    view = lambda d, w: pl.BlockSpec((tm // d, d * w), behind)
    view_sds = lambda d, w: jax.ShapeDtypeStruct((n_tok // d, d * w), BF16)
    cast_specs = _cast_specs(later_weights, n_tiles + 1)
    return pl.pallas_call(
        functools.partial(_in_proj_kernel, tm=tm, tiles_per_seq=tiles_per_seq),
        grid=(n_tiles + 1,),
        in_specs=[pl.BlockSpec((tm, D_MODEL), ahead), _resident((1, D_MODEL)),
                  _resident((D_MODEL, IN_W)), _resident((1, HEAD_DIM)), _resident((1, HEAD_DIM)),
                  pl.BlockSpec((tm, LANES), behind), pl.BlockSpec((tm, LANES), behind),
                  _resident((len(POOL_WINDOWS), HEAD_DIM, HEAD_DIM)), _resident((1, POOL_W))]
                 + cast_specs,
        out_specs=[view(d, w) for d, w in out_dw] + cast_specs,
        out_shape=[view_sds(d, w) for d, w in out_dw]
                  + [jax.ShapeDtypeStruct(w.shape, BF16) for w in later_weights],
        scratch_shapes=[pltpu.VMEM((tm, D_MODEL), BF16),
                        pltpu.VMEM((4, tm, LANES), F32),
                        pltpu.VMEM((tm, IN_W), F32),
                        pltpu.VMEM((tm, IN_W), F32),
                        pltpu.VMEM((POOL_HALO + tm, POOL_W), F32),
                        pltpu.VMEM((len(POOL_WINDOWS), POOL_HALO + tm, HEAD_DIM), F32),
                        pltpu.VMEM((4, ATT_HEADS, tm, HEAD_DIM), F32),
                        pltpu.VMEM((3, ATT_HEADS, 4, tm // 4, HEAD_DIM), F32)],
        compiler_params=pltpu.CompilerParams(dimension_semantics=("arbitrary",),
                                             vmem_limit_bytes=VMEM_LIMIT),
        name="in_proj",
    )(x2, g_mix, w_in, qg, kg, cos, sin, pool_w, pool_scale, *later_weights)


def _dilated_attn_kernel(q_ref, k_ref, kh_ref, v_ref, vh_ref, o_ref, *, tl, rb):
    n_sub = tl // ATT_BLOCK
    qi = lax.broadcasted_iota(jnp.int32, (ATT_BLOCK, 2 * ATT_BLOCK), 0)
    kj = lax.broadcasted_iota(jnp.int32, (ATT_BLOCK, 2 * ATT_BLOCK), 1)
    delta = qi + ATT_BLOCK - kj
    in_band = jnp.logical_and(delta >= 0, delta <= ATT_BLOCK)
    bias = jnp.where(in_band, 0.0, NEG_INF).astype(F32)
    n_missing = jnp.where(pl.program_id(2) == 0, ATT_BLOCK, 0)
    bias_first = jnp.where(kj < n_missing, NEG_INF, bias)
    lane = lax.broadcasted_iota(jnp.int32, (ATT_BLOCK, LANES), 1)
    key_lane = lax.broadcasted_iota(jnp.int32, (2 * ATT_BLOCK, HEAD_DIM), 1)
    sum_blk = [jnp.where(jnp.logical_or(key_lane == ATT_HEADS + h, key_lane == 2 * ATT_HEADS + h),
                         1.0, 0.0).astype(BF16) for h in range(ATT_HEADS)]

    for r in range(rb):
        for i in range(n_sub):
            rows = slice(i * ATT_BLOCK, (i + 1) * ATT_BLOCK)
            rows2 = slice((i - 1) * ATT_BLOCK, (i + 1) * ATT_BLOCK)
            shift = jnp.zeros((ATT_BLOCK, LANES), F32)
            den = jnp.zeros((ATT_BLOCK, LANES), F32)
            for h in range(ATT_HEADS):
                c0 = r * KV_W + h * HEAD_DIM
                sl = slice(c0, c0 + HEAD_DIM)
                if i == 0:
                    k_blk = jnp.concatenate([kh_ref[0, :, sl], k_ref[0, rows, sl]], axis=0)
                    v_blk = jnp.concatenate([vh_ref[0, :, sl], v_ref[0, rows, sl]], axis=0)
                    b = bias_first
                else:
                    k_blk, v_blk, b = k_ref[0, rows2, sl], v_ref[0, rows2, sl], bias
                s = lax.dot_general(q_ref[0, rows, sl], k_blk, (((1,), (1,)), ((), ())),
                                    preferred_element_type=F32) + b
                m = jnp.max(s, axis=-1, keepdims=True).astype(BF16).astype(F32)
                p = jnp.exp2(s - m).astype(BF16)
                acc_l = jnp.dot(p, jnp.concatenate([v_blk, sum_blk[h]], axis=1),
                                preferred_element_type=F32)
                o0 = r * ATT_OUT_W + h * HEAD_DIM
                o_ref[0, rows, o0:o0 + HEAD_DIM] = acc_l[:, :HEAD_DIM].astype(BF16)
                den = den + acc_l[:, HEAD_DIM:]
                shift = jnp.where(lane == h, m, shift)
            den_hi = den.astype(BF16).astype(F32)
            st = shift + jnp.where(lane < 2 * ATT_HEADS, den_hi, den - den_hi)
            o_ref[0, rows, r * ATT_OUT_W + KV_W:(r + 1) * ATT_OUT_W] = st.astype(BF16)


def _dilated_attn(qkv, batch, seq, dilation):
    sub_len = seq // dilation
    tl = min(sub_len, ATT_STEP_QUERIES)
    rb = min(dilation, ATT_STEP_QUERIES // tl)
    n_sub = tl // ATT_BLOCK
    n_col = dilation // rb
    qkv3 = qkv.reshape(batch, sub_len, qkv.shape[1])
    cur = lambda part: pl.BlockSpec((1, tl, rb * KV_W), lambda b, r, i: (b, i, part * n_col + r))
    if sub_len > tl:
        halo_idx = lambda part: lambda b, r, i: (b, jnp.maximum(i * n_sub - 1, 0), part * n_col + r)
    else:
        halo_idx = lambda part: lambda b, r, i: (0, 0, part * n_col)
    halo = lambda part: pl.BlockSpec((1, ATT_BLOCK, rb * KV_W), halo_idx(part))
    out = pl.pallas_call(
        functools.partial(_dilated_attn_kernel, tl=tl, rb=rb),
        grid=(batch, n_col, sub_len // tl),
        in_specs=[cur(0), cur(1), halo(1), cur(2), halo(2)],
        out_specs=pl.BlockSpec((1, tl, rb * ATT_OUT_W), lambda b, r, i: (b, i, r)),
        out_shape=jax.ShapeDtypeStruct((batch, sub_len, dilation * ATT_OUT_W), BF16),
        compiler_params=pltpu.CompilerParams(
            dimension_semantics=("parallel", "parallel", "parallel"), vmem_limit_bytes=VMEM_LIMIT),
        name=f"dilated_attn_d{dilation}",
    )(qkv3, qkv3, qkv3, qkv3, qkv3)
    return out.reshape(batch * sub_len, dilation * ATT_OUT_W)


def _mem_kv_kernel(mem_ref, g_ref, w_ref, kg_ref, k_ref, v_ref):
    mn = _rms(mem_ref[0], g_ref[...]).astype(BF16)
    kv = jnp.dot(mn, w_ref[...], preferred_element_type=F32)
    for h in range(X_HEADS):
        sl = slice(h * HEAD_DIM, (h + 1) * HEAD_DIM)
        k_ref[0, :, sl] = _rms(kv[:, sl], kg_ref[...]).astype(BF16)
    v_ref[0] = kv[:, X_W:].astype(BF16)


def _mem_kv(mem, g_mem, w_ckv, ckg):
    batch, n_mem, _ = mem.shape
    out = jax.ShapeDtypeStruct((batch, n_mem, X_W), BF16)
    blk = pl.BlockSpec((1, n_mem, X_W), lambda b: (b, 0, 0))
    return pl.pallas_call(
        _mem_kv_kernel,
        grid=(batch,),
        in_specs=[pl.BlockSpec((1, n_mem, D_MODEL), lambda b: (b, 0, 0)), _resident((1, D_MODEL)),
                  _resident((D_MODEL, 2 * X_W)), _resident((1, HEAD_DIM))],
        out_specs=[blk, blk],
        out_shape=[out, out],
        compiler_params=pltpu.CompilerParams(dimension_semantics=("parallel",),
                                             vmem_limit_bytes=VMEM_LIMIT),
        name="mem_kv",
    )(mem, g_mem, w_ckv, ckg)


def _mix_cross_kernel(x_ref, a1_ref, a4_ref, a16_ref, y_ref, wo_ref,
                      gc_ref, wq_ref, qg_ref, km_ref, vm_ref, wc_ref, wgu_ref, wd_ref,
                      h_ref, wgu_o, wd_o, acc_tok, st_tok, cat_a, cat_b, o_ref, *, tm):
    step = pl.program_id(0)
    wgu_o[...] = wgu_ref[...].astype(BF16)
    wd_o[...] = wd_ref[...].astype(BF16)

    def merge(cat_ref):
        for gi, (a_ref, d) in enumerate(((a4_ref, 4), (a16_ref, 16))):
            n = tm // d
            for r in range(d):
                dst = pl.ds(r, n, stride=d)
                st_tok[gi, dst, :] = a_ref[:, r * ATT_OUT_W + KV_W:(r + 1) * ATT_OUT_W].astype(F32)
                for h in range(ATT_HEADS):
                    c0 = r * ATT_OUT_W + h * HEAD_DIM
                    acc_tok[gi, h, dst, :] = a_ref[:, c0:c0 + HEAD_DIM].astype(F32)

        for i in range(tm // MERGE_CHUNK):
            rows = slice(i * MERGE_CHUNK, (i + 1) * MERGE_CHUNK)
            stats = [a1_ref[rows, KV_W:].astype(F32), st_tok[0, rows, :], st_tok[1, rows, :]]
            m_all = jnp.maximum(jnp.maximum(stats[0], stats[1]), stats[2])
            wts = [jnp.exp2(s - m_all) for s in stats]
            dens = [pltpu.roll(s, LANES - ATT_HEADS, 1) + pltpu.roll(s, LANES - 2 * ATT_HEADS, 1)
                    for s in stats]
            total = wts[0] * dens[0] + wts[1] * dens[1] + wts[2] * dens[2]
            coef = [w / total for w in wts]
            for h in range(ATT_HEADS):
                sl = slice(h * HEAD_DIM, (h + 1) * HEAD_DIM)
                accs = (a1_ref[rows, sl].astype(F32), acc_tok[0, h, rows, :], acc_tok[1, h, rows, :])
                num = jnp.zeros((MERGE_CHUNK, HEAD_DIM), F32)
                for g in range(len(DILATED_PAIRS)):
                    num = num + coef[g][:, h:h + 1] * accs[g]
                cat_ref[rows, sl] = num.astype(BF16)
        cat_ref[:, KV_W:] = y_ref[...]

    def mix(cat_ref):
        n_mem = km_ref.shape[1]
        h1 = x_ref[...] + jnp.dot(cat_ref[...], wo_ref[...], preferred_element_type=F32)
        hn = _rms(h1, gc_ref[...]).astype(BF16)
        qc = jnp.dot(hn, wq_ref[...], preferred_element_type=F32)
        for h in range(X_HEADS):
            sl = slice(h * HEAD_DIM, (h + 1) * HEAD_DIM)
            qh = _rms(qc[:, sl], qg_ref[...] * SCORE_SCALE).astype(BF16)
            s = lax.dot_general(qh, km_ref[0, :, sl], (((1,), (1,)), ((), ())),
                                preferred_element_type=F32)
            p = jnp.exp2(s - jnp.max(s, axis=-1, keepdims=True)).astype(BF16)
            v_ones = jnp.concatenate([vm_ref[0, :, sl], jnp.ones((n_mem, HEAD_DIM), BF16)], axis=1)
            o_l = jnp.dot(p, v_ones, preferred_element_type=F32)
            o_ref[:, sl] = (o_l[:, :HEAD_DIM] / o_l[:, HEAD_DIM:]).astype(BF16)
        h_ref[...] = h1 + jnp.dot(o_ref[...], wc_ref[...], preferred_element_type=F32)

    last = pl.num_programs(0) - 1

    @pl.when(step == 0)
    def _():
        merge(cat_a)

    @pl.when(jnp.logical_and(step % 2 == 0, jnp.logical_and(step > 0, step < last)))
    def _():
        merge(cat_a)
        mix(cat_b)

    @pl.when(step % 2 == 1)
    def _():
        merge(cat_b)
        mix(cat_a)

    @pl.when(step == last)
    def _():
        mix(cat_b)


def _mix_cross(x2, branches, y, w_out, g_cross, w_cq, cqg, k_mem, v_mem, w_co, w_gate_up, w_down, seq):
    n_tok = x2.shape[0]
    tm = TOKEN_TILE
    n_tiles = n_tok // tm
    assert n_tiles % 2 == 0
    tiles_per_seq = seq // tm
    n_mem = k_mem.shape[1]
    ahead = lambda i: (jnp.minimum(i, n_tiles - 1), 0)
    behind = lambda i: (jnp.maximum(i - 1, 0), 0)
    view = lambda d, w: pl.BlockSpec((tm // d, d * w), ahead)
    mem_blk = pl.BlockSpec((1, n_mem, X_W), lambda i: (jnp.maximum(i - 1, 0) // tiles_per_seq, 0, 0))

    def cast_blk(w, rows):
        n_blk = w.shape[0] // rows
        assert n_blk * rows == w.shape[0] and n_blk <= n_tiles + 1
        return pl.BlockSpec((rows, w.shape[1]), lambda i: (jnp.minimum(i, n_blk - 1), 0))

    cast_specs = [cast_blk(w_gate_up, 32), cast_blk(w_down, 128)]
    return pl.pallas_call(
        functools.partial(_mix_cross_kernel, tm=tm),
        grid=(n_tiles + 1,),
        in_specs=[pl.BlockSpec((tm, D_MODEL), behind)] + [view(d, ATT_OUT_W) for d in DILATIONS]
                 + [view(1, POOL_W), _resident((KV_W + POOL_W, D_MODEL)), _resident((1, D_MODEL)),
                    _resident((D_MODEL, X_W)), _resident((1, HEAD_DIM)), mem_blk, mem_blk,
                    _resident((X_W, D_MODEL))] + cast_specs,
        out_specs=[pl.BlockSpec((tm, D_MODEL), behind)] + cast_specs,
        out_shape=[jax.ShapeDtypeStruct((n_tok, D_MODEL), F32),
                   jax.ShapeDtypeStruct(w_gate_up.shape, BF16),
                   jax.ShapeDtypeStruct(w_down.shape, BF16)],
        scratch_shapes=[pltpu.VMEM((2, ATT_HEADS, tm, HEAD_DIM), F32),
                        pltpu.VMEM((2, tm, LANES), F32),
                        pltpu.VMEM((tm, KV_W + POOL_W), BF16),
                        pltpu.VMEM((tm, KV_W + POOL_W), BF16),
                        pltpu.VMEM((tm, X_W), BF16)],
        compiler_params=pltpu.CompilerParams(dimension_semantics=("arbitrary",),
                                             vmem_limit_bytes=VMEM_LIMIT),
        name="mix_cross",
    )(x2, *branches, y, w_out, g_cross, w_cq, cqg, k_mem, v_mem, w_co, w_gate_up, w_down)


def _ffn_kernel(h_ref, g_ref, wgu_ref, wd_ref, o_ref):
    h = h_ref[...]
    hn = _rms(h, g_ref[...]).astype(BF16)
    gate = jnp.dot(hn, wgu_ref[:, :D_FF], preferred_element_type=F32)
    up = jnp.dot(hn, wgu_ref[:, D_FF:], preferred_element_type=F32)
    act = (gate * jax.nn.sigmoid(gate) * up).astype(BF16)
    o_ref[...] = h + jnp.dot(act, wd_ref[...], preferred_element_type=F32)


def _ffn(h2, g_ffn, w_gate_up, w_down):
    n_tok = h2.shape[0]
    tm = TOKEN_TILE
    tok = pl.BlockSpec((tm, D_MODEL), lambda i: (i, 0))
    return pl.pallas_call(
        _ffn_kernel,
        grid=(n_tok // tm,),
        in_specs=[tok, _resident((1, D_MODEL)), _resident((D_MODEL, 2 * D_FF)),
                  _resident((D_FF, D_MODEL))],
        out_specs=tok,
        out_shape=jax.ShapeDtypeStruct((n_tok, D_MODEL), F32),
        compiler_params=pltpu.CompilerParams(dimension_semantics=("parallel",),
                                             vmem_limit_bytes=VMEM_LIMIT),
        name="ffn",
    )(h2, g_ffn, w_gate_up, w_down)


def kernel(x, mem, positions, mix_norm_g, w_in, q_norm_g, k_norm_g, pool_w, pool_scale, w_out,
           cross_norm_g, mem_norm_g, w_cq, w_ckv, cq_norm_g, ck_norm_g, w_co,
           ffn_norm_g, w_gate_up, w_down):
    batch, seq, _ = x.shape
    depth = w_in.shape[0]
    row = lambda a: a.reshape(1, -1)
    h = x.reshape(batch * seq, D_MODEL)
    for layer in range(depth):
        cos, sin, w_in_l = _prep(positions, w_in[layer])
        *qkv_views, y, w_out_l, w_cq_l, w_ckv_l, w_co_l = _in_proj(
            h, row(mix_norm_g[layer]), w_in_l, row(q_norm_g[layer]), row(k_norm_g[layer]), cos, sin,
            pool_w[layer], row(pool_scale[layer]),
            (w_out[layer], w_cq[layer], w_ckv[layer], w_co[layer]), seq)
        branches = [_dilated_attn(qkv, batch, seq, d) for qkv, d in zip(qkv_views, DILATIONS)]
        k_mem, v_mem = _mem_kv(mem, row(mem_norm_g[layer]), w_ckv_l, row(ck_norm_g[layer]))
        h, w_gu_l, w_down_l = _mix_cross(
            h, branches, y, w_out_l, row(cross_norm_g[layer]), w_cq_l,
            row(cq_norm_g[layer]), k_mem, v_mem, w_co_l, w_gate_up[layer], w_down[layer], seq)
        h = _ffn(h, row(ffn_norm_g[layer]), w_gu_l, w_down_l)
    return h.reshape(batch, seq, D_MODEL)
```

```python
import functools
import math

import jax
import jax.numpy as jnp
from jax import lax
from jax.experimental import pallas as pl
from jax.experimental.pallas import tpu as pltpu

D_MODEL = 1024
HEAD_DIM = 128
ATT_HEADS = 4
DILATED_PAIRS = ((128, 1), (512, 4), (2048, 16))
DILATIONS = tuple(d for _, d in DILATED_PAIRS)
KV_W = ATT_HEADS * HEAD_DIM
POOL_WINDOWS = (2, 4, 8, 16)
POOL_W = len(POOL_WINDOWS) * HEAD_DIM
IN_W = (len(DILATED_PAIRS) + 2) * KV_W + POOL_W
ROT_DIM = HEAD_DIM // 4
ROT_HALF = ROT_DIM // 2
ROPE_THETA = 500000.0
X_HEADS = 4
X_W = X_HEADS * HEAD_DIM
D_FF = 2816
EPS = 1e-6
NEG_INF = -1e30
SCORE_SCALE = HEAD_DIM ** -0.5 * math.log2(math.e)
ATT_BLOCK = 128
ATT_STEP_QUERIES = 2048
LANES = 128
SUBLANES = 8
BF16_TILE_ROWS = 2 * SUBLANES
ROT_PAIR_LANE = LANES // 2
TOKEN_TILE = 512
ROW_CHUNK = 256
MERGE_CHUNK = 512
POOL_HALO = 32

F32 = jnp.float32
BF16 = jnp.bfloat16

VMEM_LIMIT = 56 * 1024 * 1024


def _rms(x, g):
    ms = jnp.mean(x * x, axis=-1, keepdims=True)
    return x * lax.rsqrt(ms + EPS) * g


def _resident(shape):
    nd = len(shape)
    return pl.BlockSpec(shape, lambda *_: (0,) * nd, pipeline_mode=pl.Buffered(1))


def _to_rotary_lanes(w):
    n_mid = ROT_PAIR_LANE - ROT_HALF
    lane = lax.broadcasted_iota(jnp.int32, w.shape, w.ndim - 1)
    keep = jnp.logical_or(lane < ROT_HALF, lane >= ROT_DIM + n_mid)
    from_low = jnp.logical_and(lane >= ROT_PAIR_LANE, lane < ROT_PAIR_LANE + ROT_HALF)
    moved = jnp.where(from_low, pltpu.roll(w, n_mid, w.ndim - 1),
                      pltpu.roll(w, LANES - ROT_HALF, w.ndim - 1))
    return jnp.where(keep, w, moved)


def _prep_kernel(pos_ref, invf_ref, win_ref, ctab_ref, stab_ref, win_o, *, n_qk_heads):
    ang = pos_ref[...] * invf_ref[...]
    cos = jnp.cos(ang)
    sin = jnp.sin(ang)
    n_rows = ang.shape[0]
    per_row = LANES // ROT_HALF
    lane = lax.broadcasted_iota(jnp.int32, (n_rows, LANES), 1)
    first = lane < ROT_HALF
    second = jnp.logical_and(lane >= ROT_PAIR_LANE, lane < ROT_PAIR_LANE + ROT_HALF)
    half = per_row // 2
    for j in range(half):
        shift = (LANES - ROT_HALF * j) % LANES
        c_lo = pltpu.roll(cos, shift, 1) if shift else cos
        s_lo = pltpu.roll(sin, shift, 1) if shift else sin
        c_hi = pltpu.roll(c_lo, ROT_PAIR_LANE, 1)
        s_hi = pltpu.roll(s_lo, ROT_PAIR_LANE, 1)
        for tok, (ca, cb, sa, sb) in ((j, (c_lo, c_hi, s_lo, s_hi)), (j + half, (c_hi, c_lo, s_hi, s_lo))):
            dst = pl.ds(tok, n_rows, stride=per_row)
            ctab_ref[dst, :] = jnp.where(first, ca, jnp.where(second, cb, 1.0))
            stab_ref[dst, :] = jnp.where(first, -sa, jnp.where(second, sb, 0.0))

    for h in range(n_qk_heads):
        sl = slice(h * HEAD_DIM, (h + 1) * HEAD_DIM)
        win_o[:, sl] = _to_rotary_lanes(win_ref[:, sl]).astype(BF16)
    rest = slice(n_qk_heads * HEAD_DIM, IN_W)
    win_o[:, rest] = win_ref[:, rest].astype(BF16)


def _cast_specs(weights, n_steps):
    specs = []
    for w in weights:
        rows = -(-w.shape[0] // (n_steps * BF16_TILE_ROWS)) * BF16_TILE_ROWS
        n_blk = w.shape[0] // rows
        assert n_blk * rows == w.shape[0] and n_blk <= n_steps
        specs.append(pl.BlockSpec((rows, w.shape[1]), lambda i, n_blk=n_blk: (jnp.minimum(i, n_blk - 1), 0)))
    return specs


def _prep(positions, w_in):
    n_tok = positions.size
    per_row = LANES // ROT_HALF
    n_steps = 4
    rows = n_tok // per_row // n_steps
    pos = jnp.broadcast_to(positions.reshape(-1, per_row, 1).astype(F32),
                           (n_tok // per_row, per_row, ROT_HALF)).reshape(n_tok // per_row, LANES)
    inv_freq = ROPE_THETA ** (-jnp.arange(0, ROT_DIM, 2, dtype=F32) / ROT_DIM)
    invf = jnp.tile(inv_freq, per_row).reshape(1, LANES)
    weights = (w_in,)
    row_blk = lambda w: pl.BlockSpec((w.shape[0] // n_steps, w.shape[1]), lambda i: (i, 0))
    tab = pl.BlockSpec((rows * per_row, LANES), lambda i: (i, 0))
    return pl.pallas_call(
        functools.partial(_prep_kernel, n_qk_heads=(len(DILATED_PAIRS) + 1) * ATT_HEADS),
        grid=(n_steps,),
        in_specs=[pl.BlockSpec((rows, LANES), lambda i: (i, 0)), pl.BlockSpec((1, LANES), lambda i: (0, 0))]
                 + [row_blk(w) for w in weights],
        out_specs=[tab, tab] + [row_blk(w) for w in weights],
        out_shape=[jax.ShapeDtypeStruct((n_tok, LANES), F32)] * 2
                  + [jax.ShapeDtypeStruct(w.shape, BF16) for w in weights],
        compiler_params=pltpu.CompilerParams(dimension_semantics=("parallel",),
                                             vmem_limit_bytes=VMEM_LIMIT),
        name="prep",
    )(pos, invf, *weights)


def _in_proj_kernel(x_ref, g_ref, w_ref, qg_ref, kg_ref, cos_ref, sin_ref, pw_ref, ps_ref,
                    wout_ref, wcq_ref, wckv_ref, wco_ref,
                    tok_ref, view4_ref, view16_ref, y_ref, wout_o, wcq_o, wckv_o, wco_o,
                    xn_ref, tab_ref, pa_ref, pb_ref, hist_ref, lvl_ref, stage_ref, stage4_ref,
                    *, tm, tiles_per_seq):
    step = pl.program_id(0)
    for src, dst in ((wout_ref, wout_o), (wcq_ref, wcq_o), (wckv_ref, wckv_o), (wco_ref, wco_o)):
        dst[...] = src[...].astype(BF16)
    rc = ROW_CHUNK
    n_rc = tm // rc

    def project(p_ref):
        for i in range(n_rc):
            rows = slice(i * rc, (i + 1) * rc)
            xn_ref[rows, :] = _rms(x_ref[rows, :], g_ref[...]).astype(BF16)
        for c in range(IN_W // KV_W):
            cols = slice(c * KV_W, (c + 1) * KV_W)
            p_ref[:, cols] = jnp.dot(xn_ref[...], w_ref[:, cols], preferred_element_type=F32)

    def finish(p_ref):
        for gi, (gain_ref, scale) in enumerate(((qg_ref, SCORE_SCALE), (kg_ref, 1.0))):
            gain = _to_rotary_lanes(gain_ref[...] * scale)
            tab_ref[2 * gi] = cos_ref[...] * gain
            tab_ref[2 * gi + 1] = sin_ref[...] * pltpu.roll(gain, ROT_PAIR_LANE, 1)

        def emit(c, gi, part, to_tok, to_v4, to_v16, a):
            for h in range(ATT_HEADS):
                src = slice(c * KV_W + h * HEAD_DIM, c * KV_W + (h + 1) * HEAD_DIM)
                for i in range(n_rc):
                    rows = slice(i * rc, (i + 1) * rc)
                    t = p_ref[rows, src]
                    if gi >= 0:
                        rs = lax.rsqrt(jnp.mean(t * t, axis=-1, keepdims=True) + EPS)
                        t = (t * tab_ref[2 * gi, rows, :]
                             + pltpu.roll(t, ROT_PAIR_LANE, 1) * tab_ref[2 * gi + 1, rows, :]) * rs
                    if to_tok:
                        c0 = part * KV_W + h * HEAD_DIM
                        tok_ref[rows, c0:c0 + HEAD_DIM] = t.astype(BF16)
                    if to_v4 or to_v16:
                        stage_ref[a, h, rows, :] = t
                if not (to_v4 or to_v16):
                    continue
                n4 = tm // 4
                for b in range(4):
                    t4 = stage_ref[a, h, pl.ds(b, n4, stride=4), :]
                    if to_v4:
                        c0 = (part * 4 + b) * KV_W + h * HEAD_DIM
                        view4_ref[:, c0:c0 + HEAD_DIM] = t4.astype(BF16)
                    if to_v16:
                        stage4_ref[a, h, b] = t4
                if to_v16:
                    for b in range(4):
                        for a4 in range(4):
                            c0 = (part * 16 + 4 * a4 + b) * KV_W + h * HEAD_DIM
                            view16_ref[:, c0:c0 + HEAD_DIM] = (
                                stage4_ref[a, h, b, pl.ds(a4, n4 // 4, stride=4), :].astype(BF16))

        emit(3, 1, 1, True, True, True, 0)
        emit(4, -1, 2, True, True, True, 1)
        emit(2, 0, 0, False, False, True, 2)
        emit(1, 0, 0, False, True, False, 3)
        emit(0, 0, 0, True, False, False, 0)

        seq_tile = jnp.maximum(step - 1, 0) % tiles_per_seq
        end = POOL_HALO + tm
        hist_ref[0:POOL_HALO, :] = jnp.where(seq_tile == 0, 0.0, 1.0) * hist_ref[tm:end, :]
        u_cols = slice(IN_W - POOL_W, IN_W)
        hist_ref[POOL_HALO:end, :] = p_ref[:, u_cols]
        row16 = lax.broadcasted_iota(jnp.int32, (2 * SUBLANES, HEAD_DIM), 0)
        for g, w in enumerate(POOL_WINDOWS):
            sl = slice(g * HEAD_DIM, (g + 1) * HEAD_DIM)
            src, start, shift = hist_ref, SUBLANES, 1
            while True:
                cols = sl if src is hist_ref else slice(None)
                tot = src[start:end, cols] + src[start - shift:end - shift, cols]
                shift *= 2
                if shift == w:
                    break
                lvl_ref[g, start:end, :] = tot
                src, start = lvl_ref.at[g], start + SUBLANES
            tot = tot[POOL_HALO - start:, :]
            n_first = jnp.maximum(jnp.minimum(row16 + 1, w), jnp.where(seq_tile == 0, 0, w))
            inv_first = 1.0 / n_first.astype(F32)
            ug = hist_ref[POOL_HALO:end, sl]
            d = jnp.concatenate([tot[:2 * SUBLANES] * inv_first, tot[2 * SUBLANES:] * (1.0 / w)],
                                axis=0) - ug
            yg = jnp.dot(d.astype(BF16), pw_ref[g].astype(BF16),
                         preferred_element_type=F32) * ps_ref[:, sl]
            y_ref[:, sl] = yg.astype(BF16)

    last = pl.num_programs(0) - 1

    @pl.when(step == 0)
    def _():
        hist_ref[...] = jnp.zeros(hist_ref.shape, F32)
        project(pa_ref)

    @pl.when(jnp.logical_and(step % 2 == 0, jnp.logical_and(step > 0, step < last)))
    def _():
        project(pa_ref)
        finish(pb_ref)

    @pl.when(step % 2 == 1)
    def _():
        project(pb_ref)
        finish(pa_ref)

    @pl.when(step == last)
    def _():
        finish(pb_ref)


def _in_proj(x2, g_mix, w_in, qg, kg, cos, sin, pool_w, pool_scale, later_weights, seq):
    n_tok = x2.shape[0]
    tm = TOKEN_TILE
    n_tiles = n_tok // tm
    assert n_tiles % 2 == 0
    tiles_per_seq = seq // tm
    ahead = lambda i: (jnp.minimum(i, n_tiles - 1), 0)
    behind = lambda i: (jnp.maximum(i - 1, 0), 0)
    out_dw = [(d, 3 * KV_W) for d in DILATIONS] + [(1, POOL_W)]
    view = lambda d, w: pl.BlockSpec((tm // d, d * w), behind)
    view_sds = lambda d, w: jax.ShapeDtypeStruct((n_tok // d, d * w), BF16)
    cast_specs = _cast_specs(later_weights, n_tiles + 1)
    return pl.pallas_call(
        functools.partial(_in_proj_kernel, tm=tm, tiles_per_seq=tiles_per_seq),
        grid=(n_tiles + 1,),
        in_specs=[pl.BlockSpec((tm, D_MODEL), ahead), _resident((1, D_MODEL)),
                  _resident((D_MODEL, IN_W)), _resident((1, HEAD_DIM)), _resident((1, HEAD_DIM)),
                  pl.BlockSpec((tm, LANES), behind), pl.BlockSpec((tm, LANES), behind),
                  _resident((len(POOL_WINDOWS), HEAD_DIM, HEAD_DIM)), _resident((1, POOL_W))]
                 + cast_specs,
        out_specs=[view(d, w) for d, w in out_dw] + cast_specs,
        out_shape=[view_sds(d, w) for d, w in out_dw]
                  + [jax.ShapeDtypeStruct(w.shape, BF16) for w in later_weights],
        scratch_shapes=[pltpu.VMEM((tm, D_MODEL), BF16),
                        pltpu.VMEM((4, tm, LANES), F32),
                        pltpu.VMEM((tm, IN_W), F32),
                        pltpu.VMEM((tm, IN_W), F32),
                        pltpu.VMEM((POOL_HALO + tm, POOL_W), F32),
                        pltpu.VMEM((len(POOL_WINDOWS), POOL_HALO + tm, HEAD_DIM), F32),
                        pltpu.VMEM((4, ATT_HEADS, tm, HEAD_DIM), F32),
                        pltpu.VMEM((3, ATT_HEADS, 4, tm // 4, HEAD_DIM), F32)],
        compiler_params=pltpu.CompilerParams(dimension_semantics=("arbitrary",),
                                             vmem_limit_bytes=VMEM_LIMIT),
        name="in_proj",
    )(x2, g_mix, w_in, qg, kg, cos, sin, pool_w, pool_scale, *later_weights)


def _dilated_attn_kernel(q_ref, k_ref, kh_ref, v_ref, vh_ref, o_ref, st_ref, *, tl, rb):
    n_sub = tl // ATT_BLOCK
    qi = lax.broadcasted_iota(jnp.int32, (ATT_BLOCK, 2 * ATT_BLOCK), 0)
    kj = lax.broadcasted_iota(jnp.int32, (ATT_BLOCK, 2 * ATT_BLOCK), 1)
    delta = qi + ATT_BLOCK - kj
    in_band = jnp.logical_and(delta >= 0, delta <= ATT_BLOCK)
    bias = jnp.where(in_band, 0.0, NEG_INF).astype(F32)
    n_missing = jnp.where(pl.program_id(2) == 0, ATT_BLOCK, 0)
    bias_first = jnp.where(kj < n_missing, NEG_INF, bias)
    lane = lax.broadcasted_iota(jnp.int32, (ATT_BLOCK, LANES), 1)
    key_lane = lax.broadcasted_iota(jnp.int32, (2 * ATT_BLOCK, HEAD_DIM), 1)
    sum_blk = [jnp.where(key_lane == ATT_HEADS + h, 1.0, 0.0).astype(BF16) for h in range(ATT_HEADS)]

    for r in range(rb):
        for i in range(n_sub):
            rows = slice(i * ATT_BLOCK, (i + 1) * ATT_BLOCK)
            rows2 = slice((i - 1) * ATT_BLOCK, (i + 1) * ATT_BLOCK)
            shift = jnp.zeros((ATT_BLOCK, LANES), F32)
            den = jnp.zeros((ATT_BLOCK, LANES), F32)
            for h in range(ATT_HEADS):
                c0 = r * KV_W + h * HEAD_DIM
                sl = slice(c0, c0 + HEAD_DIM)
                if i == 0:
                    k_blk = jnp.concatenate([kh_ref[0, :, sl], k_ref[0, rows, sl]], axis=0)
                    v_blk = jnp.concatenate([vh_ref[0, :, sl], v_ref[0, rows, sl]], axis=0)
                    b = bias_first
                else:
                    k_blk, v_blk, b = k_ref[0, rows2, sl], v_ref[0, rows2, sl], bias
                s = lax.dot_general(q_ref[0, rows, sl], k_blk, (((1,), (1,)), ((), ())),
                                    preferred_element_type=F32) + b
                m = jnp.max(s, axis=-1, keepdims=True)
                p = jnp.exp2(s - m).astype(BF16)
                acc_l = jnp.dot(p, jnp.concatenate([v_blk, sum_blk[h]], axis=1),
                                preferred_element_type=F32)
                o_ref[0, rows, sl] = acc_l[:, :HEAD_DIM].astype(BF16)
                den = den + acc_l[:, HEAD_DIM:]
                shift = jnp.where(lane == h, m, shift)
            st_ref[0, rows, r * LANES:(r + 1) * LANES] = shift + den


def _dilated_attn(qkv, batch, seq, dilation):
    sub_len = seq // dilation
    tl = min(sub_len, ATT_STEP_QUERIES)
    rb = min(dilation, ATT_STEP_QUERIES // tl)
    n_sub = tl // ATT_BLOCK
    n_col = dilation // rb
    qkv3 = qkv.reshape(batch, sub_len, qkv.shape[1])
    cur = lambda part: pl.BlockSpec((1, tl, rb * KV_W), lambda b, r, i: (b, i, part * n_col + r))
    if sub_len > tl:
        halo_idx = lambda part: lambda b, r, i: (b, jnp.maximum(i * n_sub - 1, 0), part * n_col + r)
    else:
        halo_idx = lambda part: lambda b, r, i: (0, 0, part * n_col)
    halo = lambda part: pl.BlockSpec((1, ATT_BLOCK, rb * KV_W), halo_idx(part))
    out = lambda w: pl.BlockSpec((1, tl, rb * w), lambda b, r, i: (b, i, r))
    acc, st = pl.pallas_call(
        functools.partial(_dilated_attn_kernel, tl=tl, rb=rb),
        grid=(batch, n_col, sub_len // tl),
        in_specs=[cur(0), cur(1), halo(1), cur(2), halo(2)],
        out_specs=[out(KV_W), out(LANES)],
        out_shape=[jax.ShapeDtypeStruct((batch, sub_len, dilation * KV_W), BF16),
                   jax.ShapeDtypeStruct((batch, sub_len, dilation * LANES), F32)],
        compiler_params=pltpu.CompilerParams(
            dimension_semantics=("parallel", "parallel", "parallel"), vmem_limit_bytes=VMEM_LIMIT),
        name=f"dilated_attn_d{dilation}",
    )(qkv3, qkv3, qkv3, qkv3, qkv3)
    return acc.reshape(batch * sub_len, dilation * KV_W), st.reshape(batch * sub_len, dilation * LANES)


def _mem_kv_kernel(mem_ref, g_ref, w_ref, kg_ref, k_ref, v_ref):
    mn = _rms(mem_ref[0], g_ref[...]).astype(BF16)
    kv = jnp.dot(mn, w_ref[...], preferred_element_type=F32)
    for h in range(X_HEADS):
        sl = slice(h * HEAD_DIM, (h + 1) * HEAD_DIM)
        k_ref[0, :, sl] = _rms(kv[:, sl], kg_ref[...]).astype(BF16)
    v_ref[0] = kv[:, X_W:].astype(BF16)


def _mem_kv(mem, g_mem, w_ckv, ckg):
    batch, n_mem, _ = mem.shape
    out = jax.ShapeDtypeStruct((batch, n_mem, X_W), BF16)
    blk = pl.BlockSpec((1, n_mem, X_W), lambda b: (b, 0, 0))
    return pl.pallas_call(
        _mem_kv_kernel,
        grid=(batch,),
        in_specs=[pl.BlockSpec((1, n_mem, D_MODEL), lambda b: (b, 0, 0)), _resident((1, D_MODEL)),
                  _resident((D_MODEL, 2 * X_W)), _resident((1, HEAD_DIM))],
        out_specs=[blk, blk],
        out_shape=[out, out],
        compiler_params=pltpu.CompilerParams(dimension_semantics=("parallel",),
                                             vmem_limit_bytes=VMEM_LIMIT),
        name="mem_kv",
    )(mem, g_mem, w_ckv, ckg)


def _mix_cross_kernel(x_ref, a1_ref, a4_ref, a16_ref, s1_ref, s4_ref, s16_ref, y_ref, wo_ref,
                      gc_ref, wq_ref, qg_ref, km_ref, vm_ref, wc_ref, wgu_ref, wd_ref,
                      h_ref, wgu_o, wd_o, acc_tok, st_tok, cat_a, cat_b, o_ref, *, tm):
    step = pl.program_id(0)
    wgu_o[...] = wgu_ref[...].astype(BF16)
    wd_o[...] = wd_ref[...].astype(BF16)

    def merge(cat_ref):
        for gi, (a_ref, s_ref, d) in enumerate(((a4_ref, s4_ref, 4), (a16_ref, s16_ref, 16))):
            n = tm // d
            for r in range(d):
                dst = pl.ds(r, n, stride=d)
                st_tok[gi, dst, :] = s_ref[:, r * LANES:(r + 1) * LANES]
                for h in range(ATT_HEADS):
                    c0 = r * KV_W + h * HEAD_DIM
                    acc_tok[gi, h, dst, :] = a_ref[:, c0:c0 + HEAD_DIM].astype(F32)

        for i in range(tm // MERGE_CHUNK):
            rows = slice(i * MERGE_CHUNK, (i + 1) * MERGE_CHUNK)
            stats = [s1_ref[rows, :], st_tok[0, rows, :], st_tok[1, rows, :]]
            m_all = jnp.maximum(jnp.maximum(stats[0], stats[1]), stats[2])
            wts = [jnp.exp2(s - m_all) for s in stats]
            dens = [pltpu.roll(s, LANES - ATT_HEADS, 1) for s in stats]
            total = wts[0] * dens[0] + wts[1] * dens[1] + wts[2] * dens[2]
            coef = [w / total for w in wts]
            for h in range(ATT_HEADS):
                sl = slice(h * HEAD_DIM, (h + 1) * HEAD_DIM)
                accs = (a1_ref[rows, sl].astype(F32), acc_tok[0, h, rows, :], acc_tok[1, h, rows, :])
                num = jnp.zeros((MERGE_CHUNK, HEAD_DIM), F32)
                for g in range(len(DILATED_PAIRS)):
                    num = num + coef[g][:, h:h + 1] * accs[g]
                cat_ref[rows, sl] = num.astype(BF16)
        cat_ref[:, KV_W:] = y_ref[...]

    def mix(cat_ref):
        n_mem = km_ref.shape[1]
        h1 = x_ref[...] + jnp.dot(cat_ref[...], wo_ref[...], preferred_element_type=F32)
        hn = _rms(h1, gc_ref[...]).astype(BF16)
        qc = jnp.dot(hn, wq_ref[...], preferred_element_type=F32)
        for h in range(X_HEADS):
            sl = slice(h * HEAD_DIM, (h + 1) * HEAD_DIM)
            qh = _rms(qc[:, sl], qg_ref[...] * SCORE_SCALE).astype(BF16)
            s = lax.dot_general(qh, km_ref[0, :, sl], (((1,), (1,)), ((), ())),
                                preferred_element_type=F32)
            p = jnp.exp2(s - jnp.max(s, axis=-1, keepdims=True)).astype(BF16)
            v_ones = jnp.concatenate([vm_ref[0, :, sl], jnp.ones((n_mem, HEAD_DIM), BF16)], axis=1)
            o_l = jnp.dot(p, v_ones, preferred_element_type=F32)
            o_ref[:, sl] = (o_l[:, :HEAD_DIM] / o_l[:, HEAD_DIM:]).astype(BF16)
        h_ref[...] = h1 + jnp.dot(o_ref[...], wc_ref[...], preferred_element_type=F32)

    last = pl.num_programs(0) - 1

    @pl.when(step == 0)
    def _():
        merge(cat_a)

    @pl.when(jnp.logical_and(step % 2 == 0, jnp.logical_and(step > 0, step < last)))
    def _():
        merge(cat_a)
        mix(cat_b)

    @pl.when(step % 2 == 1)
    def _():
        merge(cat_b)
        mix(cat_a)

    @pl.when(step == last)
    def _():
        mix(cat_b)


def _mix_cross(x2, accs, stats, y, w_out, g_cross, w_cq, cqg, k_mem, v_mem, w_co, w_gate_up, w_down,
               seq):
    n_tok = x2.shape[0]
    tm = TOKEN_TILE
    n_tiles = n_tok // tm
    assert n_tiles % 2 == 0
    tiles_per_seq = seq // tm
    n_mem = k_mem.shape[1]
    ahead = lambda i: (jnp.minimum(i, n_tiles - 1), 0)
    behind = lambda i: (jnp.maximum(i - 1, 0), 0)
    view = lambda d, w: pl.BlockSpec((tm // d, d * w), ahead)
    mem_blk = pl.BlockSpec((1, n_mem, X_W), lambda i: (jnp.maximum(i - 1, 0) // tiles_per_seq, 0, 0))

    def cast_blk(w, rows):
        n_blk = w.shape[0] // rows
        assert n_blk * rows == w.shape[0] and n_blk <= n_tiles + 1
        return pl.BlockSpec((rows, w.shape[1]), lambda i: (jnp.minimum(i, n_blk - 1), 0))

    cast_specs = [cast_blk(w_gate_up, 32), cast_blk(w_down, 128)]
    return pl.pallas_call(
        functools.partial(_mix_cross_kernel, tm=tm),
        grid=(n_tiles + 1,),
        in_specs=[pl.BlockSpec((tm, D_MODEL), behind)] + [view(d, KV_W) for d in DILATIONS]
                 + [view(d, LANES) for d in DILATIONS]
                 + [view(1, POOL_W), _resident((KV_W + POOL_W, D_MODEL)), _resident((1, D_MODEL)),
                    _resident((D_MODEL, X_W)), _resident((1, HEAD_DIM)), mem_blk, mem_blk,
                    _resident((X_W, D_MODEL))] + cast_specs,
        out_specs=[pl.BlockSpec((tm, D_MODEL), behind)] + cast_specs,
        out_shape=[jax.ShapeDtypeStruct((n_tok, D_MODEL), F32),
                   jax.ShapeDtypeStruct(w_gate_up.shape, BF16),
                   jax.ShapeDtypeStruct(w_down.shape, BF16)],
        scratch_shapes=[pltpu.VMEM((2, ATT_HEADS, tm, HEAD_DIM), F32),
                        pltpu.VMEM((2, tm, LANES), F32),
                        pltpu.VMEM((tm, KV_W + POOL_W), BF16),
                        pltpu.VMEM((tm, KV_W + POOL_W), BF16),
                        pltpu.VMEM((tm, X_W), BF16)],
        compiler_params=pltpu.CompilerParams(dimension_semantics=("arbitrary",),
                                             vmem_limit_bytes=VMEM_LIMIT),
        name="mix_cross",
    )(x2, *accs, *stats, y, w_out, g_cross, w_cq, cqg, k_mem, v_mem, w_co, w_gate_up, w_down)


def _ffn_kernel(h_ref, g_ref, wgu_ref, wd_ref, o_ref):
    h = h_ref[...]
    hn = _rms(h, g_ref[...]).astype(BF16)
    gate = jnp.dot(hn, wgu_ref[:, :D_FF], preferred_element_type=F32)
    up = jnp.dot(hn, wgu_ref[:, D_FF:], preferred_element_type=F32)
    act = (gate * jax.nn.sigmoid(gate) * up).astype(BF16)
    o_ref[...] = h + jnp.dot(act, wd_ref[...], preferred_element_type=F32)


def _ffn(h2, g_ffn, w_gate_up, w_down):
    n_tok = h2.shape[0]
    tm = TOKEN_TILE
    tok = pl.BlockSpec((tm, D_MODEL), lambda i: (i, 0))
    return pl.pallas_call(
        _ffn_kernel,
        grid=(n_tok // tm,),
        in_specs=[tok, _resident((1, D_MODEL)), _resident((D_MODEL, 2 * D_FF)),
                  _resident((D_FF, D_MODEL))],
        out_specs=tok,
        out_shape=jax.ShapeDtypeStruct((n_tok, D_MODEL), F32),
        compiler_params=pltpu.CompilerParams(dimension_semantics=("parallel",),
                                             vmem_limit_bytes=VMEM_LIMIT),
        name="ffn",
    )(h2, g_ffn, w_gate_up, w_down)


def kernel(x, mem, positions, mix_norm_g, w_in, q_norm_g, k_norm_g, pool_w, pool_scale, w_out,
           cross_norm_g, mem_norm_g, w_cq, w_ckv, cq_norm_g, ck_norm_g, w_co,
           ffn_norm_g, w_gate_up, w_down):
    batch, seq, _ = x.shape
    depth = w_in.shape[0]
    row = lambda a: a.reshape(1, -1)
    h = x.reshape(batch * seq, D_MODEL)
    for layer in range(depth):
        cos, sin, w_in_l = _prep(positions, w_in[layer])
        *qkv_views, y, w_out_l, w_cq_l, w_ckv_l, w_co_l = _in_proj(
            h, row(mix_norm_g[layer]), w_in_l, row(q_norm_g[layer]), row(k_norm_g[layer]), cos, sin,
            pool_w[layer], row(pool_scale[layer]),
            (w_out[layer], w_cq[layer], w_ckv[layer], w_co[layer]), seq)
        accs, stats = zip(*[_dilated_attn(qkv, batch, seq, d) for qkv, d in zip(qkv_views, DILATIONS)])
        k_mem, v_mem = _mem_kv(mem, row(mem_norm_g[layer]), w_ckv_l, row(ck_norm_g[layer]))
        h, w_gu_l, w_down_l = _mix_cross(
            h, accs, stats, y, w_out_l, row(cross_norm_g[layer]), w_cq_l,
            row(cq_norm_g[layer]), k_mem, v_mem, w_co_l, w_gate_up[layer], w_down[layer], seq)
        h = _ffn(h, row(ffn_norm_g[layer]), w_gu_l, w_down_l)
    return h.reshape(batch, seq, D_MODEL)
```

```python
import functools
import math

import jax
import jax.numpy as jnp
from jax import lax
from jax.experimental import pallas as pl
from jax.experimental.pallas import tpu as pltpu

D_MODEL = 1024
HEAD_DIM = 128
ATT_HEADS = 4
DILATED_PAIRS = ((128, 1), (512, 4), (2048, 16))
DILATIONS = tuple(d for _, d in DILATED_PAIRS)
KV_W = ATT_HEADS * HEAD_DIM
POOL_WINDOWS = (2, 4, 8, 16)
POOL_W = len(POOL_WINDOWS) * HEAD_DIM
IN_W = (len(DILATED_PAIRS) + 2) * KV_W + POOL_W
ROT_DIM = HEAD_DIM // 4
ROT_HALF = ROT_DIM // 2
ROPE_THETA = 500000.0
X_HEADS = 4
X_W = X_HEADS * HEAD_DIM
D_FF = 2816
EPS = 1e-6
NEG_INF = -1e30
SCORE_SCALE = HEAD_DIM ** -0.5 * math.log2(math.e)
ATT_BLOCK = 128
ATT_STEP_QUERIES = 2048
LANES = 128
ATT_OUT_W = KV_W + LANES
SUBLANES = 8
BF16_TILE_ROWS = 2 * SUBLANES
ROT_PAIR_LANE = LANES // 2
TOKEN_TILE = 512
ROW_CHUNK = 256
MERGE_CHUNK = 512
POOL_HALO = 32

F32 = jnp.float32
BF16 = jnp.bfloat16

VMEM_LIMIT = 56 * 1024 * 1024


def _rms(x, g):
    ms = jnp.mean(x * x, axis=-1, keepdims=True)
    return x * lax.rsqrt(ms + EPS) * g


def _resident(shape):
    nd = len(shape)
    return pl.BlockSpec(shape, lambda *_: (0,) * nd, pipeline_mode=pl.Buffered(1))


def _to_rotary_lanes(w):
    n_mid = ROT_PAIR_LANE - ROT_HALF
    lane = lax.broadcasted_iota(jnp.int32, w.shape, w.ndim - 1)
    keep = jnp.logical_or(lane < ROT_HALF, lane >= ROT_DIM + n_mid)
    from_low = jnp.logical_and(lane >= ROT_PAIR_LANE, lane < ROT_PAIR_LANE + ROT_HALF)
    moved = jnp.where(from_low, pltpu.roll(w, n_mid, w.ndim - 1),
                      pltpu.roll(w, LANES - ROT_HALF, w.ndim - 1))
    return jnp.where(keep, w, moved)


def _prep_kernel(pos_ref, invf_ref, win_ref, ctab_ref, stab_ref, win_o, *, n_qk_heads):
    ang = pos_ref[...] * invf_ref[...]
    cos = jnp.cos(ang)
    sin = jnp.sin(ang)
    n_rows = ang.shape[0]
    per_row = LANES // ROT_HALF
    lane = lax.broadcasted_iota(jnp.int32, (n_rows, LANES), 1)
    first = lane < ROT_HALF
    second = jnp.logical_and(lane >= ROT_PAIR_LANE, lane < ROT_PAIR_LANE + ROT_HALF)
    half = per_row // 2
    for j in range(half):
        shift = (LANES - ROT_HALF * j) % LANES
        c_lo = pltpu.roll(cos, shift, 1) if shift else cos
        s_lo = pltpu.roll(sin, shift, 1) if shift else sin
        c_hi = pltpu.roll(c_lo, ROT_PAIR_LANE, 1)
        s_hi = pltpu.roll(s_lo, ROT_PAIR_LANE, 1)
        for tok, (ca, cb, sa, sb) in ((j, (c_lo, c_hi, s_lo, s_hi)), (j + half, (c_hi, c_lo, s_hi, s_lo))):
            dst = pl.ds(tok, n_rows, stride=per_row)
            ctab_ref[dst, :] = jnp.where(first, ca, jnp.where(second, cb, 1.0))
            stab_ref[dst, :] = jnp.where(first, -sa, jnp.where(second, sb, 0.0))

    for h in range(n_qk_heads):
        sl = slice(h * HEAD_DIM, (h + 1) * HEAD_DIM)
        win_o[:, sl] = _to_rotary_lanes(win_ref[:, sl]).astype(BF16)
    rest = slice(n_qk_heads * HEAD_DIM, IN_W)
    win_o[:, rest] = win_ref[:, rest].astype(BF16)


def _cast_specs(weights, n_steps):
    specs = []
    for w in weights:
        rows = -(-w.shape[0] // (n_steps * BF16_TILE_ROWS)) * BF16_TILE_ROWS
        n_blk = w.shape[0] // rows
        assert n_blk * rows == w.shape[0] and n_blk <= n_steps
        specs.append(pl.BlockSpec((rows, w.shape[1]), lambda i, n_blk=n_blk: (jnp.minimum(i, n_blk - 1), 0)))
    return specs


def _prep(positions, w_in):
    n_tok = positions.size
    per_row = LANES // ROT_HALF
    n_steps = 4
    rows = n_tok // per_row // n_steps
    pos = jnp.broadcast_to(positions.reshape(-1, per_row, 1).astype(F32),
                           (n_tok // per_row, per_row, ROT_HALF)).reshape(n_tok // per_row, LANES)
    inv_freq = ROPE_THETA ** (-jnp.arange(0, ROT_DIM, 2, dtype=F32) / ROT_DIM)
    invf = jnp.tile(inv_freq, per_row).reshape(1, LANES)
    weights = (w_in,)
    row_blk = lambda w: pl.BlockSpec((w.shape[0] // n_steps, w.shape[1]), lambda i: (i, 0))
    tab = pl.BlockSpec((rows * per_row, LANES), lambda i: (i, 0))
    return pl.pallas_call(
        functools.partial(_prep_kernel, n_qk_heads=(len(DILATED_PAIRS) + 1) * ATT_HEADS),
        grid=(n_steps,),
        in_specs=[pl.BlockSpec((rows, LANES), lambda i: (i, 0)), pl.BlockSpec((1, LANES), lambda i: (0, 0))]
                 + [row_blk(w) for w in weights],
        out_specs=[tab, tab] + [row_blk(w) for w in weights],
        out_shape=[jax.ShapeDtypeStruct((n_tok, LANES), F32)] * 2
                  + [jax.ShapeDtypeStruct(w.shape, BF16) for w in weights],
        compiler_params=pltpu.CompilerParams(dimension_semantics=("parallel",),
                                             vmem_limit_bytes=VMEM_LIMIT),
        name="prep",
    )(pos, invf, *weights)


def _in_proj_kernel(x_ref, g_ref, w_ref, qg_ref, kg_ref, cos_ref, sin_ref, pw_ref, ps_ref,
                    wout_ref, wcq_ref, wckv_ref, wco_ref,
                    tok_ref, view4_ref, view16_ref, y_ref, wout_o, wcq_o, wckv_o, wco_o,
                    xn_ref, tab_ref, pa_ref, pb_ref, hist_ref, lvl_ref, stage_ref, stage4_ref,
                    *, tm, tiles_per_seq):
    step = pl.program_id(0)
    for src, dst in ((wout_ref, wout_o), (wcq_ref, wcq_o), (wckv_ref, wckv_o), (wco_ref, wco_o)):
        dst[...] = src[...].astype(BF16)
    rc = ROW_CHUNK
    n_rc = tm // rc

    def project(p_ref):
        for i in range(n_rc):
            rows = slice(i * rc, (i + 1) * rc)
            xn_ref[rows, :] = _rms(x_ref[rows, :], g_ref[...]).astype(BF16)
        for c in range(IN_W // KV_W):
            cols = slice(c * KV_W, (c + 1) * KV_W)
            p_ref[:, cols] = jnp.dot(xn_ref[...], w_ref[:, cols], preferred_element_type=F32)

    def finish(p_ref):
        for gi, (gain_ref, scale) in enumerate(((qg_ref, SCORE_SCALE), (kg_ref, 1.0))):
            gain = _to_rotary_lanes(gain_ref[...] * scale)
            tab_ref[2 * gi] = cos_ref[...] * gain
            tab_ref[2 * gi + 1] = sin_ref[...] * pltpu.roll(gain, ROT_PAIR_LANE, 1)

        def emit(c, gi, part, to_tok, to_v4, to_v16, a):
            for h in range(ATT_HEADS):
                src = slice(c * KV_W + h * HEAD_DIM, c * KV_W + (h + 1) * HEAD_DIM)
                for i in range(n_rc):
                    rows = slice(i * rc, (i + 1) * rc)
                    t = p_ref[rows, src]
                    if gi >= 0:
                        rs = lax.rsqrt(jnp.mean(t * t, axis=-1, keepdims=True) + EPS)
                        t = (t * tab_ref[2 * gi, rows, :]
                             + pltpu.roll(t, ROT_PAIR_LANE, 1) * tab_ref[2 * gi + 1, rows, :]) * rs
                    if to_tok:
                        c0 = part * KV_W + h * HEAD_DIM
                        tok_ref[rows, c0:c0 + HEAD_DIM] = t.astype(BF16)
                    if to_v4 or to_v16:
                        stage_ref[a, h, rows, :] = t
                if not (to_v4 or to_v16):
                    continue
                n4 = tm // 4
                for b in range(4):
                    t4 = stage_ref[a, h, pl.ds(b, n4, stride=4), :]
                    if to_v4:
                        c0 = (part * 4 + b) * KV_W + h * HEAD_DIM
                        view4_ref[:, c0:c0 + HEAD_DIM] = t4.astype(BF16)
                    if to_v16:
                        stage4_ref[a, h, b] = t4
                if to_v16:
                    for b in range(4):
                        for a4 in range(4):
                            c0 = (part * 16 + 4 * a4 + b) * KV_W + h * HEAD_DIM
                            view16_ref[:, c0:c0 + HEAD_DIM] = (
                                stage4_ref[a, h, b, pl.ds(a4, n4 // 4, stride=4), :].astype(BF16))

        emit(3, 1, 1, True, True, True, 0)
        emit(4, -1, 2, True, True, True, 1)
        emit(2, 0, 0, False, False, True, 2)
        emit(1, 0, 0, False, True, False, 3)
        emit(0, 0, 0, True, False, False, 0)

        seq_tile = jnp.maximum(step - 1, 0) % tiles_per_seq
        end = POOL_HALO + tm
        hist_ref[0:POOL_HALO, :] = jnp.where(seq_tile == 0, 0.0, 1.0) * hist_ref[tm:end, :]
        u_cols = slice(IN_W - POOL_W, IN_W)
        hist_ref[POOL_HALO:end, :] = p_ref[:, u_cols]
        row16 = lax.broadcasted_iota(jnp.int32, (2 * SUBLANES, HEAD_DIM), 0)
        for g, w in enumerate(POOL_WINDOWS):
            sl = slice(g * HEAD_DIM, (g + 1) * HEAD_DIM)
            src, start, shift = hist_ref, SUBLANES, 1
            while True:
                cols = sl if src is hist_ref else slice(None)
                tot = src[start:end, cols] + src[start - shift:end - shift, cols]
                shift *= 2
                if shift == w:
                    break
                lvl_ref[g, start:end, :] = tot
                src, start = lvl_ref.at[g], start + SUBLANES
            tot = tot[POOL_HALO - start:, :]
            n_first = jnp.maximum(jnp.minimum(row16 + 1, w), jnp.where(seq_tile == 0, 0, w))
            inv_first = 1.0 / n_first.astype(F32)
            ug = hist_ref[POOL_HALO:end, sl]
            d = jnp.concatenate([tot[:2 * SUBLANES] * inv_first, tot[2 * SUBLANES:] * (1.0 / w)],
                                axis=0) - ug
            yg = jnp.dot(d.astype(BF16), pw_ref[g].astype(BF16),
                         preferred_element_type=F32) * ps_ref[:, sl]
            y_ref[:, sl] = yg.astype(BF16)

    last = pl.num_programs(0) - 1

    @pl.when(step == 0)
    def _():
        hist_ref[...] = jnp.zeros(hist_ref.shape, F32)
        project(pa_ref)

    @pl.when(jnp.logical_and(step % 2 == 0, jnp.logical_and(step > 0, step < last)))
    def _():
        project(pa_ref)
        finish(pb_ref)

    @pl.when(step % 2 == 1)
    def _():
        project(pb_ref)
        finish(pa_ref)

    @pl.when(step == last)
    def _():
        finish(pb_ref)


def _in_proj(x2, g_mix, w_in, qg, kg, cos, sin, pool_w, pool_scale, later_weights, seq):
    n_tok = x2.shape[0]
    tm = TOKEN_TILE
    n_tiles = n_tok // tm
    assert n_tiles % 2 == 0
    tiles_per_seq = seq // tm
    ahead = lambda i: (jnp.minimum(i, n_tiles - 1), 0)
    behind = lambda i: (jnp.maximum(i - 1, 0), 0)
    out_dw = [(d, 3 * KV_W) for d in DILATIONS] + [(1, POOL_W)]
    view = lambda d, w: pl.BlockSpec((tm // d, d * w), behind)
    view_sds = lambda d, w: jax.ShapeDtypeStruct((n_tok // d, d * w), BF16)
    cast_specs = _cast_specs(later_weights, n_tiles + 1)
    return pl.pallas_call(
        functools.partial(_in_proj_kernel, tm=tm, tiles_per_seq=tiles_per_seq),
        grid=(n_tiles + 1,),
        in_specs=[pl.BlockSpec((tm, D_MODEL), ahead), _resident((1, D_MODEL)),
                  _resident((D_MODEL, IN_W)), _resident((1, HEAD_DIM)), _resident((1, HEAD_DIM)),
                  pl.BlockSpec((tm, LANES), behind), pl.BlockSpec((tm, LANES), behind),
                  _resident((len(POOL_WINDOWS), HEAD_DIM, HEAD_DIM)), _resident((1, POOL_W))]
                 + cast_specs,
        out_specs=[view(d, w) for d, w in out_dw] + cast_specs,
        out_shape=[view_sds(d, w) for d, w in out_dw]
                  + [jax.ShapeDtypeStruct(w.shape, BF16) for w in later_weights],
        scratch_shapes=[pltpu.VMEM((tm, D_MODEL), BF16),
                        pltpu.VMEM((4, tm, LANES), F32),
                        pltpu.VMEM((tm, IN_W), F32),
                        pltpu.VMEM((tm, IN_W), F32),
                        pltpu.VMEM((POOL_HALO + tm, POOL_W), F32),
                        pltpu.VMEM((len(POOL_WINDOWS), POOL_HALO + tm, HEAD_DIM), F32),
                        pltpu.VMEM((4, ATT_HEADS, tm, HEAD_DIM), F32),
                        pltpu.VMEM((3, ATT_HEADS, 4, tm // 4, HEAD_DIM), F32)],
        compiler_params=pltpu.CompilerParams(dimension_semantics=("arbitrary",),
                                             vmem_limit_bytes=VMEM_LIMIT),
        name="in_proj",
    )(x2, g_mix, w_in, qg, kg, cos, sin, pool_w, pool_scale, *later_weights)


def _dilated_attn_kernel(q_ref, k_ref, kh_ref, v_ref, vh_ref, o_ref, *, tl, rb):
    n_sub = tl // ATT_BLOCK
    qi = lax.broadcasted_iota(jnp.int32, (ATT_BLOCK, 2 * ATT_BLOCK), 0)
    kj = lax.broadcasted_iota(jnp.int32, (ATT_BLOCK, 2 * ATT_BLOCK), 1)
    delta = qi + ATT_BLOCK - kj
    in_band = jnp.logical_and(delta >= 0, delta <= ATT_BLOCK)
    bias = jnp.where(in_band, 0.0, NEG_INF).astype(F32)
    n_missing = jnp.where(pl.program_id(2) == 0, ATT_BLOCK, 0)
    bias_first = jnp.where(kj < n_missing, NEG_INF, bias)
    lane = lax.broadcasted_iota(jnp.int32, (ATT_BLOCK, LANES), 1)
    key_lane = lax.broadcasted_iota(jnp.int32, (2 * ATT_BLOCK, HEAD_DIM), 1)
    sum_blk = [jnp.where(jnp.logical_or(key_lane == 3 * ATT_HEADS + h, key_lane == 4 * ATT_HEADS + h),
                         1.0, 0.0).astype(BF16) for h in range(ATT_HEADS)]
    shift_head = jnp.where(lane < 3 * ATT_HEADS, jnp.bitwise_and(lane, ATT_HEADS - 1), -1)
    keeps_hi = jnp.logical_or(lane < ATT_HEADS,
                              jnp.logical_and(lane >= 3 * ATT_HEADS, lane < 4 * ATT_HEADS))
    keeps_lo = jnp.logical_and(lane >= 2 * ATT_HEADS, lane < 3 * ATT_HEADS)

    for r in range(rb):
        for i in range(n_sub):
            rows = slice(i * ATT_BLOCK, (i + 1) * ATT_BLOCK)
            rows2 = slice((i - 1) * ATT_BLOCK, (i + 1) * ATT_BLOCK)
            shift = jnp.zeros((ATT_BLOCK, LANES), F32)
            den = jnp.zeros((ATT_BLOCK, LANES), F32)
            for h in range(ATT_HEADS):
                c0 = r * KV_W + h * HEAD_DIM
                sl = slice(c0, c0 + HEAD_DIM)
                if i == 0:
                    k_blk = jnp.concatenate([kh_ref[0, :, sl], k_ref[0, rows, sl]], axis=0)
                    v_blk = jnp.concatenate([vh_ref[0, :, sl], v_ref[0, rows, sl]], axis=0)
                    b = bias_first
                else:
                    k_blk, v_blk, b = k_ref[0, rows2, sl], v_ref[0, rows2, sl], bias
                s = lax.dot_general(q_ref[0, rows, sl], k_blk, (((1,), (1,)), ((), ())),
                                    preferred_element_type=F32) + b
                m = jnp.max(s, axis=-1, keepdims=True)
                p = jnp.exp2(s - m).astype(BF16)
                acc_l = jnp.dot(p, jnp.concatenate([v_blk, sum_blk[h]], axis=1),
                                preferred_element_type=F32)
                o0 = r * ATT_OUT_W + h * HEAD_DIM
                o_ref[0, rows, o0:o0 + HEAD_DIM] = acc_l[:, :HEAD_DIM].astype(BF16)
                den = den + acc_l[:, HEAD_DIM:]
                shift = jnp.where(shift_head == h, m, shift)
            full = shift + den
            hi = full.astype(BF16).astype(F32)
            rest = full - hi
            lo = rest - rest.astype(BF16).astype(F32)
            st = jnp.where(keeps_hi, hi, jnp.where(keeps_lo, lo, rest))
            o_ref[0, rows, r * ATT_OUT_W + KV_W:(r + 1) * ATT_OUT_W] = st.astype(BF16)


def _dilated_attn(qkv, batch, seq, dilation):
    sub_len = seq // dilation
    tl = min(sub_len, ATT_STEP_QUERIES)
    rb = min(dilation, ATT_STEP_QUERIES // tl)
    n_sub = tl // ATT_BLOCK
    n_col = dilation // rb
    qkv3 = qkv.reshape(batch, sub_len, qkv.shape[1])
    cur = lambda part: pl.BlockSpec((1, tl, rb * KV_W), lambda b, r, i: (b, i, part * n_col + r))
    if sub_len > tl:
        halo_idx = lambda part: lambda b, r, i: (b, jnp.maximum(i * n_sub - 1, 0), part * n_col + r)
    else:
        halo_idx = lambda part: lambda b, r, i: (0, 0, part * n_col)
    halo = lambda part: pl.BlockSpec((1, ATT_BLOCK, rb * KV_W), halo_idx(part))
    out = pl.pallas_call(
        functools.partial(_dilated_attn_kernel, tl=tl, rb=rb),
        grid=(batch, n_col, sub_len // tl),
        in_specs=[cur(0), cur(1), halo(1), cur(2), halo(2)],
        out_specs=pl.BlockSpec((1, tl, rb * ATT_OUT_W), lambda b, r, i: (b, i, r)),
        out_shape=jax.ShapeDtypeStruct((batch, sub_len, dilation * ATT_OUT_W), BF16),
        compiler_params=pltpu.CompilerParams(
            dimension_semantics=("parallel", "parallel", "parallel"), vmem_limit_bytes=VMEM_LIMIT),
        name=f"dilated_attn_d{dilation}",
    )(qkv3, qkv3, qkv3, qkv3, qkv3)
    return out.reshape(batch * sub_len, dilation * ATT_OUT_W)


def _mem_kv_kernel(mem_ref, g_ref, w_ref, kg_ref, k_ref, v_ref):
    mn = _rms(mem_ref[0], g_ref[...]).astype(BF16)
    kv = jnp.dot(mn, w_ref[...], preferred_element_type=F32)
    for h in range(X_HEADS):
        sl = slice(h * HEAD_DIM, (h + 1) * HEAD_DIM)
        k_ref[0, :, sl] = _rms(kv[:, sl], kg_ref[...]).astype(BF16)
    v_ref[0] = kv[:, X_W:].astype(BF16)


def _mem_kv(mem, g_mem, w_ckv, ckg):
    batch, n_mem, _ = mem.shape
    out = jax.ShapeDtypeStruct((batch, n_mem, X_W), BF16)
    blk = pl.BlockSpec((1, n_mem, X_W), lambda b: (b, 0, 0))
    return pl.pallas_call(
        _mem_kv_kernel,
        grid=(batch,),
        in_specs=[pl.BlockSpec((1, n_mem, D_MODEL), lambda b: (b, 0, 0)), _resident((1, D_MODEL)),
                  _resident((D_MODEL, 2 * X_W)), _resident((1, HEAD_DIM))],
        out_specs=[blk, blk],
        out_shape=[out, out],
        compiler_params=pltpu.CompilerParams(dimension_semantics=("parallel",),
                                             vmem_limit_bytes=VMEM_LIMIT),
        name="mem_kv",
    )(mem, g_mem, w_ckv, ckg)


def _mix_cross_kernel(x_ref, a1_ref, a4_ref, a16_ref, y_ref, wo_ref,
                      gc_ref, wq_ref, qg_ref, km_ref, vm_ref, wc_ref, wgu_ref, wd_ref,
                      h_ref, wgu_o, wd_o, acc_tok, st_tok, cat_a, cat_b, o_ref, *, tm):
    step = pl.program_id(0)
    wgu_o[...] = wgu_ref[...].astype(BF16)
    wd_o[...] = wd_ref[...].astype(BF16)

    def merge(cat_ref):
        for gi, (a_ref, d) in enumerate(((a4_ref, 4), (a16_ref, 16))):
            n = tm // d
            for r in range(d):
                dst = pl.ds(r, n, stride=d)
                st_tok[gi, dst, :] = a_ref[:, r * ATT_OUT_W + KV_W:(r + 1) * ATT_OUT_W].astype(F32)
                for h in range(ATT_HEADS):
                    c0 = r * ATT_OUT_W + h * HEAD_DIM
                    acc_tok[gi, h, dst, :] = a_ref[:, c0:c0 + HEAD_DIM].astype(F32)

        for i in range(tm // MERGE_CHUNK):
            rows = slice(i * MERGE_CHUNK, (i + 1) * MERGE_CHUNK)
            stats = [a1_ref[rows, KV_W:].astype(F32), st_tok[0, rows, :], st_tok[1, rows, :]]
            pairs = [s + pltpu.roll(s, LANES - ATT_HEADS, 1) for s in stats]
            shifts = [p + pltpu.roll(s, LANES - 2 * ATT_HEADS, 1) for p, s in zip(pairs, stats)]
            dens = [pltpu.roll(p, LANES - 3 * ATT_HEADS, 1) for p in pairs]
            m_all = jnp.maximum(jnp.maximum(shifts[0], shifts[1]), shifts[2])
            wts = [jnp.exp2(s - m_all) for s in shifts]
            total = wts[0] * dens[0] + wts[1] * dens[1] + wts[2] * dens[2]
            coef = [w / total for w in wts]
            for h in range(ATT_HEADS):
                sl = slice(h * HEAD_DIM, (h + 1) * HEAD_DIM)
                accs = (a1_ref[rows, sl].astype(F32), acc_tok[0, h, rows, :], acc_tok[1, h, rows, :])
                num = jnp.zeros((MERGE_CHUNK, HEAD_DIM), F32)
                for g in range(len(DILATED_PAIRS)):
                    num = num + coef[g][:, h:h + 1] * accs[g]
                cat_ref[rows, sl] = num.astype(BF16)
        cat_ref[:, KV_W:] = y_ref[...]

    def mix(cat_ref):
        n_mem = km_ref.shape[1]
        h1 = x_ref[...] + jnp.dot(cat_ref[...], wo_ref[...], preferred_element_type=F32)
        hn = _rms(h1, gc_ref[...]).astype(BF16)
        qc = jnp.dot(hn, wq_ref[...], preferred_element_type=F32)
        for h in range(X_HEADS):
            sl = slice(h * HEAD_DIM, (h + 1) * HEAD_DIM)
            qh = _rms(qc[:, sl], qg_ref[...] * SCORE_SCALE).astype(BF16)
            s = lax.dot_general(qh, km_ref[0, :, sl], (((1,), (1,)), ((), ())),
                                preferred_element_type=F32)
            p = jnp.exp2(s - jnp.max(s, axis=-1, keepdims=True)).astype(BF16)
            v_ones = jnp.concatenate([vm_ref[0, :, sl], jnp.ones((n_mem, HEAD_DIM), BF16)], axis=1)
            o_l = jnp.dot(p, v_ones, preferred_element_type=F32)
            o_ref[:, sl] = (o_l[:, :HEAD_DIM] / o_l[:, HEAD_DIM:]).astype(BF16)
        h_ref[...] = h1 + jnp.dot(o_ref[...], wc_ref[...], preferred_element_type=F32)

    last = pl.num_programs(0) - 1

    @pl.when(step == 0)
    def _():
        merge(cat_a)

    @pl.when(jnp.logical_and(step % 2 == 0, jnp.logical_and(step > 0, step < last)))
    def _():
        merge(cat_a)
        mix(cat_b)

    @pl.when(step % 2 == 1)
    def _():
        merge(cat_b)
        mix(cat_a)

    @pl.when(step == last)
    def _():
        mix(cat_b)


def _mix_cross(x2, branches, y, w_out, g_cross, w_cq, cqg, k_mem, v_mem, w_co, w_gate_up, w_down, seq):
    n_tok = x2.shape[0]
    tm = TOKEN_TILE
    n_tiles = n_tok // tm
    assert n_tiles % 2 == 0
    tiles_per_seq = seq // tm
    n_mem = k_mem.shape[1]
    ahead = lambda i: (jnp.minimum(i, n_tiles - 1), 0)
    behind = lambda i: (jnp.maximum(i - 1, 0), 0)
    view = lambda d, w: pl.BlockSpec((tm // d, d * w), ahead)
    mem_blk = pl.BlockSpec((1, n_mem, X_W), lambda i: (jnp.maximum(i - 1, 0) // tiles_per_seq, 0, 0))

    def cast_blk(w, rows):
        n_blk = w.shape[0] // rows
        assert n_blk * rows == w.shape[0] and n_blk <= n_tiles + 1
        return pl.BlockSpec((rows, w.shape[1]), lambda i: (jnp.minimum(i, n_blk - 1), 0))

    cast_specs = [cast_blk(w_gate_up, 32), cast_blk(w_down, 128)]
    return pl.pallas_call(
        functools.partial(_mix_cross_kernel, tm=tm),
        grid=(n_tiles + 1,),
        in_specs=[pl.BlockSpec((tm, D_MODEL), behind)] + [view(d, ATT_OUT_W) for d in DILATIONS]
                 + [view(1, POOL_W), _resident((KV_W + POOL_W, D_MODEL)), _resident((1, D_MODEL)),
                    _resident((D_MODEL, X_W)), _resident((1, HEAD_DIM)), mem_blk, mem_blk,
                    _resident((X_W, D_MODEL))] + cast_specs,
        out_specs=[pl.BlockSpec((tm, D_MODEL), behind)] + cast_specs,
        out_shape=[jax.ShapeDtypeStruct((n_tok, D_MODEL), F32),
                   jax.ShapeDtypeStruct(w_gate_up.shape, BF16),
                   jax.ShapeDtypeStruct(w_down.shape, BF16)],
        scratch_shapes=[pltpu.VMEM((2, ATT_HEADS, tm, HEAD_DIM), F32),
                        pltpu.VMEM((2, tm, LANES), F32),
                        pltpu.VMEM((tm, KV_W + POOL_W), BF16),
                        pltpu.VMEM((tm, KV_W + POOL_W), BF16),
                        pltpu.VMEM((tm, X_W), BF16)],
        compiler_params=pltpu.CompilerParams(dimension_semantics=("arbitrary",),
                                             vmem_limit_bytes=VMEM_LIMIT),
        name="mix_cross",
    )(x2, *branches, y, w_out, g_cross, w_cq, cqg, k_mem, v_mem, w_co, w_gate_up, w_down)


def _ffn_kernel(h_ref, g_ref, wgu_ref, wd_ref, o_ref):
    h = h_ref[...]
    hn = _rms(h, g_ref[...]).astype(BF16)
    gate = jnp.dot(hn, wgu_ref[:, :D_FF], preferred_element_type=F32)
    up = jnp.dot(hn, wgu_ref[:, D_FF:], preferred_element_type=F32)
    act = (gate * jax.nn.sigmoid(gate) * up).astype(BF16)
    o_ref[...] = h + jnp.dot(act, wd_ref[...], preferred_element_type=F32)


def _ffn(h2, g_ffn, w_gate_up, w_down):
    n_tok = h2.shape[0]
    tm = TOKEN_TILE
    tok = pl.BlockSpec((tm, D_MODEL), lambda i: (i, 0))
    return pl.pallas_call(
        _ffn_kernel,
        grid=(n_tok // tm,),
        in_specs=[tok, _resident((1, D_MODEL)), _resident((D_MODEL, 2 * D_FF)),
                  _resident((D_FF, D_MODEL))],
        out_specs=tok,
        out_shape=jax.ShapeDtypeStruct((n_tok, D_MODEL), F32),
        compiler_params=pltpu.CompilerParams(dimension_semantics=("parallel",),
                                             vmem_limit_bytes=VMEM_LIMIT),
        name="ffn",
    )(h2, g_ffn, w_gate_up, w_down)


def kernel(x, mem, positions, mix_norm_g, w_in, q_norm_g, k_norm_g, pool_w, pool_scale, w_out,
           cross_norm_g, mem_norm_g, w_cq, w_ckv, cq_norm_g, ck_norm_g, w_co,
           ffn_norm_g, w_gate_up, w_down):
    batch, seq, _ = x.shape
    depth = w_in.shape[0]
    row = lambda a: a.reshape(1, -1)
    h = x.reshape(batch * seq, D_MODEL)
    for layer in range(depth):
        cos, sin, w_in_l = _prep(positions, w_in[layer])
        *qkv_views, y, w_out_l, w_cq_l, w_ckv_l, w_co_l = _in_proj(
            h, row(mix_norm_g[layer]), w_in_l, row(q_norm_g[layer]), row(k_norm_g[layer]), cos, sin,
            pool_w[layer], row(pool_scale[layer]),
            (w_out[layer], w_cq[layer], w_ckv[layer], w_co[layer]), seq)
        branches = [_dilated_attn(qkv, batch, seq, d) for qkv, d in zip(qkv_views, DILATIONS)]
        k_mem, v_mem = _mem_kv(mem, row(mem_norm_g[layer]), w_ckv_l, row(ck_norm_g[layer]))
        h, w_gu_l, w_down_l = _mix_cross(
            h, branches, y, w_out_l, row(cross_norm_g[layer]), w_cq_l,
            row(cq_norm_g[layer]), k_mem, v_mem, w_co_l, w_gate_up[layer], w_down[layer], seq)
        h = _ffn(h, row(ffn_norm_g[layer]), w_gu_l, w_down_l)
    return h.reshape(batch, seq, D_MODEL)
```

```python
import functools
import math

import jax
import jax.numpy as jnp
import numpy as np
from jax import lax
from jax.experimental import pallas as pl
from jax.experimental.pallas import tpu as pltpu

D_MODEL = 1024
HEAD_DIM = 128
ATT_HEADS = 4
DILATED_PAIRS = ((128, 1), (512, 4), (2048, 16))
DILATIONS = tuple(d for _, d in DILATED_PAIRS)
KV_W = ATT_HEADS * HEAD_DIM
POOL_WINDOWS = (2, 4, 8, 16)
POOL_W = len(POOL_WINDOWS) * HEAD_DIM
IN_W = (len(DILATED_PAIRS) + 2) * KV_W + POOL_W
ROT_DIM = HEAD_DIM // 4
ROT_HALF = ROT_DIM // 2
ROPE_THETA = 500000.0
X_HEADS = 4
X_W = X_HEADS * HEAD_DIM
D_FF = 2816
EPS = 1e-6
NEG_INF = -1e30
SCORE_SCALE = HEAD_DIM ** -0.5 * math.log2(math.e)
ATT_BLOCK = 128
ATT_STEP_QUERIES = 1024
ATT_INPUT_BUFFERS = 3
LANES = 128
SUBLANES = 8
BF16_TILE_ROWS = 2 * SUBLANES
ROT_PAIR_LANE = LANES // 2
TOKEN_TILE = 512
ROW_CHUNK = 256
MERGE_CHUNK = 512
POOL_HALO = 32

F32 = jnp.float32
BF16 = jnp.bfloat16

VMEM_LIMIT = 56 * 1024 * 1024


def _rms(x, g):
    ms = jnp.mean(x * x, axis=-1, keepdims=True)
    return x * lax.rsqrt(ms + EPS) * g


def _resident(shape):
    nd = len(shape)
    return pl.BlockSpec(shape, lambda *_: (0,) * nd, pipeline_mode=pl.Buffered(1))


def _to_rotary_lanes(w):
    n_mid = ROT_PAIR_LANE - ROT_HALF
    lane = lax.broadcasted_iota(jnp.int32, w.shape, w.ndim - 1)
    keep = jnp.logical_or(lane < ROT_HALF, lane >= ROT_DIM + n_mid)
    from_low = jnp.logical_and(lane >= ROT_PAIR_LANE, lane < ROT_PAIR_LANE + ROT_HALF)
    moved = jnp.where(from_low, pltpu.roll(w, n_mid, w.ndim - 1),
                      pltpu.roll(w, LANES - ROT_HALF, w.ndim - 1))
    return jnp.where(keep, w, moved)


def _prep_kernel(pos_ref, invf_ref, win_ref, ctab_ref, stab_ref, win_o, *, n_qk_heads):
    ang = pos_ref[...] * invf_ref[...]
    cos = jnp.cos(ang)
    sin = jnp.sin(ang)
    n_rows = ang.shape[0]
    per_row = LANES // ROT_HALF
    lane = lax.broadcasted_iota(jnp.int32, (n_rows, LANES), 1)
    first = lane < ROT_HALF
    second = jnp.logical_and(lane >= ROT_PAIR_LANE, lane < ROT_PAIR_LANE + ROT_HALF)
    half = per_row // 2
    for j in range(half):
        shift = (LANES - ROT_HALF * j) % LANES
        c_lo = pltpu.roll(cos, shift, 1) if shift else cos
        s_lo = pltpu.roll(sin, shift, 1) if shift else sin
        c_hi = pltpu.roll(c_lo, ROT_PAIR_LANE, 1)
        s_hi = pltpu.roll(s_lo, ROT_PAIR_LANE, 1)
        for tok, (ca, cb, sa, sb) in ((j, (c_lo, c_hi, s_lo, s_hi)), (j + half, (c_hi, c_lo, s_hi, s_lo))):
            dst = pl.ds(tok, n_rows, stride=per_row)
            ctab_ref[dst, :] = jnp.where(first, ca, jnp.where(second, cb, 1.0))
            stab_ref[dst, :] = jnp.where(first, -sa, jnp.where(second, sb, 0.0))

    for h in range(n_qk_heads):
        sl = slice(h * HEAD_DIM, (h + 1) * HEAD_DIM)
        win_o[:, sl] = _to_rotary_lanes(win_ref[:, sl]).astype(BF16)
    rest = slice(n_qk_heads * HEAD_DIM, IN_W)
    win_o[:, rest] = win_ref[:, rest].astype(BF16)


def _cast_specs(weights, n_steps):
    specs = []
    for w in weights:
        rows = -(-w.shape[0] // (n_steps * BF16_TILE_ROWS)) * BF16_TILE_ROWS
        n_blk = w.shape[0] // rows
        assert n_blk * rows == w.shape[0] and n_blk <= n_steps
        specs.append(pl.BlockSpec((rows, w.shape[1]), lambda i, n_blk=n_blk: (jnp.minimum(i, n_blk - 1), 0)))
    return specs


def _prep(positions, w_in):
    n_tok = positions.size
    per_row = LANES // ROT_HALF
    n_steps = 4
    rows = n_tok // per_row // n_steps
    pos = jnp.broadcast_to(positions.reshape(-1, per_row, 1).astype(F32),
                           (n_tok // per_row, per_row, ROT_HALF)).reshape(n_tok // per_row, LANES)
    inv_freq = ROPE_THETA ** (-jnp.arange(0, ROT_DIM, 2, dtype=F32) / ROT_DIM)
    invf = jnp.tile(inv_freq, per_row).reshape(1, LANES)
    weights = (w_in,)
    row_blk = lambda w: pl.BlockSpec((w.shape[0] // n_steps, w.shape[1]), lambda i: (i, 0))
    tab = pl.BlockSpec((rows * per_row, LANES), lambda i: (i, 0))
    return pl.pallas_call(
        functools.partial(_prep_kernel, n_qk_heads=(len(DILATED_PAIRS) + 1) * ATT_HEADS),
        grid=(n_steps,),
        in_specs=[pl.BlockSpec((rows, LANES), lambda i: (i, 0)), pl.BlockSpec((1, LANES), lambda i: (0, 0))]
                 + [row_blk(w) for w in weights],
        out_specs=[tab, tab] + [row_blk(w) for w in weights],
        out_shape=[jax.ShapeDtypeStruct((n_tok, LANES), F32)] * 2
                  + [jax.ShapeDtypeStruct(w.shape, BF16) for w in weights],
        compiler_params=pltpu.CompilerParams(dimension_semantics=("parallel",),
                                             vmem_limit_bytes=VMEM_LIMIT),
        name="prep",
    )(pos, invf, *weights)


def _in_proj_kernel(x_ref, g_ref, w_ref, qg_ref, kg_ref, cos_ref, sin_ref, pw_ref, ps_ref,
                    wout_ref, wcq_ref, wckv_ref, wco_ref,
                    tok_ref, view4_ref, view16_ref, y_ref, wout_o, wcq_o, wckv_o, wco_o,
                    xn_ref, tab_ref, pa_ref, pb_ref, hist_ref, lvl_ref, stage_ref, stage4_ref,
                    *, tm, tiles_per_seq):
    step = pl.program_id(0)
    for src, dst in ((wout_ref, wout_o), (wcq_ref, wcq_o), (wckv_ref, wckv_o), (wco_ref, wco_o)):
        dst[...] = src[...].astype(BF16)
    rc = ROW_CHUNK
    n_rc = tm // rc

    def project(p_ref):
        for i in range(n_rc):
            rows = slice(i * rc, (i + 1) * rc)
            xn_ref[rows, :] = _rms(x_ref[rows, :], g_ref[...]).astype(BF16)
        for c in range(IN_W // KV_W):
            cols = slice(c * KV_W, (c + 1) * KV_W)
            p_ref[:, cols] = jnp.dot(xn_ref[...], w_ref[:, cols], preferred_element_type=F32)

    def finish(p_ref):
        for gi, (gain_ref, scale) in enumerate(((qg_ref, SCORE_SCALE), (kg_ref, 1.0))):
            gain = _to_rotary_lanes(gain_ref[...] * scale)
            tab_ref[2 * gi] = cos_ref[...] * gain
            tab_ref[2 * gi + 1] = sin_ref[...] * pltpu.roll(gain, ROT_PAIR_LANE, 1)

        def emit(c, gi, part, to_tok, to_v4, to_v16, a):
            for h in range(ATT_HEADS):
                src = slice(c * KV_W + h * HEAD_DIM, c * KV_W + (h + 1) * HEAD_DIM)
                for i in range(n_rc):
                    rows = slice(i * rc, (i + 1) * rc)
                    t = p_ref[rows, src]
                    if gi >= 0:
                        rs = lax.rsqrt(jnp.mean(t * t, axis=-1, keepdims=True) + EPS)
                        t = (t * tab_ref[2 * gi, rows, :]
                             + pltpu.roll(t, ROT_PAIR_LANE, 1) * tab_ref[2 * gi + 1, rows, :]) * rs
                    if to_tok:
                        c0 = part * KV_W + h * HEAD_DIM
                        tok_ref[rows, c0:c0 + HEAD_DIM] = t.astype(BF16)
                    if to_v4 or to_v16:
                        stage_ref[a, h, rows, :] = t
                if not (to_v4 or to_v16):
                    continue
                n4 = tm // 4
                for b in range(4):
                    t4 = stage_ref[a, h, pl.ds(b, n4, stride=4), :]
                    if to_v4:
                        c0 = (part * 4 + b) * KV_W + h * HEAD_DIM
                        view4_ref[:, c0:c0 + HEAD_DIM] = t4.astype(BF16)
                    if to_v16:
                        stage4_ref[a, h, b] = t4
                if to_v16:
                    for b in range(4):
                        for a4 in range(4):
                            c0 = (part * 16 + 4 * a4 + b) * KV_W + h * HEAD_DIM
                            view16_ref[:, c0:c0 + HEAD_DIM] = (
                                stage4_ref[a, h, b, pl.ds(a4, n4 // 4, stride=4), :].astype(BF16))

        emit(3, 1, 1, True, True, True, 0)
        emit(4, -1, 2, True, True, True, 1)
        emit(2, 0, 0, False, False, True, 2)
        emit(1, 0, 0, False, True, False, 3)
        emit(0, 0, 0, True, False, False, 0)

        seq_tile = jnp.maximum(step - 1, 0) % tiles_per_seq
        end = POOL_HALO + tm
        hist_ref[0:POOL_HALO, :] = jnp.where(seq_tile == 0, 0.0, 1.0) * hist_ref[tm:end, :]
        u_cols = slice(IN_W - POOL_W, IN_W)
        hist_ref[POOL_HALO:end, :] = p_ref[:, u_cols]
        row16 = lax.broadcasted_iota(jnp.int32, (2 * SUBLANES, HEAD_DIM), 0)
        for g, w in enumerate(POOL_WINDOWS):
            sl = slice(g * HEAD_DIM, (g + 1) * HEAD_DIM)
            src, start, shift = hist_ref, SUBLANES, 1
            while True:
                cols = sl if src is hist_ref else slice(None)
                tot = src[start:end, cols] + src[start - shift:end - shift, cols]
                shift *= 2
                if shift == w:
                    break
                lvl_ref[g, start:end, :] = tot
                src, start = lvl_ref.at[g], start + SUBLANES
            tot = tot[POOL_HALO - start:, :]
            n_first = jnp.maximum(jnp.minimum(row16 + 1, w), jnp.where(seq_tile == 0, 0, w))
            inv_first = 1.0 / n_first.astype(F32)
            ug = hist_ref[POOL_HALO:end, sl]
            d = jnp.concatenate([tot[:2 * SUBLANES] * inv_first, tot[2 * SUBLANES:] * (1.0 / w)],
                                axis=0) - ug
            yg = jnp.dot(d.astype(BF16), pw_ref[g].astype(BF16),
                         preferred_element_type=F32) * ps_ref[:, sl]
            y_ref[:, sl] = yg.astype(BF16)

    last = pl.num_programs(0) - 1

    @pl.when(step == 0)
    def _():
        hist_ref[...] = jnp.zeros(hist_ref.shape, F32)
        project(pa_ref)

    @pl.when(jnp.logical_and(step % 2 == 0, jnp.logical_and(step > 0, step < last)))
    def _():
        project(pa_ref)
        finish(pb_ref)

    @pl.when(step % 2 == 1)
    def _():
        project(pb_ref)
        finish(pa_ref)

    @pl.when(step == last)
    def _():
        finish(pb_ref)


def _in_proj(x2, g_mix, w_in, qg, kg, cos, sin, pool_w, pool_scale, later_weights, seq):
    n_tok = x2.shape[0]
    tm = TOKEN_TILE
    n_tiles = n_tok // tm
    assert n_tiles % 2 == 0
    tiles_per_seq = seq // tm
    ahead = lambda i: (jnp.minimum(i, n_tiles - 1), 0)
    behind = lambda i: (jnp.maximum(i - 1, 0), 0)
    out_dw = [(d, 3 * KV_W) for d in DILATIONS] + [(1, POOL_W)]
    view = lambda d, w: pl.BlockSpec((tm // d, d * w), behind)
    view_sds = lambda d, w: jax.ShapeDtypeStruct((n_tok // d, d * w), BF16)
    cast_specs = _cast_specs(later_weights, n_tiles + 1)
    return pl.pallas_call(
        functools.partial(_in_proj_kernel, tm=tm, tiles_per_seq=tiles_per_seq),
        grid=(n_tiles + 1,),
        in_specs=[pl.BlockSpec((tm, D_MODEL), ahead), _resident((1, D_MODEL)),
                  _resident((D_MODEL, IN_W)), _resident((1, HEAD_DIM)), _resident((1, HEAD_DIM)),
                  pl.BlockSpec((tm, LANES), behind), pl.BlockSpec((tm, LANES), behind),
                  _resident((len(POOL_WINDOWS), HEAD_DIM, HEAD_DIM)), _resident((1, POOL_W))]
                 + cast_specs,
        out_specs=[view(d, w) for d, w in out_dw] + cast_specs,
        out_shape=[view_sds(d, w) for d, w in out_dw]
                  + [jax.ShapeDtypeStruct(w.shape, BF16) for w in later_weights],
        scratch_shapes=[pltpu.VMEM((tm, D_MODEL), BF16),
                        pltpu.VMEM((4, tm, LANES), F32),
                        pltpu.VMEM((tm, IN_W), F32),
                        pltpu.VMEM((tm, IN_W), F32),
                        pltpu.VMEM((POOL_HALO + tm, POOL_W), F32),
                        pltpu.VMEM((len(POOL_WINDOWS), POOL_HALO + tm, HEAD_DIM), F32),
                        pltpu.VMEM((4, ATT_HEADS, tm, HEAD_DIM), F32),
                        pltpu.VMEM((3, ATT_HEADS, 4, tm // 4, HEAD_DIM), F32)],
        compiler_params=pltpu.CompilerParams(dimension_semantics=("arbitrary",),
                                             vmem_limit_bytes=VMEM_LIMIT),
        name="in_proj",
    )(x2, g_mix, w_in, qg, kg, cos, sin, pool_w, pool_scale, *later_weights)


def _dilated_attn_kernel(nm_ref, q_ref, k_ref, kh_ref, v_ref, vh_ref, o_ref, st_ref, *, tl, rb):
    n_sub = tl // ATT_BLOCK
    qi = lax.broadcasted_iota(jnp.int32, (ATT_BLOCK, 2 * ATT_BLOCK), 0)
    kj = lax.broadcasted_iota(jnp.int32, (ATT_BLOCK, 2 * ATT_BLOCK), 1)
    delta = qi + ATT_BLOCK - kj
    in_band = jnp.logical_and(delta >= 0, delta <= ATT_BLOCK)
    bias = jnp.where(in_band, 0.0, NEG_INF).astype(F32)
    n_missing = nm_ref[0]
    bias_first = jnp.where(kj < n_missing, NEG_INF, bias)
    lane = lax.broadcasted_iota(jnp.int32, (ATT_BLOCK, LANES), 1)
    key_lane = lax.broadcasted_iota(jnp.int32, (2 * ATT_BLOCK, HEAD_DIM), 1)
    sum_blk = [jnp.where(key_lane == ATT_HEADS + h, 1.0, 0.0).astype(BF16) for h in range(ATT_HEADS)]

    for r in range(rb):
        for i in range(n_sub):
            rows = slice(i * ATT_BLOCK, (i + 1) * ATT_BLOCK)
            rows2 = slice((i - 1) * ATT_BLOCK, (i + 1) * ATT_BLOCK)
            shift = jnp.zeros((ATT_BLOCK, LANES), F32)
            den = jnp.zeros((ATT_BLOCK, LANES), F32)
            for h in range(ATT_HEADS):
                c0 = r * KV_W + h * HEAD_DIM
                sl = slice(c0, c0 + HEAD_DIM)
                if i == 0:
                    k_blk = jnp.concatenate([kh_ref[0, :, sl], k_ref[0, rows, sl]], axis=0)
                    v_blk = jnp.concatenate([vh_ref[0, :, sl], v_ref[0, rows, sl]], axis=0)
                    b = bias_first
                else:
                    k_blk, v_blk, b = k_ref[0, rows2, sl], v_ref[0, rows2, sl], bias
                s = lax.dot_general(q_ref[0, rows, sl], k_blk, (((1,), (1,)), ((), ())),
                                    preferred_element_type=F32) + b
                m = jnp.max(s, axis=-1, keepdims=True)
                p = jnp.exp2(s - m).astype(BF16)
                acc_l = jnp.dot(p, jnp.concatenate([v_blk, sum_blk[h]], axis=1),
                                preferred_element_type=F32)
                o_ref[0, rows, sl] = acc_l[:, :HEAD_DIM].astype(BF16)
                den = den + acc_l[:, HEAD_DIM:]
                shift = jnp.where(lane == h, m, shift)
            st_ref[0, rows, r * LANES:(r + 1) * LANES] = shift + den


def _dilated_attn(qkv, batch, seq, dilation):
    sub_len = seq // dilation
    tl = min(sub_len, ATT_STEP_QUERIES)
    rb = min(dilation, ATT_STEP_QUERIES // tl)
    n_sub = tl // ATT_BLOCK
    n_col = dilation // rb
    qkv3 = qkv.reshape(batch, sub_len, qkv.shape[1])
    cur = lambda part: pl.BlockSpec((1, tl, rb * KV_W), lambda b, r, i: (b, i, part * n_col + r),
                                    pipeline_mode=pl.Buffered(ATT_INPUT_BUFFERS))
    if sub_len > tl:
        halo_idx = lambda part: lambda b, r, i: (b, jnp.maximum(i * n_sub - 1, 0), part * n_col + r)
    else:
        halo_idx = lambda part: lambda b, r, i: (0, 0, part * n_col)
    halo = lambda part: pl.BlockSpec((1, ATT_BLOCK, rb * KV_W), halo_idx(part))
    out = lambda w: pl.BlockSpec((1, tl, rb * w), lambda b, r, i: (b, i, r))
    n_tiles = sub_len // tl
    n_missing = np.zeros((n_tiles, ATT_BLOCK, 2 * ATT_BLOCK), np.int32)
    n_missing[0] = ATT_BLOCK
    nm_spec = pl.BlockSpec((1, ATT_BLOCK, 2 * ATT_BLOCK), lambda b, r, i: (i, 0, 0))

    def streamed(nm_hbm, qkv_hbm, acc_hbm, st_hbm):
        pltpu.emit_pipeline(
            functools.partial(_dilated_attn_kernel, tl=tl, rb=rb),
            grid=(batch, n_col, n_tiles),
            in_specs=[nm_spec, cur(0), cur(1), halo(1), cur(2), halo(2)],
            out_specs=[out(KV_W), out(LANES)],
        )(nm_hbm, qkv_hbm, qkv_hbm, qkv_hbm, qkv_hbm, qkv_hbm, acc_hbm, st_hbm)

    hbm = pl.BlockSpec(memory_space=pl.ANY)
    acc, st = pl.pallas_call(
        streamed,
        in_specs=[hbm, hbm],
        out_specs=[hbm, hbm],
        out_shape=[jax.ShapeDtypeStruct((batch, sub_len, dilation * KV_W), BF16),
                   jax.ShapeDtypeStruct((batch, sub_len, dilation * LANES), F32)],
        compiler_params=pltpu.CompilerParams(vmem_limit_bytes=VMEM_LIMIT),
        name=f"dilated_attn_d{dilation}",
    )(n_missing, qkv3)
    return acc.reshape(batch * sub_len, dilation * KV_W), st.reshape(batch * sub_len, dilation * LANES)


def _mem_kv_kernel(mem_ref, g_ref, w_ref, kg_ref, k_ref, v_ref):
    mn = _rms(mem_ref[0], g_ref[...]).astype(BF16)
    kv = jnp.dot(mn, w_ref[...], preferred_element_type=F32)
    for h in range(X_HEADS):
        sl = slice(h * HEAD_DIM, (h + 1) * HEAD_DIM)
        k_ref[0, :, sl] = _rms(kv[:, sl], kg_ref[...]).astype(BF16)
    v_ref[0] = kv[:, X_W:].astype(BF16)


def _mem_kv(mem, g_mem, w_ckv, ckg):
    batch, n_mem, _ = mem.shape
    out = jax.ShapeDtypeStruct((batch, n_mem, X_W), BF16)
    blk = pl.BlockSpec((1, n_mem, X_W), lambda b: (b, 0, 0))
    return pl.pallas_call(
        _mem_kv_kernel,
        grid=(batch,),
        in_specs=[pl.BlockSpec((1, n_mem, D_MODEL), lambda b: (b, 0, 0)), _resident((1, D_MODEL)),
                  _resident((D_MODEL, 2 * X_W)), _resident((1, HEAD_DIM))],
        out_specs=[blk, blk],
        out_shape=[out, out],
        compiler_params=pltpu.CompilerParams(dimension_semantics=("parallel",),
                                             vmem_limit_bytes=VMEM_LIMIT),
        name="mem_kv",
    )(mem, g_mem, w_ckv, ckg)


def _mix_cross_kernel(x_ref, a1_ref, a4_ref, a16_ref, s1_ref, s4_ref, s16_ref, y_ref, wo_ref,
                      gc_ref, wq_ref, qg_ref, km_ref, vm_ref, wc_ref, wgu_ref, wd_ref,
                      h_ref, wgu_o, wd_o, acc_tok, st_tok, cat_a, cat_b, o_ref, *, tm):
    step = pl.program_id(0)
    wgu_o[...] = wgu_ref[...].astype(BF16)
    wd_o[...] = wd_ref[...].astype(BF16)

    def merge(cat_ref):
        for gi, (a_ref, s_ref, d) in enumerate(((a4_ref, s4_ref, 4), (a16_ref, s16_ref, 16))):
            n = tm // d
            for r in range(d):
                dst = pl.ds(r, n, stride=d)
                st_tok[gi, dst, :] = s_ref[:, r * LANES:(r + 1) * LANES]
                for h in range(ATT_HEADS):
                    c0 = r * KV_W + h * HEAD_DIM
                    acc_tok[gi, h, dst, :] = a_ref[:, c0:c0 + HEAD_DIM].astype(F32)

        for i in range(tm // MERGE_CHUNK):
            rows = slice(i * MERGE_CHUNK, (i + 1) * MERGE_CHUNK)
            stats = [s1_ref[rows, :], st_tok[0, rows, :], st_tok[1, rows, :]]
            m_all = jnp.maximum(jnp.maximum(stats[0], stats[1]), stats[2])
            wts = [jnp.exp2(s - m_all) for s in stats]
            dens = [pltpu.roll(s, LANES - ATT_HEADS, 1) for s in stats]
            total = wts[0] * dens[0] + wts[1] * dens[1] + wts[2] * dens[2]
            coef = [w / total for w in wts]
            for h in range(ATT_HEADS):
                sl = slice(h * HEAD_DIM, (h + 1) * HEAD_DIM)
                accs = (a1_ref[rows, sl].astype(F32), acc_tok[0, h, rows, :], acc_tok[1, h, rows, :])
                num = jnp.zeros((MERGE_CHUNK, HEAD_DIM), F32)
                for g in range(len(DILATED_PAIRS)):
                    num = num + coef[g][:, h:h + 1] * accs[g]
                cat_ref[rows, sl] = num.astype(BF16)
        cat_ref[:, KV_W:] = y_ref[...]

    def mix(cat_ref):
        n_mem = km_ref.shape[1]
        h1 = x_ref[...] + jnp.dot(cat_ref[...], wo_ref[...], preferred_element_type=F32)
        hn = _rms(h1, gc_ref[...]).astype(BF16)
        qc = jnp.dot(hn, wq_ref[...], preferred_element_type=F32)
        for h in range(X_HEADS):
            sl = slice(h * HEAD_DIM, (h + 1) * HEAD_DIM)
            qh = _rms(qc[:, sl], qg_ref[...] * SCORE_SCALE).astype(BF16)
            s = lax.dot_general(qh, km_ref[0, :, sl], (((1,), (1,)), ((), ())),
                                preferred_element_type=F32)
            p = jnp.exp2(s - jnp.max(s, axis=-1, keepdims=True)).astype(BF16)
            v_ones = jnp.concatenate([vm_ref[0, :, sl], jnp.ones((n_mem, HEAD_DIM), BF16)], axis=1)
            o_l = jnp.dot(p, v_ones, preferred_element_type=F32)
            o_ref[:, sl] = (o_l[:, :HEAD_DIM] / o_l[:, HEAD_DIM:]).astype(BF16)
        h_ref[...] = h1 + jnp.dot(o_ref[...], wc_ref[...], preferred_element_type=F32)

    last = pl.num_programs(0) - 1

    @pl.when(step == 0)
    def _():
        merge(cat_a)

    @pl.when(jnp.logical_and(step % 2 == 0, jnp.logical_and(step > 0, step < last)))
    def _():
        merge(cat_a)
        mix(cat_b)

    @pl.when(step % 2 == 1)
    def _():
        merge(cat_b)
        mix(cat_a)

    @pl.when(step == last)
    def _():
        mix(cat_b)


def _mix_cross(x2, accs, stats, y, w_out, g_cross, w_cq, cqg, k_mem, v_mem, w_co, w_gate_up, w_down,
               seq):
    n_tok = x2.shape[0]
    tm = TOKEN_TILE
    n_tiles = n_tok // tm
    assert n_tiles % 2 == 0
    tiles_per_seq = seq // tm
    n_mem = k_mem.shape[1]
    ahead = lambda i: (jnp.minimum(i, n_tiles - 1), 0)
    behind = lambda i: (jnp.maximum(i - 1, 0), 0)
    view = lambda d, w: pl.BlockSpec((tm // d, d * w), ahead)
    mem_blk = pl.BlockSpec((1, n_mem, X_W), lambda i: (jnp.maximum(i - 1, 0) // tiles_per_seq, 0, 0))

    def cast_blk(w, rows):
        n_blk = w.shape[0] // rows
        assert n_blk * rows == w.shape[0] and n_blk <= n_tiles + 1
        return pl.BlockSpec((rows, w.shape[1]), lambda i: (jnp.minimum(i, n_blk - 1), 0))

    cast_specs = [cast_blk(w_gate_up, 32), cast_blk(w_down, 128)]
    return pl.pallas_call(
        functools.partial(_mix_cross_kernel, tm=tm),
        grid=(n_tiles + 1,),
        in_specs=[pl.BlockSpec((tm, D_MODEL), behind)] + [view(d, KV_W) for d in DILATIONS]
                 + [view(d, LANES) for d in DILATIONS]
                 + [view(1, POOL_W), _resident((KV_W + POOL_W, D_MODEL)), _resident((1, D_MODEL)),
                    _resident((D_MODEL, X_W)), _resident((1, HEAD_DIM)), mem_blk, mem_blk,
                    _resident((X_W, D_MODEL))] + cast_specs,
        out_specs=[pl.BlockSpec((tm, D_MODEL), behind)] + cast_specs,
        out_shape=[jax.ShapeDtypeStruct((n_tok, D_MODEL), F32),
                   jax.ShapeDtypeStruct(w_gate_up.shape, BF16),
                   jax.ShapeDtypeStruct(w_down.shape, BF16)],
        scratch_shapes=[pltpu.VMEM((2, ATT_HEADS, tm, HEAD_DIM), F32),
                        pltpu.VMEM((2, tm, LANES), F32),
                        pltpu.VMEM((tm, KV_W + POOL_W), BF16),
                        pltpu.VMEM((tm, KV_W + POOL_W), BF16),
                        pltpu.VMEM((tm, X_W), BF16)],
        compiler_params=pltpu.CompilerParams(dimension_semantics=("arbitrary",),
                                             vmem_limit_bytes=VMEM_LIMIT),
        name="mix_cross",
    )(x2, *accs, *stats, y, w_out, g_cross, w_cq, cqg, k_mem, v_mem, w_co, w_gate_up, w_down)


def _ffn_kernel(h_ref, g_ref, wgu_ref, wd_ref, o_ref):
    h = h_ref[...]
    hn = _rms(h, g_ref[...]).astype(BF16)
    gate = jnp.dot(hn, wgu_ref[:, :D_FF], preferred_element_type=F32)
    up = jnp.dot(hn, wgu_ref[:, D_FF:], preferred_element_type=F32)
    act = (gate * jax.nn.sigmoid(gate) * up).astype(BF16)
    o_ref[...] = h + jnp.dot(act, wd_ref[...], preferred_element_type=F32)


def _ffn(h2, g_ffn, w_gate_up, w_down):
    n_tok = h2.shape[0]
    tm = TOKEN_TILE
    tok = pl.BlockSpec((tm, D_MODEL), lambda i: (i, 0))
    return pl.pallas_call(
        _ffn_kernel,
        grid=(n_tok // tm,),
        in_specs=[tok, _resident((1, D_MODEL)), _resident((D_MODEL, 2 * D_FF)),
                  _resident((D_FF, D_MODEL))],
        out_specs=tok,
        out_shape=jax.ShapeDtypeStruct((n_tok, D_MODEL), F32),
        compiler_params=pltpu.CompilerParams(dimension_semantics=("parallel",),
                                             vmem_limit_bytes=VMEM_LIMIT),
        name="ffn",
    )(h2, g_ffn, w_gate_up, w_down)


def kernel(x, mem, positions, mix_norm_g, w_in, q_norm_g, k_norm_g, pool_w, pool_scale, w_out,
           cross_norm_g, mem_norm_g, w_cq, w_ckv, cq_norm_g, ck_norm_g, w_co,
           ffn_norm_g, w_gate_up, w_down):
    batch, seq, _ = x.shape
    depth = w_in.shape[0]
    row = lambda a: a.reshape(1, -1)
    h = x.reshape(batch * seq, D_MODEL)
    for layer in range(depth):
        cos, sin, w_in_l = _prep(positions, w_in[layer])
        *qkv_views, y, w_out_l, w_cq_l, w_ckv_l, w_co_l = _in_proj(
            h, row(mix_norm_g[layer]), w_in_l, row(q_norm_g[layer]), row(k_norm_g[layer]), cos, sin,
            pool_w[layer], row(pool_scale[layer]),
            (w_out[layer], w_cq[layer], w_ckv[layer], w_co[layer]), seq)
        accs, stats = zip(*[_dilated_attn(qkv, batch, seq, d) for qkv, d in zip(qkv_views, DILATIONS)])
        k_mem, v_mem = _mem_kv(mem, row(mem_norm_g[layer]), w_ckv_l, row(ck_norm_g[layer]))
        h, w_gu_l, w_down_l = _mix_cross(
            h, accs, stats, y, w_out_l, row(cross_norm_g[layer]), w_cq_l,
            row(cq_norm_g[layer]), k_mem, v_mem, w_co_l, w_gate_up[layer], w_down[layer], seq)
        h = _ffn(h, row(ffn_norm_g[layer]), w_gu_l, w_down_l)
    return h.reshape(batch, seq, D_MODEL)
```
